```python
import math
import jax, jax.numpy as jnp
from jax import lax
import numpy as np

D_MODEL = 1024
BATCH = 4
SEQ = 8192
DEPTH = 2

CONV_WIDTH = D_MODEL // 4
CONV_KSIZE = 31
ML_HEADS = 4
ML_HEAD_DIM = D_MODEL // 16
ML_WIDTH = ML_HEADS * ML_HEAD_DIM
ML_QK_CONV = 4
ML_CHUNK = 128
DA_HEADS = 4
DA_QK_DIM = D_MODEL // 16
DA_V_DIM = 2 * DA_QK_DIM
DA_WIDTH = DA_HEADS * DA_V_DIM
Q_BLOCK = 128
D_FF = 4 * D_MODEL
SPLIT_SIZES = (CONV_WIDTH, CONV_WIDTH,
               2 * ML_WIDTH, ML_WIDTH, ML_WIDTH,
               ML_HEADS, ML_HEADS,
               DA_HEADS * 2 * DA_QK_DIM, DA_HEADS * 2 * DA_QK_DIM, DA_WIDTH)
SPLIT_POINTS = tuple(int(p) for p in np.cumsum(SPLIT_SIZES)[:-1])
D_IN_PROJ = sum(SPLIT_SIZES)
DEEPNORM_ALPHA = (2 * DEPTH) ** 0.25
DEEPNORM_BETA = (8 * DEPTH) ** -0.25
LN_EPS = 1e-5

kernel_name = 'hymba_conv_mlstm_diffattn_deepnorm'


def layer_norm(x, g, b=None):
    xf = x.astype(jnp.float32)
    mu = jnp.mean(xf, axis=-1, keepdims=True)
    var = jnp.mean(jnp.square(xf - mu), axis=-1, keepdims=True)
    y = ((xf - mu) * lax.rsqrt(var + LN_EPS)).astype(x.dtype) * g
    return y if b is None else y + b


def rms_norm(x, g):
    xf = x.astype(jnp.float32)
    y = xf * lax.rsqrt(jnp.mean(jnp.square(xf), axis=-1, keepdims=True) + LN_EPS)
    return y.astype(x.dtype) * g


def causal_depthwise_conv(x, w, b):
    ksize, ch = w.shape
    y = lax.conv_general_dilated(x, w[:, None, :].astype(x.dtype), window_strides=(1,),
                                 padding=((ksize - 1, 0),),
                                 dimension_numbers=('NWC', 'WIO', 'NWC'),
                                 feature_group_count=ch)
    return y + b


def mlstm_chunkwise(q, k, v, log_i, log_f):
    bsz, nh, s, dh = q.shape
    nc = s // ML_CHUNK
    f32 = jnp.float32
    q = q.astype(f32).reshape(bsz, nh, nc, ML_CHUNK, dh)
    k = (k.astype(f32) * dh ** -0.5).reshape(bsz, nh, nc, ML_CHUNK, dh)
    v = v.astype(f32).reshape(bsz, nh, nc, ML_CHUNK, dh)
    log_i = log_i.astype(f32).reshape(bsz, nh, nc, ML_CHUNK)
    log_f = log_f.astype(f32).reshape(bsz, nh, nc, ML_CHUNK)
    b = jnp.cumsum(log_f, axis=-1)
    g = b[..., -1]
    w_end = g[..., None] - b + log_i
    m_loc = jnp.max(w_end, axis=-1)
    p_end = jnp.exp(w_end - m_loc[..., None])
    c_loc = jnp.einsum('bhcsv,bhcsk->bhcvk', p_end[..., None] * v, k)
    n_loc = jnp.einsum('bhcs,bhcsk->bhck', p_end, k)

    def step(carry, xs):
        c, n, m = carry
        g_c, m_c, c_c, n_c = xs
        m_new = jnp.maximum(g_c + m, m_c)
        a = jnp.exp(g_c + m - m_new)
        e = jnp.exp(m_c - m_new)
        c_new = a[..., None, None] * c + e[..., None, None] * c_c
        n_new = a[..., None] * n + e[..., None] * n_c
        return (c_new, n_new, m_new), (c, n, m)

    init = (jnp.zeros((bsz, nh, dh, dh), f32), jnp.zeros((bsz, nh, dh), f32),
            jnp.zeros((bsz, nh), f32))
    xs = (jnp.moveaxis(g, 2, 0), jnp.moveaxis(m_loc, 2, 0),
          jnp.moveaxis(c_loc, 2, 0), jnp.moveaxis(n_loc, 2, 0))
    _, (c_prev, n_prev, m_prev) = lax.scan(step, init, xs)
    c_prev = jnp.moveaxis(c_prev, 0, 2)
    n_prev = jnp.moveaxis(n_prev, 0, 2)
    m_prev = jnp.moveaxis(m_prev, 0, 2)

    causal = jnp.tril(jnp.ones((ML_CHUNK, ML_CHUNK), dtype=bool))
    d = jnp.where(causal, b[..., :, None] - b[..., None, :] + log_i[..., None, :], -jnp.inf)
    inter = b + m_prev[..., None]
    m_pos = jnp.maximum(inter, jnp.max(d, axis=-1))
    a_inter = jnp.exp(inter - m_pos)
    sc = jnp.einsum('bhcjd,bhcsd->bhcjs', q, k) * jnp.exp(d - m_pos[..., None])
    num = (a_inter[..., None] * jnp.einsum('bhcvk,bhcjk->bhcjv', c_prev, q)
           + jnp.einsum('bhcjs,bhcsv->bhcjv', sc, v))
    den = a_inter * jnp.einsum('bhck,bhcjk->bhcj', n_prev, q) + jnp.sum(sc, axis=-1)
    h = num / jnp.maximum(jnp.abs(den), jnp.exp(-m_pos))[..., None]
    return h.reshape(bsz, nh, s, dh)


def diff_attention(q, k, v, lam, norm_g):
    bsz, s, nh, _, dk = q.shape
    dv = v.shape[-1]
    nqb = s // Q_BLOCK
    scale = dk ** -0.5
    kh = k.transpose(0, 2, 3, 1, 4)
    vh = v.transpose(0, 2, 1, 3)
    qb = q.transpose(0, 2, 3, 1, 4).reshape(bsz, nh, 2, nqb, Q_BLOCK, dk).transpose(3, 0, 1, 2, 4, 5)
    kpos = jnp.arange(s)

    def block(args):
        q_blk, i = args
        sc = jnp.einsum('bhmqd,bhmkd->bhmqk', q_blk, kh).astype(jnp.float32) * scale
        qpos = i * Q_BLOCK + jnp.arange(Q_BLOCK)
        sc = jnp.where(kpos[None, :] <= qpos[:, None], sc, -jnp.inf)
        p = jax.nn.softmax(sc, axis=-1)
        w = p[:, :, 0] - lam * p[:, :, 1]
        return jnp.einsum('bhqk,bhkv->bhqv', w.astype(vh.dtype), vh)

    o = lax.map(block, (qb, jnp.arange(nqb)))
    o = o.transpose(1, 0, 3, 2, 4).reshape(bsz, s, nh, dv)
    return rms_norm(o, norm_g)


def hybrid_mixer(x, w_in, b_igate, b_fgate, conv_dw_w, conv_dw_b, conv_ln_g, conv_ln_b,
                 conv_pw_w, conv_pw_b, ml_conv_w, ml_conv_b, ml_norm_g,
                 lam_q1, lam_k1, lam_q2, lam_k2, da_norm_g, w_out, lambda_init):
    bsz, s, _ = x.shape
    u = x @ w_in
    (c_a, c_g, m_qk, m_v, m_o, m_i, m_f, d_q, d_k, d_v) = jnp.split(u, SPLIT_POINTS, axis=-1)

    y = c_a * jax.nn.sigmoid(c_g)
    y = causal_depthwise_conv(y, conv_dw_w, conv_dw_b)
    y = jax.nn.silu(layer_norm(y, conv_ln_g, conv_ln_b))
    y_conv = y @ conv_pw_w + conv_pw_b

    qk = jax.nn.silu(causal_depthwise_conv(m_qk, ml_conv_w, ml_conv_b))
    q_m, k_m = jnp.split(qk, 2, axis=-1)
    to_heads = lambda t: t.reshape(bsz, s, ML_HEADS, ML_HEAD_DIM).transpose(0, 2, 1, 3)
    log_i = (m_i + b_igate).astype(jnp.float32).transpose(0, 2, 1)
    log_f = jax.nn.log_sigmoid((m_f + b_fgate).astype(jnp.float32)).transpose(0, 2, 1)
    h = mlstm_chunkwise(to_heads(q_m), to_heads(k_m), to_heads(m_v), log_i, log_f)
    h = h.transpose(0, 2, 1, 3).astype(x.dtype) * jax.nn.sigmoid(m_o).reshape(bsz, s, ML_HEADS, ML_HEAD_DIM)
    y_ml = layer_norm(h, ml_norm_g.reshape(ML_HEADS, ML_HEAD_DIM)).reshape(bsz, s, ML_WIDTH)

    lam = (jnp.exp(jnp.sum(lam_q1.astype(jnp.float32) * lam_k1.astype(jnp.float32)))
           - jnp.exp(jnp.sum(lam_q2.astype(jnp.float32) * lam_k2.astype(jnp.float32)))
           + lambda_init)
    y_da = diff_attention(d_q.reshape(bsz, s, DA_HEADS, 2, DA_QK_DIM),
                          d_k.reshape(bsz, s, DA_HEADS, 2, DA_QK_DIM),
                          d_v.reshape(bsz, s, DA_HEADS, DA_V_DIM), lam, da_norm_g)
    y_da = (y_da * (1.0 - lambda_init)).reshape(bsz, s, DA_WIDTH)

    return jnp.concatenate([y_conv, y_ml, y_da], axis=-1) @ w_out


def setup_inputs(seed: int = 0) -> dict:
    key = jax.random.key(seed)
    ks = jax.random.split(key, 26)
    L = DEPTH
    nrm = lambda k, shape, scale: jax.random.normal(k, shape, jnp.float32) * scale
    x = nrm(ks[0], (BATCH, SEQ, D_MODEL), 1.0)
    w_in = nrm(ks[1], (L, D_MODEL, D_IN_PROJ), D_MODEL ** -0.5)
    b_igate = nrm(ks[2], (L, ML_HEADS), 0.1)
    b_fgate = (jnp.broadcast_to(jnp.linspace(3.0, 6.0, ML_HEADS, dtype=jnp.float32), (L, ML_HEADS))
               + nrm(ks[3], (L, ML_HEADS), 0.01))
    conv_dw_w = nrm(ks[4], (L, CONV_KSIZE, CONV_WIDTH), CONV_KSIZE ** -0.5)
    conv_dw_b = nrm(ks[5], (L, CONV_WIDTH), 0.01)
    conv_ln_g = 1.0 + nrm(ks[6], (L, CONV_WIDTH), 0.01)
    conv_ln_b = nrm(ks[7], (L, CONV_WIDTH), 0.01)
    conv_pw_w = nrm(ks[8], (L, CONV_WIDTH, CONV_WIDTH), CONV_WIDTH ** -0.5 * DEEPNORM_BETA)
    conv_pw_b = nrm(ks[9], (L, CONV_WIDTH), 0.01)
    ml_conv_w = nrm(ks[10], (L, ML_QK_CONV, 2 * ML_WIDTH), ML_QK_CONV ** -0.5)
    ml_conv_b = nrm(ks[11], (L, 2 * ML_WIDTH), 0.01)
    ml_norm_g = 1.0 + nrm(ks[12], (L, ML_WIDTH), 0.01)
    lam_q1 = nrm(ks[13], (L, DA_QK_DIM), 0.1)
    lam_k1 = nrm(ks[14], (L, DA_QK_DIM), 0.1)
    lam_q2 = nrm(ks[15], (L, DA_QK_DIM), 0.1)
    lam_k2 = nrm(ks[16], (L, DA_QK_DIM), 0.1)
    da_norm_g = 1.0 + nrm(ks[17], (L, DA_V_DIM), 0.01)
    w_out = nrm(ks[18], (L, D_MODEL, D_MODEL), D_MODEL ** -0.5 * DEEPNORM_BETA)
    ln1_g = 1.0 + nrm(ks[19], (L, D_MODEL), 0.01)
    ln1_b = nrm(ks[20], (L, D_MODEL), 0.01)
    w_up = nrm(ks[21], (L, D_MODEL, D_FF), D_MODEL ** -0.5 * DEEPNORM_BETA)
    w_down = nrm(ks[22], (L, D_FF, D_MODEL), D_FF ** -0.5 * DEEPNORM_BETA)
    ln2_g = 1.0 + nrm(ks[23], (L, D_MODEL), 0.01)
    ln2_b = nrm(ks[24], (L, D_MODEL), 0.01)
    return {'x': x, 'w_in': w_in, 'b_igate': b_igate, 'b_fgate': b_fgate,
            'conv_dw_w': conv_dw_w, 'conv_dw_b': conv_dw_b, 'conv_ln_g': conv_ln_g,
            'conv_ln_b': conv_ln_b, 'conv_pw_w': conv_pw_w, 'conv_pw_b': conv_pw_b,
            'ml_conv_w': ml_conv_w, 'ml_conv_b': ml_conv_b, 'ml_norm_g': ml_norm_g,
            'lam_q1': lam_q1, 'lam_k1': lam_k1, 'lam_q2': lam_q2, 'lam_k2': lam_k2,
            'da_norm_g': da_norm_g, 'w_out': w_out, 'ln1_g': ln1_g, 'ln1_b': ln1_b,
            'w_up': w_up, 'w_down': w_down, 'ln2_g': ln2_g, 'ln2_b': ln2_b}


def reference(x, w_in, b_igate, b_fgate, conv_dw_w, conv_dw_b, conv_ln_g, conv_ln_b,
              conv_pw_w, conv_pw_b, ml_conv_w, ml_conv_b, ml_norm_g,
              lam_q1, lam_k1, lam_q2, lam_k2, da_norm_g, w_out, ln1_g, ln1_b,
              w_up, w_down, ln2_g, ln2_b):
    for l in range(DEPTH):
        lambda_init = 0.8 - 0.6 * math.exp(-0.3 * l)
        h = hybrid_mixer(x, w_in[l], b_igate[l], b_fgate[l], conv_dw_w[l], conv_dw_b[l],
                         conv_ln_g[l], conv_ln_b[l], conv_pw_w[l], conv_pw_b[l],
                         ml_conv_w[l], ml_conv_b[l], ml_norm_g[l],
                         lam_q1[l], lam_k1[l], lam_q2[l], lam_k2[l], da_norm_g[l],
                         w_out[l], lambda_init)
        x = layer_norm(DEEPNORM_ALPHA * x + h, ln1_g[l], ln1_b[l])
        h = jnp.square(jax.nn.relu(x @ w_up[l])) @ w_down[l]
        x = layer_norm(DEEPNORM_ALPHA * x + h, ln2_g[l], ln2_b[l])
    return x
```

```python
import functools
import math

import numpy as np
import jax
import jax.numpy as jnp
from jax import lax
from jax.experimental import pallas as pl
from jax.experimental.pallas import tpu as pltpu

F32 = jnp.float32
BF16 = jnp.bfloat16

LANES = 128
SUBLANES = 8
VMEM_LIMIT_BYTES = 56 * 1024 * 1024

DEPTH = 2
CONV_KSIZE = 31
ML_HEADS = 4
ML_HEAD_DIM = 64
ML_QK_CONV = 4
ML_CHUNK = 128
DA_HEADS = 4
DA_QK_DIM = 64
DA_V_DIM = 128
LN_EPS = 1e-5
DEEPNORM_ALPHA = (2 * DEPTH) ** 0.25

PROJ_ROWS = 512
CONV_ROWS = 512
CONV_HALO = 32
CONV_SUB = 64
ATT_Q = 512
ATT_K = 512
MLP_ROWS = 1024
MLP_FF = 512


def _params(*semantics):
    return pltpu.CompilerParams(dimension_semantics=semantics,
                                vmem_limit_bytes=VMEM_LIMIT_BYTES)


def _sigmoid(x):
    return 1.0 / (1.0 + jnp.exp(-x))


def _layer_norm_rows(x, g, b):
    mu = jnp.mean(x, axis=-1, keepdims=True)
    xc = x - mu
    var = jnp.mean(xc * xc, axis=-1, keepdims=True)
    return xc * lax.rsqrt(var + LN_EPS) * g + b


def _in_proj_kernel(x_ref, w_ref, glu_ref, mqk_ref, mv_ref, mo_ref, gate_ref,
                    dq_ref, dk_ref, dv_ref):
    xb = x_ref[...].astype(BF16)

    def sec(lo, width):
        return jnp.dot(xb, w_ref[:, lo:lo + width], preferred_element_type=F32)

    glu_ref[...] = sec(0, 256) * _sigmoid(sec(256, 256))
    mqk_ref[...] = sec(512, 512)
    mv_ref[...] = sec(1024, 256)
    mo_ref[...] = sec(1280, 256)
    dq_ref[...] = (sec(1536, 512) * (DA_QK_DIM ** -0.5)).astype(BF16)
    dk_ref[...] = sec(2048, 512).astype(BF16)
    dv_ref[...] = sec(2560, 512).astype(BF16)
    gate_ref[...] = sec(3072, LANES)


def _in_proj(x2, w_r):
    n, d = x2.shape
    tm = PROJ_ROWS
    widths = (256, 512, 256, 256, LANES, 512, 512, 512)
    dtypes = (F32, F32, F32, F32, F32, BF16, BF16, BF16)
    return pl.pallas_call(
        _in_proj_kernel,
        grid=(n // tm,),
        in_specs=[pl.BlockSpec((tm, d), lambda i: (i, 0)),
                  pl.BlockSpec(w_r.shape, lambda i: (0, 0))],
        out_specs=[pl.BlockSpec((tm, w), lambda i: (i, 0)) for w in widths],
        out_shape=[jax.ShapeDtypeStruct((n, w), dt) for w, dt in zip(widths, dtypes)],
        compiler_params=_params("parallel"),
        name="in_proj",
    )(x2, w_r)


def _conv_kernel(x_ref, halo_ref, dw_ref, dwb_ref, g_ref, b_ref, pw_ref, pwb_ref,
                 out_ref, xpad, ybuf):
    s = pl.program_id(1)
    rows = x_ref.shape[1]
    xpad[0:CONV_HALO, :] = jnp.where(s > 0, halo_ref[0], 0.0)
    xpad[CONV_HALO:CONV_HALO + rows, :] = x_ref[0]
    first = CONV_HALO - (CONV_KSIZE - 1)
    for r in range(0, rows, CONV_SUB):
        acc = jnp.broadcast_to(dwb_ref[...], (CONV_SUB, x_ref.shape[2]))
        for j in range(CONV_KSIZE):
            lo = r + first + j
            acc = acc + dw_ref[j:j + 1, :] * xpad[lo:lo + CONV_SUB, :]
        y = _layer_norm_rows(acc, g_ref[...], b_ref[...])
        ybuf[r:r + CONV_SUB, :] = (y * _sigmoid(y)).astype(BF16)
    out = jnp.dot(ybuf[...], pw_ref[...], preferred_element_type=F32) + pwb_ref[...]
    out_ref[0] = out.astype(BF16)


def _conv_module(glu, dw, dwb, g, b, pw, pwb):
    bsz, s, ch = glu.shape
    ts = CONV_ROWS
    halo_per_tile = ts // CONV_HALO
    return pl.pallas_call(
        _conv_kernel,
        grid=(bsz, s // ts),
        in_specs=[
            pl.BlockSpec((1, ts, ch), lambda bi, si: (bi, si, 0)),
            pl.BlockSpec((1, CONV_HALO, ch),
                         lambda bi, si: (bi, jnp.maximum(si * halo_per_tile - 1, 0), 0)),
            pl.BlockSpec(dw.shape, lambda bi, si: (0, 0)),
            pl.BlockSpec(dwb.shape, lambda bi, si: (0, 0)),
            pl.BlockSpec(g.shape, lambda bi, si: (0, 0)),
            pl.BlockSpec(b.shape, lambda bi, si: (0, 0)),
            pl.BlockSpec(pw.shape, lambda bi, si: (0, 0)),
            pl.BlockSpec(pwb.shape, lambda bi, si: (0, 0)),
        ],
        out_specs=pl.BlockSpec((1, ts, ch), lambda bi, si: (bi, si, 0)),
        out_shape=jax.ShapeDtypeStruct((bsz, s, ch), BF16),
        scratch_shapes=[pltpu.VMEM((CONV_HALO + ts, ch), F32),
                        pltpu.VMEM((ts, ch), BF16)],
        compiler_params=_params("parallel", "parallel"),
        name="conv_module",
    )(glu, glu, dw, dwb, g, b, pw, pwb)


def _mlstm_kernel(qk_ref, halo_ref, v_ref, o_ref, gate_ref, cw_ref, cb_ref, gb_ref,
                  ng_ref, out_ref, qpad, ct_ref, m_ref):
    c = pl.program_id(1)
    t = ML_CHUNK

    @pl.when(c == 0)
    def _():
        ct_ref[...] = jnp.zeros_like(ct_ref)
        m_ref[...] = jnp.zeros_like(m_ref)

    qpad[0:SUBLANES, :] = jnp.where(c > 0, halo_ref[0], 0.0)
    qpad[SUBLANES:SUBLANES + t, :] = qk_ref[0]
    first = SUBLANES - (ML_QK_CONV - 1)
    acc = jnp.broadcast_to(cb_ref[...], (t, qk_ref.shape[2]))
    for j in range(ML_QK_CONV):
        acc = acc + cw_ref[j:j + 1, :] * qpad[first + j:first + j + t, :]
    qk = acc * _sigmoid(acc)

    gpre = gate_ref[0] + gb_ref[...]
    log_f = jnp.minimum(gpre, 0.0) - jnp.log(1.0 + jnp.exp(-jnp.abs(gpre)))
    row = lax.broadcasted_iota(jnp.int32, (t, t), 0)
    col = lax.broadcasted_iota(jnp.int32, (t, t), 1)
    causal = row >= col
    tri = causal.astype(F32)
    bcum = jnp.dot(tri, log_f, preferred_element_type=F32, precision=lax.Precision.HIGHEST)
    gates_t = jnp.where(col < ML_HEADS, gpre, bcum).T

    w_pair = 2 * ML_HEAD_DIM
    k_off = ML_HEADS * ML_HEAD_DIM
    lane = lax.broadcasted_iota(jnp.int32, (t, w_pair), 1)
    for pair in range(ML_HEADS // 2):
        lo = pair * w_pair
        q_pair = qk[:, lo:lo + w_pair]
        k_pair = qk[:, k_off + lo:k_off + lo + w_pair] * (ML_HEAD_DIM ** -0.5)
        v_pair = v_ref[0, :, lo:lo + w_pair]
        k_b = k_pair.astype(BF16)
        h_pair = None
        for par in range(2):
            h = 2 * pair + par
            own = (lane >= ML_HEAD_DIM) if par else (lane < ML_HEAD_DIM)
            b_col = bcum[:, ML_HEADS + h:ML_HEADS + h + 1]
            li_col = gpre[:, h:h + 1]
            b_row = gates_t[ML_HEADS + h:ML_HEADS + h + 1, :]
            li_row = gates_t[h:h + 1, :]
            m_prev = m_ref[h:h + 1, 0:1]
            ct_prev = ct_ref[h]

            d = jnp.where(causal, b_col - b_row + li_row, -jnp.inf)
            inter = b_col + m_prev
            m_pos = jnp.maximum(inter, jnp.max(d, axis=-1, keepdims=True))
            a_inter = jnp.exp(inter - m_pos)
            q_m = jnp.where(own, q_pair, 0.0).astype(BF16)
            vx = jnp.where(own, v_pair, 1.0).astype(BF16)
            sc = lax.dot_general(q_m, k_b, (((1,), (1,)), ((), ())),
                                 preferred_element_type=F32) * jnp.exp(d - m_pos)
            numden = (a_inter * jnp.dot(q_m, ct_prev.astype(BF16), preferred_element_type=F32)
                      + jnp.dot(sc.astype(BF16), vx, preferred_element_type=F32))
            den = pltpu.roll(numden, ML_HEAD_DIM, 1)
            hh = numden / jnp.maximum(jnp.abs(den), jnp.exp(-m_pos))
            h_pair = hh if h_pair is None else jnp.where(own, hh, h_pair)

            g_end = b_col[t - 1:t, :]
            w_end = g_end - b_col + li_col
            m_loc = jnp.max(w_end, axis=0, keepdims=True)
            p_end = jnp.exp(w_end - m_loc)
            m_new = jnp.maximum(g_end + m_prev, m_loc)
            a_st = jnp.exp(g_end + m_prev - m_new)
            e_st = jnp.exp(m_loc - m_new)
            kp = (jnp.where(own, k_pair, 0.0) * p_end).astype(BF16)
            c_loc = lax.dot_general(kp, vx, (((0,), (0,)), ((), ())),
                                    preferred_element_type=F32)
            ct_ref[h] = a_st * ct_prev + e_st * c_loc
            m_ref[h:h + 1, :] = jnp.broadcast_to(m_new, (1, LANES))

        x = h_pair * _sigmoid(o_ref[0, :, lo:lo + w_pair])
        low = lane < ML_HEAD_DIM
        inv = 1.0 / ML_HEAD_DIM
        s_low = jnp.sum(jnp.where(low, x, 0.0), axis=-1, keepdims=True)
        s_all = jnp.sum(x, axis=-1, keepdims=True)
        xc = x - jnp.where(low, s_low, s_all - s_low) * inv
        sq = xc * xc
        v_low = jnp.sum(jnp.where(low, sq, 0.0), axis=-1, keepdims=True)
        v_all = jnp.sum(sq, axis=-1, keepdims=True)
        var = jnp.where(low, v_low, v_all - v_low) * inv
        y = xc * lax.rsqrt(var + LN_EPS) * ng_ref[:, lo:lo + w_pair]
        out_ref[0, :, lo:lo + w_pair] = y.astype(BF16)


def _mlstm(mqk, mv, mo, gates, cw, cb, gb, ng):
    bsz, s, wqk = mqk.shape
    wv = mv.shape[2]
    t = ML_CHUNK
    halo_per_chunk = t // SUBLANES
    const = lambda bi, ci: (0, 0)
    return pl.pallas_call(
        _mlstm_kernel,
        grid=(bsz, s // t),
        in_specs=[
            pl.BlockSpec((1, t, wqk), lambda bi, ci: (bi, ci, 0)),
            pl.BlockSpec((1, SUBLANES, wqk),
                         lambda bi, ci: (bi, jnp.maximum(ci * halo_per_chunk - 1, 0), 0)),
            pl.BlockSpec((1, t, wv), lambda bi, ci: (bi, ci, 0)),
            pl.BlockSpec((1, t, wv), lambda bi, ci: (bi, ci, 0)),
            pl.BlockSpec((1, t, LANES), lambda bi, ci: (bi, ci, 0)),
            pl.BlockSpec(cw.shape, const),
            pl.BlockSpec(cb.shape, const),
            pl.BlockSpec(gb.shape, const),
            pl.BlockSpec(ng.shape, const),
        ],
        out_specs=pl.BlockSpec((1, t, wv), lambda bi, ci: (bi, ci, 0)),
        out_shape=jax.ShapeDtypeStruct((bsz, s, wv), BF16),
        scratch_shapes=[pltpu.VMEM((SUBLANES + t, wqk), F32),
                        pltpu.VMEM((ML_HEADS, 2 * ML_HEAD_DIM, LANES), F32),
                        pltpu.VMEM((SUBLANES, LANES), F32)],
        compiler_params=_params("parallel", "arbitrary"),
        name="mlstm",
    )(mqk, mqk, mv, mo, gates, cw, cb, gb, ng)


def _attn_kernel(qi_ref, kj_ref, q_ref, k_ref, v_ref, lam_ref, ng_ref, out_ref,
                 qm_ref, m_ref, l_ref, acc_ref, *, lambda_init):
    p = pl.program_id(2)
    qi = qi_ref[p]
    kj = kj_ref[p]
    tq = q_ref.shape[1]
    tk = k_ref.shape[1]

    @pl.when(kj == 0)
    def _():
        q = q_ref[0]
        lane = lax.broadcasted_iota(jnp.int32, q.shape, 1)
        zero = jnp.zeros_like(q)
        qm_ref[0] = jnp.where(lane < DA_QK_DIM, q, zero)
        qm_ref[1] = jnp.where(lane >= DA_QK_DIM, q, zero)
        m_ref[...] = jnp.full_like(m_ref, -jnp.inf)
        l_ref[...] = jnp.zeros_like(l_ref)
        acc_ref[...] = jnp.zeros_like(acc_ref)

    def update(masked):
        k = k_ref[0]
        v = v_ref[0]
        if masked:
            row = lax.broadcasted_iota(jnp.int32, (tq, tk), 0)
            col = lax.broadcasted_iota(jnp.int32, (tq, tk), 1)
            keep = col <= row
        for mp in range(2):
            s = lax.dot_general(qm_ref[mp], k, (((1,), (1,)), ((), ())),
                                preferred_element_type=F32)
            if masked:
                s = jnp.where(keep, s, -jnp.inf)
            m_old = m_ref[mp]
            m_new = jnp.maximum(m_old, jnp.max(s, axis=-1, keepdims=True))
            alpha = jnp.exp(m_old - m_new)
            pexp = jnp.exp(s - m_new[:, 0:1])
            l_ref[mp] = alpha * l_ref[mp] + jnp.sum(pexp, axis=-1, keepdims=True)
            acc_ref[mp] = alpha * acc_ref[mp] + jnp.dot(pexp.astype(BF16), v,
                                                       preferred_element_type=F32)
            m_ref[mp] = m_new

    @pl.when(kj < qi)
    def _():
        update(False)

    @pl.when(kj == qi)
    def _():
        update(True)
        lp = lam_ref[...]
        lam = (jnp.exp(jnp.sum(lp[0:1, :] * lp[1:2, :], axis=-1, keepdims=True))
               - jnp.exp(jnp.sum(lp[2:3, :] * lp[3:4, :], axis=-1, keepdims=True))
               + lambda_init)
        o = acc_ref[0] / l_ref[0] - lam * (acc_ref[1] / l_ref[1])
        ms = jnp.mean(o * o, axis=-1, keepdims=True)
        y = o * lax.rsqrt(ms + LN_EPS) * ng_ref[...]
        out_ref[0] = (y * (1.0 - lambda_init)).astype(BF16)


def _diff_attention(dq, dk, dv, lam_p, ng, lambda_init):
    bsz, s, _ = dq.shape
    tq, tk = ATT_Q, ATT_K
    assert tq == tk
    nq = s // tq
    qi = np.concatenate([np.full(i + 1, i, np.int32) for i in range(nq)])
    kj = np.concatenate([np.arange(i + 1, dtype=np.int32) for i in range(nq)])
    grid_spec = pltpu.PrefetchScalarGridSpec(
        num_scalar_prefetch=2,
        grid=(bsz, DA_HEADS, qi.shape[0]),
        in_specs=[
            pl.BlockSpec((1, tq, DA_V_DIM), lambda b, h, p, qi_r, kj_r: (b, qi_r[p], h)),
            pl.BlockSpec((1, tk, DA_V_DIM), lambda b, h, p, qi_r, kj_r: (b, kj_r[p], h)),
            pl.BlockSpec((1, tk, DA_V_DIM), lambda b, h, p, qi_r, kj_r: (b, kj_r[p], h)),
            pl.BlockSpec(lam_p.shape, lambda b, h, p, qi_r, kj_r: (0, 0)),
            pl.BlockSpec(ng.shape, lambda b, h, p, qi_r, kj_r: (0, 0)),
        ],
        out_specs=pl.BlockSpec((1, tq, DA_V_DIM), lambda b, h, p, qi_r, kj_r: (b, qi_r[p], h)),
        scratch_shapes=[pltpu.VMEM((2, tq, DA_V_DIM), BF16),
                        pltpu.VMEM((2, tq, LANES), F32),
                        pltpu.VMEM((2, tq, LANES), F32),
                        pltpu.VMEM((2, tq, DA_V_DIM), F32)],
    )
    return pl.pallas_call(
        functools.partial(_attn_kernel, lambda_init=lambda_init),
        grid_spec=grid_spec,
        out_shape=jax.ShapeDtypeStruct((bsz, s, DA_HEADS * DA_V_DIM), BF16),
        compiler_params=_params("parallel", "parallel", "arbitrary"),
        name="diff_attention",
    )(jnp.asarray(qi), jnp.asarray(kj), dq, dk, dv, lam_p, ng)


def _out_proj_kernel(yc_ref, ym_ref, yd_ref, x_ref, w_ref, g_ref, b_ref, out_ref):
    wc = yc_ref.shape[1]
    wm = ym_ref.shape[1]
    h = jnp.dot(yc_ref[...], w_ref[0:wc, :], preferred_element_type=F32)
    h = h + jnp.dot(ym_ref[...], w_ref[wc:wc + wm, :], preferred_element_type=F32)
    h = h + jnp.dot(yd_ref[...], w_ref[wc + wm:, :], preferred_element_type=F32)
    out_ref[...] = _layer_norm_rows(DEEPNORM_ALPHA * x_ref[...] + h, g_ref[...], b_ref[...])


def _out_proj(yc, ym, yd, x2, w, g, b):
    n, d = x2.shape
    tm = PROJ_ROWS
    const = lambda i: (0, 0)
    return pl.pallas_call(
        _out_proj_kernel,
        grid=(n // tm,),
        in_specs=[pl.BlockSpec((tm, yc.shape[1]), lambda i: (i, 0)),
                  pl.BlockSpec((tm, ym.shape[1]), lambda i: (i, 0)),
                  pl.BlockSpec((tm, yd.shape[1]), lambda i: (i, 0)),
                  pl.BlockSpec((tm, d), lambda i: (i, 0)),
                  pl.BlockSpec(w.shape, const),
                  pl.BlockSpec(g.shape, const),
                  pl.BlockSpec(b.shape, const)],
        out_specs=pl.BlockSpec((tm, d), lambda i: (i, 0)),
        out_shape=jax.ShapeDtypeStruct((n, d), F32),
        compiler_params=_params("parallel"),
        name="out_proj_ln",
    )(yc, ym, yd, x2, w, g, b)


def _mlp_kernel(x_ref, wu_ref, wd_ref, g_ref, b_ref, out_ref, xb_ref, acc_ref):
    f = pl.program_id(1)

    @pl.when(f == 0)
    def _():
        xb_ref[...] = x_ref[...].astype(BF16)
        acc_ref[...] = jnp.zeros_like(acc_ref)

    up = jnp.maximum(jnp.dot(xb_ref[...], wu_ref[...], preferred_element_type=F32), 0.0)
    acc_ref[...] += jnp.dot((up * up).astype(BF16), wd_ref[...], preferred_element_type=F32)

    @pl.when(f == pl.num_programs(1) - 1)
    def _():
        out_ref[...] = _layer_norm_rows(DEEPNORM_ALPHA * x_ref[...] + acc_ref[...],
                                        g_ref[...], b_ref[...])


def _mlp(x2, wu, wd, g, b):
    n, d = x2.shape
    dff = wu.shape[1]
    tm, tf = MLP_ROWS, MLP_FF
    return pl.pallas_call(
        _mlp_kernel,
        grid=(n // tm, dff // tf),
        in_specs=[pl.BlockSpec((tm, d), lambda i, f: (i, 0)),
                  pl.BlockSpec((d, tf), lambda i, f: (0, f)),
                  pl.BlockSpec((tf, d), lambda i, f: (f, 0)),
                  pl.BlockSpec(g.shape, lambda i, f: (0, 0)),
                  pl.BlockSpec(b.shape, lambda i, f: (0, 0))],
        out_specs=pl.BlockSpec((tm, d), lambda i, f: (i, 0)),
        out_shape=jax.ShapeDtypeStruct((n, d), F32),
        scratch_shapes=[pltpu.VMEM((tm, d), BF16), pltpu.VMEM((tm, d), F32)],
        compiler_params=_params("parallel", "arbitrary"),
        name="mlp_ln",
    )(x2, wu, wd, g, b)


def _row(v):
    return v.reshape(1, -1).astype(F32)


def _pad_rows(w, rows):
    return jnp.pad(w.astype(F32), ((0, rows - w.shape[0]), (0, 0)))


def _rearranged_w_in(w):
    d = w.shape[0]
    gate_lo = 256 + 256 + 512 + 256 + 256
    gate_hi = gate_lo + 2 * ML_HEADS
    pad = jnp.zeros((d, LANES - 2 * ML_HEADS), w.dtype)
    return jnp.concatenate([w[:, :gate_lo], w[:, gate_hi:], w[:, gate_lo:gate_hi], pad],
                           axis=1).astype(BF16)


def kernel(x, w_in, b_igate, b_fgate, conv_dw_w, conv_dw_b, conv_ln_g, conv_ln_b, conv_pw_w, conv_pw_b, ml_conv_w, ml_conv_b, ml_norm_g, lam_q1, lam_k1, lam_q2, lam_k2, da_norm_g, w_out, ln1_g, ln1_b, w_up, w_down, ln2_g, ln2_b):
    bsz, s, d = x.shape
    n = bsz * s
    x2 = x.reshape(n, d)
    for l in range(DEPTH):
        lambda_init = 0.8 - 0.6 * math.exp(-0.3 * l)
        glu, mqk, mv, mo, gates, dq, dk, dv = _in_proj(x2, _rearranged_w_in(w_in[l]))
        seq = lambda a: a.reshape(bsz, s, a.shape[1])

        y_conv = _conv_module(seq(glu), _pad_rows(conv_dw_w[l], 32), _row(conv_dw_b[l]),
                              _row(conv_ln_g[l]), _row(conv_ln_b[l]),
                              conv_pw_w[l].astype(BF16), _row(conv_pw_b[l]))

        gate_bias = jnp.pad(jnp.concatenate([b_igate[l], b_fgate[l]]).astype(F32),
                            (0, LANES - 2 * ML_HEADS)).reshape(1, LANES)
        y_ml = _mlstm(seq(mqk), seq(mv), seq(mo), seq(gates),
                      _pad_rows(ml_conv_w[l], SUBLANES), _row(ml_conv_b[l]),
                      gate_bias, _row(ml_norm_g[l]))

        lam_p = jnp.stack([lam_q1[l], lam_k1[l], lam_q2[l], lam_k2[l]]).astype(F32)
        y_da = _diff_attention(seq(dq), seq(dk), seq(dv), lam_p, _row(da_norm_g[l]),
                               lambda_init)

        x2 = _out_proj(y_conv.reshape(n, -1), y_ml.reshape(n, -1), y_da.reshape(n, -1), x2,
                       w_out[l].astype(BF16), _row(ln1_g[l]), _row(ln1_b[l]))
        x2 = _mlp(x2, w_up[l].astype(BF16), w_down[l].astype(BF16),
                  _row(ln2_g[l]), _row(ln2_b[l]))
    return x2.reshape(bsz, s, d)
```

```python
import functools
import math

import jax
import jax.numpy as jnp
from jax import lax
from jax.experimental import pallas as pl
from jax.experimental.pallas import tpu as pltpu

F32 = jnp.float32
BF16 = jnp.bfloat16

LANES = 128
SUBLANES = 8
VMEM_LIMIT_BYTES = 56 * 1024 * 1024

DEPTH = 2
CONV_KSIZE = 31
ML_HEADS = 4
ML_HEAD_DIM = 64
ML_QK_CONV = 4
ML_CHUNK = 128
DA_HEADS = 4
DA_QK_DIM = 64
DA_V_DIM = 128
LN_EPS = 1e-5
DEEPNORM_ALPHA = (2 * DEPTH) ** 0.25

PROJ_ROWS = 512
CONV_ROWS = 512
CONV_HALO = 32
CONV_SUB = 64
ATT_Q = 512
ATT_K = PROJ_ROWS
ATT_SUM_ROWS = 16
MLP_ROWS = 1024
MLP_FF = 512


def _params(*semantics):
    return pltpu.CompilerParams(dimension_semantics=semantics,
                                vmem_limit_bytes=VMEM_LIMIT_BYTES)


def _sigmoid(x):
    return 1.0 / (1.0 + jnp.exp(-x))


def _layer_norm_rows(x, g, b):
    mu = jnp.mean(x, axis=-1, keepdims=True)
    xc = x - mu
    var = jnp.mean(xc * xc, axis=-1, keepdims=True)
    return xc * lax.rsqrt(var + LN_EPS) * g + b


def _in_proj_kernel(x_ref, w_ref, wvt_ref, glu_ref, mqk_ref, mv_ref, mo_ref, gate_ref,
                    dq_ref, dk_ref, dvt_ref):
    xb = x_ref[...].astype(BF16)

    def sec(lo, width):
        return jnp.dot(xb, w_ref[:, lo:lo + width], preferred_element_type=F32)

    glu_ref[...] = sec(0, 256) * _sigmoid(sec(256, 256))
    mqk_ref[...] = sec(512, 512)
    mv_ref[...] = sec(1024, 256)
    mo_ref[...] = sec(1280, 256)
    dq_ref[...] = (sec(1536, 512) * (DA_QK_DIM ** -0.5 * math.log2(math.e))).astype(BF16)
    dk_ref[...] = sec(2048, 512).astype(BF16)
    gate_ref[...] = sec(2560, LANES)
    dvt_ref[0, 0] = lax.dot_general(wvt_ref[...], xb, (((1,), (1,)), ((), ())),
                                    preferred_element_type=F32).astype(BF16)


def _in_proj(x2, w_r, w_vt, bsz):
    n, d = x2.shape
    tm = PROJ_ROWS
    tiles_per_seq = n // bsz // tm
    widths = (256, 512, 256, 256, LANES, 512, 512)
    dtypes = (F32, F32, F32, F32, F32, BF16, BF16)
    wv = w_vt.shape[0]
    return pl.pallas_call(
        _in_proj_kernel,
        grid=(n // tm,),
        in_specs=[pl.BlockSpec((tm, d), lambda i: (i, 0)),
                  pl.BlockSpec(w_r.shape, lambda i: (0, 0)),
                  pl.BlockSpec(w_vt.shape, lambda i: (0, 0))],
        out_specs=[pl.BlockSpec((tm, w), lambda i: (i, 0)) for w in widths]
        + [pl.BlockSpec((1, 1, wv, tm),
                        lambda i: (i // tiles_per_seq, i % tiles_per_seq, 0, 0))],
        out_shape=[jax.ShapeDtypeStruct((n, w), dt) for w, dt in zip(widths, dtypes)]
        + [jax.ShapeDtypeStruct((bsz, tiles_per_seq, wv, tm), BF16)],
        compiler_params=_params("parallel"),
        name="in_proj",
    )(x2, w_r, w_vt)


def _conv_kernel(x_ref, halo_ref, dw_ref, dwb_ref, g_ref, b_ref, pw_ref, pwb_ref,
                 out_ref, xpad, ybuf):
    s = pl.program_id(1)
    rows = x_ref.shape[1]
    xpad[0:CONV_HALO, :] = jnp.where(s > 0, halo_ref[0], 0.0)
    xpad[CONV_HALO:CONV_HALO + rows, :] = x_ref[0]
    first = CONV_HALO - (CONV_KSIZE - 1)
    for r in range(0, rows, CONV_SUB):
        acc = jnp.broadcast_to(dwb_ref[...], (CONV_SUB, x_ref.shape[2]))
        for j in range(CONV_KSIZE):
            lo = r + first + j
            acc = acc + dw_ref[j:j + 1, :] * xpad[lo:lo + CONV_SUB, :]
        y = _layer_norm_rows(acc, g_ref[...], b_ref[...])
        ybuf[r:r + CONV_SUB, :] = (y * _sigmoid(y)).astype(BF16)
    out = jnp.dot(ybuf[...], pw_ref[...], preferred_element_type=F32) + pwb_ref[...]
    out_ref[0] = out.astype(BF16)


def _conv_module(glu, dw, dwb, g, b, pw, pwb):
    bsz, s, ch = glu.shape
    ts = CONV_ROWS
    halo_per_tile = ts // CONV_HALO
    return pl.pallas_call(
        _conv_kernel,
        grid=(bsz, s // ts),
        in_specs=[
            pl.BlockSpec((1, ts, ch), lambda bi, si: (bi, si, 0)),
            pl.BlockSpec((1, CONV_HALO, ch),
                         lambda bi, si: (bi, jnp.maximum(si * halo_per_tile - 1, 0), 0)),
            pl.BlockSpec(dw.shape, lambda bi, si: (0, 0)),
            pl.BlockSpec(dwb.shape, lambda bi, si: (0, 0)),
            pl.BlockSpec(g.shape, lambda bi, si: (0, 0)),
            pl.BlockSpec(b.shape, lambda bi, si: (0, 0)),
            pl.BlockSpec(pw.shape, lambda bi, si: (0, 0)),
            pl.BlockSpec(pwb.shape, lambda bi, si: (0, 0)),
        ],
        out_specs=pl.BlockSpec((1, ts, ch), lambda bi, si: (bi, si, 0)),
        out_shape=jax.ShapeDtypeStruct((bsz, s, ch), BF16),
        scratch_shapes=[pltpu.VMEM((CONV_HALO + ts, ch), F32),
                        pltpu.VMEM((ts, ch), BF16)],
        compiler_params=_params("parallel", "parallel"),
        name="conv_module",
    )(glu, glu, dw, dwb, g, b, pw, pwb)


def _mlstm_kernel(qk_ref, halo_ref, v_ref, o_ref, gate_ref, cw_ref, cb_ref, gb_ref,
                  ng_ref, out_ref, qpad, ct_ref, m_ref):
    c = pl.program_id(1)
    t = ML_CHUNK

    @pl.when(c == 0)
    def _():
        ct_ref[...] = jnp.zeros_like(ct_ref)
        m_ref[...] = jnp.zeros_like(m_ref)

    qpad[0:SUBLANES, :] = jnp.where(c > 0, halo_ref[0], 0.0)
    qpad[SUBLANES:SUBLANES + t, :] = qk_ref[0]
    first = SUBLANES - (ML_QK_CONV - 1)
    acc = jnp.broadcast_to(cb_ref[...], (t, qk_ref.shape[2]))
    for j in range(ML_QK_CONV):
        acc = acc + cw_ref[j:j + 1, :] * qpad[first + j:first + j + t, :]
    qk = acc * _sigmoid(acc)

    gpre = gate_ref[0] + gb_ref[...]
    log_f = jnp.minimum(gpre, 0.0) - jnp.log(1.0 + jnp.exp(-jnp.abs(gpre)))
    row = lax.broadcasted_iota(jnp.int32, (t, t), 0)
    col = lax.broadcasted_iota(jnp.int32, (t, t), 1)
    causal = row >= col
    tri = causal.astype(F32)
    bcum = jnp.dot(tri, log_f, preferred_element_type=F32, precision=lax.Precision.HIGHEST)
    gates_t = jnp.where(col < ML_HEADS, gpre, bcum).T

    w_pair = 2 * ML_HEAD_DIM
    k_off = ML_HEADS * ML_HEAD_DIM
    lane = lax.broadcasted_iota(jnp.int32, (t, w_pair), 1)
    for pair in range(ML_HEADS // 2):
        lo = pair * w_pair
        q_pair = qk[:, lo:lo + w_pair]
        k_pair = qk[:, k_off + lo:k_off + lo + w_pair] * (ML_HEAD_DIM ** -0.5)
        v_pair = v_ref[0, :, lo:lo + w_pair]
        k_b = k_pair.astype(BF16)
        h_pair = None
        for par in range(2):
            h = 2 * pair + par
            own = (lane >= ML_HEAD_DIM) if par else (lane < ML_HEAD_DIM)
            b_col = bcum[:, ML_HEADS + h:ML_HEADS + h + 1]
            li_col = gpre[:, h:h + 1]
            b_row = gates_t[ML_HEADS + h:ML_HEADS + h + 1, :]
            li_row = gates_t[h:h + 1, :]
            m_prev = m_ref[h:h + 1, 0:1]
            ct_prev = ct_ref[h]

            d = jnp.where(causal, b_col - b_row + li_row, -jnp.inf)
            inter = b_col + m_prev
            m_pos = jnp.maximum(inter, jnp.max(d, axis=-1, keepdims=True))
            a_inter = jnp.exp(inter - m_pos)
            q_m = jnp.where(own, q_pair, 0.0).astype(BF16)
            vx = jnp.where(own, v_pair, 1.0).astype(BF16)
            sc = lax.dot_general(q_m, k_b, (((1,), (1,)), ((), ())),
                                 preferred_element_type=F32) * jnp.exp(d - m_pos)
            numden = (a_inter * jnp.dot(q_m, ct_prev.astype(BF16), preferred_element_type=F32)
                      + jnp.dot(sc.astype(BF16), vx, preferred_element_type=F32))
            den = pltpu.roll(numden, ML_HEAD_DIM, 1)
            hh = numden / jnp.maximum(jnp.abs(den), jnp.exp(-m_pos))
            h_pair = hh if h_pair is None else jnp.where(own, hh, h_pair)

            g_end = b_col[t - 1:t, :]
            w_end = g_end - b_col + li_col
            m_loc = jnp.max(w_end, axis=0, keepdims=True)
            p_end = jnp.exp(w_end - m_loc)
            m_new = jnp.maximum(g_end + m_prev, m_loc)
            a_st = jnp.exp(g_end + m_prev - m_new)
            e_st = jnp.exp(m_loc - m_new)
            kp = (jnp.where(own, k_pair, 0.0) * p_end).astype(BF16)
            c_loc = lax.dot_general(kp, vx, (((0,), (0,)), ((), ())),
                                    preferred_element_type=F32)
            ct_ref[h] = a_st * ct_prev + e_st * c_loc
            m_ref[h:h + 1, :] = jnp.broadcast_to(m_new, (1, LANES))

        x = h_pair * _sigmoid(o_ref[0, :, lo:lo + w_pair])
        low = lane < ML_HEAD_DIM
        inv = 1.0 / ML_HEAD_DIM
        s_low = jnp.sum(jnp.where(low, x, 0.0), axis=-1, keepdims=True)
        s_all = jnp.sum(x, axis=-1, keepdims=True)
        xc = x - jnp.where(low, s_low, s_all - s_low) * inv
        sq = xc * xc
        v_low = jnp.sum(jnp.where(low, sq, 0.0), axis=-1, keepdims=True)
        v_all = jnp.sum(sq, axis=-1, keepdims=True)
        var = jnp.where(low, v_low, v_all - v_low) * inv
        y = xc * lax.rsqrt(var + LN_EPS) * ng_ref[:, lo:lo + w_pair]
        out_ref[0, :, lo:lo + w_pair] = y.astype(BF16)


def _mlstm(mqk, mv, mo, gates, cw, cb, gb, ng):
    bsz, s, wqk = mqk.shape
    wv = mv.shape[2]
    t = ML_CHUNK
    halo_per_chunk = t // SUBLANES
    const = lambda bi, ci: (0, 0)
    return pl.pallas_call(
        _mlstm_kernel,
        grid=(bsz, s // t),
        in_specs=[
            pl.BlockSpec((1, t, wqk), lambda bi, ci: (bi, ci, 0)),
            pl.BlockSpec((1, SUBLANES, wqk),
                         lambda bi, ci: (bi, jnp.maximum(ci * halo_per_chunk - 1, 0), 0)),
            pl.BlockSpec((1, t, wv), lambda bi, ci: (bi, ci, 0)),
            pl.BlockSpec((1, t, wv), lambda bi, ci: (bi, ci, 0)),
            pl.BlockSpec((1, t, LANES), lambda bi, ci: (bi, ci, 0)),
            pl.BlockSpec(cw.shape, const),
            pl.BlockSpec(cb.shape, const),
            pl.BlockSpec(gb.shape, const),
            pl.BlockSpec(ng.shape, const),
        ],
        out_specs=pl.BlockSpec((1, t, wv), lambda bi, ci: (bi, ci, 0)),
        out_shape=jax.ShapeDtypeStruct((bsz, s, wv), BF16),
        scratch_shapes=[pltpu.VMEM((SUBLANES + t, wqk), F32),
                        pltpu.VMEM((ML_HEADS, 2 * ML_HEAD_DIM, LANES), F32),
                        pltpu.VMEM((SUBLANES, LANES), F32)],
        compiler_params=_params("parallel", "arbitrary"),
        name="mlstm",
    )(mqk, mqk, mv, mo, gates, cw, cb, gb, ng)


def _attn_kernel(q_ref, k_ref, vt_ref, lam_ref, ng_ref, out_ref, qm_ref, acc_ref,
                 *, lambda_init):
    i = pl.program_id(2)
    tq = q_ref.shape[1]
    tk = vt_ref.shape[3]

    q = q_ref[0]
    lane = lax.broadcasted_iota(jnp.int32, q.shape, 1)
    zero = jnp.zeros_like(q)
    qm_ref[0] = jnp.where(lane < DA_QK_DIM, q, zero)
    qm_ref[1] = jnp.where(lane >= DA_QK_DIM, q, zero)
    acc_ref[...] = jnp.zeros_like(acc_ref)

    ones_rows = jnp.ones((ATT_SUM_ROWS, tk), BF16)

    def block(j, m_olds, masked):
        k_blk = k_ref[0, pl.ds(pl.multiple_of(j * tk, tk), tk), :]
        vt_blk = jnp.concatenate([vt_ref[0, j], ones_rows], axis=0)
        sts = [lax.dot_general(k_blk, qm_ref[mp], (((1,), (1,)), ((), ())),
                               preferred_element_type=F32) for mp in range(2)]
        if masked:
            key = lax.broadcasted_iota(jnp.int32, (tk, tq), 0)
            qry = lax.broadcasted_iota(jnp.int32, (tk, tq), 1)
            keep = key <= qry
            sts = [jnp.where(keep, st, -jnp.inf) for st in sts]
        m_news = []
        for mp in range(2):
            m_new = jnp.maximum(m_olds[mp], jnp.max(sts[mp], axis=0, keepdims=True))
            alpha = jnp.exp2(m_olds[mp] - m_new)
            pexp = jnp.exp2(sts[mp] - m_new).astype(BF16)
            acc_ref[mp] = alpha * acc_ref[mp] + jnp.dot(vt_blk, pexp,
                                                       preferred_element_type=F32)
            m_news.append(m_new)
        return tuple(m_news)

    m0 = jnp.full((1, tq), -jnp.inf, F32)
    m_run = lax.fori_loop(0, i, lambda j, c: block(j, c, False), (m0, m0))
    block(i, m_run, True)

    lp = lam_ref[...]
    lam = (jnp.exp(jnp.sum(lp[0:1, :] * lp[1:2, :], axis=-1, keepdims=True))
           - jnp.exp(jnp.sum(lp[2:3, :] * lp[3:4, :], axis=-1, keepdims=True))
           + lambda_init)
    dv = DA_V_DIM
    o_t = (acc_ref[0, 0:dv, :] / acc_ref[0, dv:dv + 1, :]
           - lam * (acc_ref[1, 0:dv, :] / acc_ref[1, dv:dv + 1, :]))
    ms = jnp.mean(o_t * o_t, axis=0, keepdims=True)
    y_t = o_t * lax.rsqrt(ms + LN_EPS) * (ng_ref[...] * (1.0 - lambda_init))
    out_ref[0] = y_t.T.astype(BF16)


def _diff_attention(dq, dk, dvt, lam_p, ng_col, lambda_init):
    bsz, s, _ = dq.shape
    tq = ATT_Q
    nk, tk = dvt.shape[1], dvt.shape[3]
    assert tq == tk and nk * tk == s
    return pl.pallas_call(
        functools.partial(_attn_kernel, lambda_init=lambda_init),
        grid=(bsz, DA_HEADS, s // tq),
        in_specs=[
            pl.BlockSpec((1, tq, 2 * DA_QK_DIM), lambda b, h, i: (b, i, h)),
            pl.BlockSpec((1, s, 2 * DA_QK_DIM), lambda b, h, i: (b, 0, h)),
            pl.BlockSpec((1, nk, DA_V_DIM, tk), lambda b, h, i: (b, 0, h, 0)),
            pl.BlockSpec(lam_p.shape, lambda b, h, i: (0, 0)),
            pl.BlockSpec(ng_col.shape, lambda b, h, i: (0, 0)),
        ],
        out_specs=pl.BlockSpec((1, tq, DA_V_DIM), lambda b, h, i: (b, i, h)),
        out_shape=jax.ShapeDtypeStruct((bsz, s, DA_HEADS * DA_V_DIM), BF16),
        scratch_shapes=[pltpu.VMEM((2, tq, 2 * DA_QK_DIM), BF16),
                        pltpu.VMEM((2, DA_V_DIM + ATT_SUM_ROWS, tq), F32)],
        compiler_params=_params("parallel", "parallel", "arbitrary"),
        name="diff_attention",
    )(dq, dk, dvt, lam_p, ng_col)


def _out_proj_kernel(yc_ref, ym_ref, yd_ref, x_ref, w_ref, g_ref, b_ref, out_ref):
    wc = yc_ref.shape[1]
    wm = ym_ref.shape[1]
    h = jnp.dot(yc_ref[...], w_ref[0:wc, :], preferred_element_type=F32)
    h = h + jnp.dot(ym_ref[...], w_ref[wc:wc + wm, :], preferred_element_type=F32)
    h = h + jnp.dot(yd_ref[...], w_ref[wc + wm:, :], preferred_element_type=F32)
    out_ref[...] = _layer_norm_rows(DEEPNORM_ALPHA * x_ref[...] + h, g_ref[...], b_ref[...])


def _out_proj(yc, ym, yd, x2, w, g, b):
    n, d = x2.shape
    tm = PROJ_ROWS
    const = lambda i: (0, 0)
    return pl.pallas_call(
        _out_proj_kernel,
        grid=(n // tm,),
        in_specs=[pl.BlockSpec((tm, yc.shape[1]), lambda i: (i, 0)),
                  pl.BlockSpec((tm, ym.shape[1]), lambda i: (i, 0)),
                  pl.BlockSpec((tm, yd.shape[1]), lambda i: (i, 0)),
                  pl.BlockSpec((tm, d), lambda i: (i, 0)),
                  pl.BlockSpec(w.shape, const),
                  pl.BlockSpec(g.shape, const),
                  pl.BlockSpec(b.shape, const)],
        out_specs=pl.BlockSpec((tm, d), lambda i: (i, 0)),
        out_shape=jax.ShapeDtypeStruct((n, d), F32),
        compiler_params=_params("parallel"),
        name="out_proj_ln",
    )(yc, ym, yd, x2, w, g, b)


def _mlp_kernel(x_ref, wu_ref, wd_ref, g_ref, b_ref, out_ref, xb_ref, acc_ref):
    f = pl.program_id(1)

    @pl.when(f == 0)
    def _():
        xb_ref[...] = x_ref[...].astype(BF16)
        acc_ref[...] = jnp.zeros_like(acc_ref)

    up = jnp.maximum(jnp.dot(xb_ref[...], wu_ref[...], preferred_element_type=F32), 0.0)
    acc_ref[...] += jnp.dot((up * up).astype(BF16), wd_ref[...], preferred_element_type=F32)

    @pl.when(f == pl.num_programs(1) - 1)
    def _():
        out_ref[...] = _layer_norm_rows(DEEPNORM_ALPHA * x_ref[...] + acc_ref[...],
                                        g_ref[...], b_ref[...])


def _mlp(x2, wu, wd, g, b):
    n, d = x2.shape
    dff = wu.shape[1]
    tm, tf = MLP_ROWS, MLP_FF
    return pl.pallas_call(
        _mlp_kernel,
        grid=(n // tm, dff // tf),
        in_specs=[pl.BlockSpec((tm, d), lambda i, f: (i, 0)),
                  pl.BlockSpec((d, tf), lambda i, f: (0, f)),
                  pl.BlockSpec((tf, d), lambda i, f: (f, 0)),
                  pl.BlockSpec(g.shape, lambda i, f: (0, 0)),
                  pl.BlockSpec(b.shape, lambda i, f: (0, 0))],
        out_specs=pl.BlockSpec((tm, d), lambda i, f: (i, 0)),
        out_shape=jax.ShapeDtypeStruct((n, d), F32),
        scratch_shapes=[pltpu.VMEM((tm, d), BF16), pltpu.VMEM((tm, d), F32)],
        compiler_params=_params("parallel", "arbitrary"),
        name="mlp_ln",
    )(x2, wu, wd, g, b)


def _row(v):
    return v.reshape(1, -1).astype(F32)


def _pad_rows(w, rows):
    return jnp.pad(w.astype(F32), ((0, rows - w.shape[0]), (0, 0)))


def _rearranged_w_in(w):
    d = w.shape[0]
    gate_lo = 256 + 256 + 512 + 256 + 256
    gate_hi = gate_lo + 2 * ML_HEADS
    v_lo = gate_hi + 2 * DA_HEADS * 2 * DA_QK_DIM
    pad = jnp.zeros((d, LANES - 2 * ML_HEADS), w.dtype)
    w_r = jnp.concatenate([w[:, :gate_lo], w[:, gate_hi:v_lo], w[:, gate_lo:gate_hi], pad], axis=1)
    return w_r.astype(BF16), w[:, v_lo:].T.astype(BF16)


def kernel(x, w_in, b_igate, b_fgate, conv_dw_w, conv_dw_b, conv_ln_g, conv_ln_b, conv_pw_w, conv_pw_b, ml_conv_w, ml_conv_b, ml_norm_g, lam_q1, lam_k1, lam_q2, lam_k2, da_norm_g, w_out, ln1_g, ln1_b, w_up, w_down, ln2_g, ln2_b):
    bsz, s, d = x.shape
    n = bsz * s
    x2 = x.reshape(n, d)
    for l in range(DEPTH):
        lambda_init = 0.8 - 0.6 * math.exp(-0.3 * l)
        w_r, w_vt = _rearranged_w_in(w_in[l])
        glu, mqk, mv, mo, gates, dq, dk, dvt = _in_proj(x2, w_r, w_vt, bsz)
        seq = lambda a: a.reshape(bsz, s, a.shape[1])

        y_conv = _conv_module(seq(glu), _pad_rows(conv_dw_w[l], 32), _row(conv_dw_b[l]),
                              _row(conv_ln_g[l]), _row(conv_ln_b[l]),
                              conv_pw_w[l].astype(BF16), _row(conv_pw_b[l]))

        gate_bias = jnp.pad(jnp.concatenate([b_igate[l], b_fgate[l]]).astype(F32),
                            (0, LANES - 2 * ML_HEADS)).reshape(1, LANES)
        y_ml = _mlstm(seq(mqk), seq(mv), seq(mo), seq(gates),
                      _pad_rows(ml_conv_w[l], SUBLANES), _row(ml_conv_b[l]),
                      gate_bias, _row(ml_norm_g[l]))

        lam_p = jnp.stack([lam_q1[l], lam_k1[l], lam_q2[l], lam_k2[l]]).astype(F32)
        y_da = _diff_attention(seq(dq), seq(dk), dvt, lam_p,
                               da_norm_g[l].astype(F32).reshape(-1, 1), lambda_init)

        x2 = _out_proj(y_conv.reshape(n, -1), y_ml.reshape(n, -1), y_da.reshape(n, -1), x2,
                       w_out[l].astype(BF16), _row(ln1_g[l]), _row(ln1_b[l]))
        x2 = _mlp(x2, w_up[l].astype(BF16), w_down[l].astype(BF16),
                  _row(ln2_g[l]), _row(ln2_b[l]))
    return x2.reshape(bsz, s, d)
```

```python
import functools
import math

import jax
import jax.numpy as jnp
from jax import lax
from jax.experimental import pallas as pl
from jax.experimental.pallas import tpu as pltpu

F32 = jnp.float32
BF16 = jnp.bfloat16

LANES = 128
SUBLANES = 8
VMEM_LIMIT_BYTES = 56 * 1024 * 1024

DEPTH = 2
CONV_KSIZE = 31
ML_HEADS = 4
ML_HEAD_DIM = 64
ML_QK_CONV = 4
ML_CHUNK = 128
DA_HEADS = 4
DA_QK_DIM = 64
DA_V_DIM = 128
LN_EPS = 1e-5
DEEPNORM_ALPHA = (2 * DEPTH) ** 0.25

PROJ_ROWS = 512
CONV_ROWS = 512
CONV_HALO = 32
CONV_SUB = 64
ATT_Q = 512
ATT_K = PROJ_ROWS
ML_BATCH = 4
ATT_SUM_ROWS = 16
MLP_ROWS = 1024
MLP_FF = 512


def _params(*semantics):
    return pltpu.CompilerParams(dimension_semantics=semantics,
                                vmem_limit_bytes=VMEM_LIMIT_BYTES)


def _sigmoid(x):
    return 1.0 / (1.0 + jnp.exp(-x))


def _layer_norm_rows(x, g, b):
    mu = jnp.mean(x, axis=-1, keepdims=True)
    xc = x - mu
    var = jnp.mean(xc * xc, axis=-1, keepdims=True)
    return xc * lax.rsqrt(var + LN_EPS) * g + b


def _in_proj_kernel(x_ref, w_ref, wvt_ref, glu_ref, mqk_ref, mv_ref, mo_ref, gate_ref,
                    dq_ref, dk_ref, dvt_ref):
    xb = x_ref[...].astype(BF16)

    def sec(lo, width):
        return jnp.dot(xb, w_ref[:, lo:lo + width], preferred_element_type=F32)

    glu_ref[...] = sec(0, 256) * _sigmoid(sec(256, 256))
    mqk_ref[...] = sec(512, 512)
    mv_ref[...] = sec(1024, 256)
    mo_ref[...] = sec(1280, 256)
    dq_ref[...] = (sec(1536, 512) * (DA_QK_DIM ** -0.5 * math.log2(math.e))).astype(BF16)
    dk_ref[...] = sec(2048, 512).astype(BF16)
    gate_ref[...] = sec(2560, LANES)
    dvt_ref[0, 0] = lax.dot_general(wvt_ref[...], xb, (((1,), (1,)), ((), ())),
                                    preferred_element_type=F32).astype(BF16)


def _in_proj(x2, w_r, w_vt, bsz):
    n, d = x2.shape
    tm = PROJ_ROWS
    tiles_per_seq = n // bsz // tm
    widths = (256, 512, 256, 256, LANES, 512, 512)
    dtypes = (F32, F32, F32, F32, F32, BF16, BF16)
    wv = w_vt.shape[0]
    return pl.pallas_call(
        _in_proj_kernel,
        grid=(n // tm,),
        in_specs=[pl.BlockSpec((tm, d), lambda i: (i, 0)),
                  pl.BlockSpec(w_r.shape, lambda i: (0, 0)),
                  pl.BlockSpec(w_vt.shape, lambda i: (0, 0))],
        out_specs=[pl.BlockSpec((tm, w), lambda i: (i, 0)) for w in widths]
        + [pl.BlockSpec((1, 1, wv, tm),
                        lambda i: (i // tiles_per_seq, i % tiles_per_seq, 0, 0))],
        out_shape=[jax.ShapeDtypeStruct((n, w), dt) for w, dt in zip(widths, dtypes)]
        + [jax.ShapeDtypeStruct((bsz, tiles_per_seq, wv, tm), BF16)],
        compiler_params=_params("parallel"),
        name="in_proj",
    )(x2, w_r, w_vt)


def _conv_kernel(x_ref, halo_ref, dw_ref, dwb_ref, g_ref, b_ref, pw_ref, pwb_ref,
                 out_ref, xpad, ybuf):
    s = pl.program_id(1)
    rows = x_ref.shape[1]
    xpad[0:CONV_HALO, :] = jnp.where(s > 0, halo_ref[0], 0.0)
    xpad[CONV_HALO:CONV_HALO + rows, :] = x_ref[0]
    first = CONV_HALO - (CONV_KSIZE - 1)
    for r in range(0, rows, CONV_SUB):
        acc = jnp.broadcast_to(dwb_ref[...], (CONV_SUB, x_ref.shape[2]))
        for j in range(CONV_KSIZE):
            lo = r + first + j
            acc = acc + dw_ref[j:j + 1, :] * xpad[lo:lo + CONV_SUB, :]
        y = _layer_norm_rows(acc, g_ref[...], b_ref[...])
        ybuf[r:r + CONV_SUB, :] = (y * _sigmoid(y)).astype(BF16)
    out = jnp.dot(ybuf[...], pw_ref[...], preferred_element_type=F32) + pwb_ref[...]
    out_ref[0] = out.astype(BF16)


def _conv_module(glu, dw, dwb, g, b, pw, pwb):
    bsz, s, ch = glu.shape
    ts = CONV_ROWS
    halo_per_tile = ts // CONV_HALO
    return pl.pallas_call(
        _conv_kernel,
        grid=(bsz, s // ts),
        in_specs=[
            pl.BlockSpec((1, ts, ch), lambda bi, si: (bi, si, 0)),
            pl.BlockSpec((1, CONV_HALO, ch),
                         lambda bi, si: (bi, jnp.maximum(si * halo_per_tile - 1, 0), 0)),
            pl.BlockSpec(dw.shape, lambda bi, si: (0, 0)),
            pl.BlockSpec(dwb.shape, lambda bi, si: (0, 0)),
            pl.BlockSpec(g.shape, lambda bi, si: (0, 0)),
            pl.BlockSpec(b.shape, lambda bi, si: (0, 0)),
            pl.BlockSpec(pw.shape, lambda bi, si: (0, 0)),
            pl.BlockSpec(pwb.shape, lambda bi, si: (0, 0)),
        ],
        out_specs=pl.BlockSpec((1, ts, ch), lambda bi, si: (bi, si, 0)),
        out_shape=jax.ShapeDtypeStruct((bsz, s, ch), BF16),
        scratch_shapes=[pltpu.VMEM((CONV_HALO + ts, ch), F32),
                        pltpu.VMEM((ts, ch), BF16)],
        compiler_params=_params("parallel", "parallel"),
        name="conv_module",
    )(glu, glu, dw, dwb, g, b, pw, pwb)


def _mlstm_kernel(qk_ref, halo_ref, v_ref, o_ref, gate_ref, cw_ref, cb_ref, gb_ref,
                  ng_ref, out_ref, qpad, ct_ref, m_ref):
    for bb in range(qk_ref.shape[0]):
        one = pl.ds(bb, 1)
        _mlstm_chunk(pl.program_id(1), qk_ref.at[one], halo_ref.at[one], v_ref.at[one],
                     o_ref.at[one], gate_ref.at[one], cw_ref, cb_ref, gb_ref, ng_ref,
                     out_ref.at[one], qpad.at[bb], ct_ref.at[bb], m_ref.at[bb])


def _mlstm_chunk(c, qk_ref, halo_ref, v_ref, o_ref, gate_ref, cw_ref, cb_ref, gb_ref,
                 ng_ref, out_ref, qpad, ct_ref, m_ref):
    t = ML_CHUNK

    @pl.when(c == 0)
    def _():
        ct_ref[...] = jnp.zeros_like(ct_ref)
        m_ref[...] = jnp.zeros_like(m_ref)

    qpad[0:SUBLANES, :] = jnp.where(c > 0, halo_ref[0], 0.0)
    qpad[SUBLANES:SUBLANES + t, :] = qk_ref[0]
    first = SUBLANES - (ML_QK_CONV - 1)
    acc = jnp.broadcast_to(cb_ref[...], (t, qk_ref.shape[2]))
    for j in range(ML_QK_CONV):
        acc = acc + cw_ref[j:j + 1, :] * qpad[first + j:first + j + t, :]
    qk = acc * _sigmoid(acc)

    gpre = gate_ref[0] + gb_ref[...]
    log_f = jnp.minimum(gpre, 0.0) - jnp.log(1.0 + jnp.exp(-jnp.abs(gpre)))
    row = lax.broadcasted_iota(jnp.int32, (t, t), 0)
    col = lax.broadcasted_iota(jnp.int32, (t, t), 1)
    causal = row >= col
    tri = causal.astype(F32)
    bcum = jnp.dot(tri, log_f, preferred_element_type=F32, precision=lax.Precision.HIGHEST)
    gates_t = jnp.where(col < ML_HEADS, gpre, bcum).T

    w_pair = 2 * ML_HEAD_DIM
    k_off = ML_HEADS * ML_HEAD_DIM
    lane = lax.broadcasted_iota(jnp.int32, (t, w_pair), 1)
    for pair in range(ML_HEADS // 2):
        lo = pair * w_pair
        q_pair = qk[:, lo:lo + w_pair]
        k_pair = qk[:, k_off + lo:k_off + lo + w_pair] * (ML_HEAD_DIM ** -0.5)
        v_pair = v_ref[0, :, lo:lo + w_pair]
        k_b = k_pair.astype(BF16)
        h_pair = None
        for par in range(2):
            h = 2 * pair + par
            own = (lane >= ML_HEAD_DIM) if par else (lane < ML_HEAD_DIM)
            b_col = bcum[:, ML_HEADS + h:ML_HEADS + h + 1]
            li_col = gpre[:, h:h + 1]
            b_row = gates_t[ML_HEADS + h:ML_HEADS + h + 1, :]
            li_row = gates_t[h:h + 1, :]
            m_prev = m_ref[h:h + 1, 0:1]
            ct_prev = ct_ref[h]

            d = jnp.where(causal, b_col - b_row + li_row, -jnp.inf)
            inter = b_col + m_prev
            m_pos = jnp.maximum(inter, jnp.max(d, axis=-1, keepdims=True))
            a_inter = jnp.exp(inter - m_pos)
            q_m = jnp.where(own, q_pair, 0.0).astype(BF16)
            vx = jnp.where(own, v_pair, 1.0).astype(BF16)
            sc = lax.dot_general(q_m, k_b, (((1,), (1,)), ((), ())),
                                 preferred_element_type=F32) * jnp.exp(d - m_pos)
            numden = (a_inter * jnp.dot(q_m, ct_prev.astype(BF16), preferred_element_type=F32)
                      + jnp.dot(sc.astype(BF16), vx, preferred_element_type=F32))
            den = pltpu.roll(numden, ML_HEAD_DIM, 1)
            hh = numden / jnp.maximum(jnp.abs(den), jnp.exp(-m_pos))
            h_pair = hh if h_pair is None else jnp.where(own, hh, h_pair)

            g_end = b_col[t - 1:t, :]
            w_end = g_end - b_col + li_col
            m_loc = jnp.max(w_end, axis=0, keepdims=True)
            p_end = jnp.exp(w_end - m_loc)
            m_new = jnp.maximum(g_end + m_prev, m_loc)
            a_st = jnp.exp(g_end + m_prev - m_new)
            e_st = jnp.exp(m_loc - m_new)
            kp = (jnp.where(own, k_pair, 0.0) * p_end).astype(BF16)
            c_loc = lax.dot_general(kp, vx, (((0,), (0,)), ((), ())),
                                    preferred_element_type=F32)
            ct_ref[h] = a_st * ct_prev + e_st * c_loc
            m_ref[h:h + 1, :] = jnp.broadcast_to(m_new, (1, LANES))

        x = h_pair * _sigmoid(o_ref[0, :, lo:lo + w_pair])
        low = lane < ML_HEAD_DIM
        inv = 1.0 / ML_HEAD_DIM
        s_low = jnp.sum(jnp.where(low, x, 0.0), axis=-1, keepdims=True)
        s_all = jnp.sum(x, axis=-1, keepdims=True)
        xc = x - jnp.where(low, s_low, s_all - s_low) * inv
        sq = xc * xc
        v_low = jnp.sum(jnp.where(low, sq, 0.0), axis=-1, keepdims=True)
        v_all = jnp.sum(sq, axis=-1, keepdims=True)
        var = jnp.where(low, v_low, v_all - v_low) * inv
        y = xc * lax.rsqrt(var + LN_EPS) * ng_ref[:, lo:lo + w_pair]
        out_ref[0, :, lo:lo + w_pair] = y.astype(BF16)


def _mlstm(mqk, mv, mo, gates, cw, cb, gb, ng):
    bsz, s, wqk = mqk.shape
    wv = mv.shape[2]
    t = ML_CHUNK
    nb = math.gcd(bsz, ML_BATCH)
    halo_per_chunk = t // SUBLANES
    const = lambda bi, ci: (0, 0)
    return pl.pallas_call(
        _mlstm_kernel,
        grid=(bsz // nb, s // t),
        in_specs=[
            pl.BlockSpec((nb, t, wqk), lambda bi, ci: (bi, ci, 0)),
            pl.BlockSpec((nb, SUBLANES, wqk),
                         lambda bi, ci: (bi, jnp.maximum(ci * halo_per_chunk - 1, 0), 0)),
            pl.BlockSpec((nb, t, wv), lambda bi, ci: (bi, ci, 0)),
            pl.BlockSpec((nb, t, wv), lambda bi, ci: (bi, ci, 0)),
            pl.BlockSpec((nb, t, LANES), lambda bi, ci: (bi, ci, 0)),
            pl.BlockSpec(cw.shape, const),
            pl.BlockSpec(cb.shape, const),
            pl.BlockSpec(gb.shape, const),
            pl.BlockSpec(ng.shape, const),
        ],
        out_specs=pl.BlockSpec((nb, t, wv), lambda bi, ci: (bi, ci, 0)),
        out_shape=jax.ShapeDtypeStruct((bsz, s, wv), BF16),
        scratch_shapes=[pltpu.VMEM((nb, SUBLANES + t, wqk), F32),
                        pltpu.VMEM((nb, ML_HEADS, 2 * ML_HEAD_DIM, LANES), F32),
                        pltpu.VMEM((nb, SUBLANES, LANES), F32)],
        compiler_params=_params("parallel", "arbitrary"),
        name="mlstm",
    )(mqk, mqk, mv, mo, gates, cw, cb, gb, ng)


def _attn_kernel(q_ref, k_ref, vt_ref, lam_ref, ng_ref, out_ref, qm_ref, acc_ref,
                 sa_ref, sb_ref, mblk_ref, m_ref, *, lambda_init):
    i = pl.program_id(2)
    tq = q_ref.shape[1]
    tk = vt_ref.shape[3]

    q = q_ref[0]
    lane = lax.broadcasted_iota(jnp.int32, q.shape, 1)
    zero = jnp.zeros_like(q)
    qm_ref[0] = jnp.where(lane < DA_QK_DIM, q, zero)
    qm_ref[1] = jnp.where(lane >= DA_QK_DIM, q, zero)
    acc_ref[...] = jnp.zeros_like(acc_ref)
    m_ref[...] = jnp.full_like(m_ref, -jnp.inf)

    ones_rows = jnp.ones((ATT_SUM_ROWS, tk), BF16)

    def scores(j, s_ref, slot):
        k_blk = k_ref[0, pl.ds(pl.multiple_of(j * tk, tk), tk), :]
        for mp in range(2):
            st = lax.dot_general(k_blk, qm_ref[mp], (((1,), (1,)), ((), ())),
                                 preferred_element_type=F32)
            s_ref[mp] = st
            mblk_ref[slot, mp] = jnp.max(st, axis=0, keepdims=True)

    def softmax_pv(j, s_ref, slot, masked):
        vt_blk = jnp.concatenate([vt_ref[0, j], ones_rows], axis=0)
        if masked:
            key = lax.broadcasted_iota(jnp.int32, (tk, tq), 0)
            qry = lax.broadcasted_iota(jnp.int32, (tk, tq), 1)
            keep = key <= qry
        for mp in range(2):
            st = s_ref[mp]
            if masked:
                st = jnp.where(keep, st, -jnp.inf)
                m_blk = jnp.max(st, axis=0, keepdims=True)
            else:
                m_blk = mblk_ref[slot, mp]
            m_old = m_ref[mp]
            m_new = jnp.maximum(m_old, m_blk)
            alpha = jnp.exp2(m_old - m_new)
            pexp = jnp.exp2(st - m_new).astype(BF16)
            acc_ref[mp] = alpha * acc_ref[mp] + jnp.dot(vt_blk, pexp,
                                                       preferred_element_type=F32)
            m_ref[mp] = m_new

    scores(0, sa_ref, 0)

    def pair(jj, carry):
        j0 = 2 * jj
        scores(j0 + 1, sb_ref, 1)
        softmax_pv(j0, sa_ref, 0, False)
        scores(j0 + 2, sa_ref, 0)
        softmax_pv(j0 + 1, sb_ref, 1, False)
        return carry

    lax.fori_loop(0, i // 2, pair, 0)

    @pl.when(i % 2 == 1)
    def _():
        scores(i, sb_ref, 1)
        softmax_pv(i - 1, sa_ref, 0, False)
        softmax_pv(i, sb_ref, 1, True)

    @pl.when(i % 2 == 0)
    def _():
        softmax_pv(i, sa_ref, 0, True)

    lp = lam_ref[...]
    lam = (jnp.exp(jnp.sum(lp[0:1, :] * lp[1:2, :], axis=-1, keepdims=True))
           - jnp.exp(jnp.sum(lp[2:3, :] * lp[3:4, :], axis=-1, keepdims=True))
           + lambda_init)
    dv = DA_V_DIM
    o_t = (acc_ref[0, 0:dv, :] / acc_ref[0, dv:dv + 1, :]
           - lam * (acc_ref[1, 0:dv, :] / acc_ref[1, dv:dv + 1, :]))
    ms = jnp.mean(o_t * o_t, axis=0, keepdims=True)
    y_t = o_t * lax.rsqrt(ms + LN_EPS) * (ng_ref[...] * (1.0 - lambda_init))
    out_ref[0] = y_t.T.astype(BF16)


def _diff_attention(dq, dk, dvt, lam_p, ng_col, lambda_init):
    bsz, s, _ = dq.shape
    tq = ATT_Q
    nk, tk = dvt.shape[1], dvt.shape[3]
    assert tq == tk and nk * tk == s
    return pl.pallas_call(
        functools.partial(_attn_kernel, lambda_init=lambda_init),
        grid=(bsz, DA_HEADS, s // tq),
        in_specs=[
            pl.BlockSpec((1, tq, 2 * DA_QK_DIM), lambda b, h, i: (b, i, h)),
            pl.BlockSpec((1, s, 2 * DA_QK_DIM), lambda b, h, i: (b, 0, h)),
            pl.BlockSpec((1, nk, DA_V_DIM, tk), lambda b, h, i: (b, 0, h, 0)),
            pl.BlockSpec(lam_p.shape, lambda b, h, i: (0, 0)),
            pl.BlockSpec(ng_col.shape, lambda b, h, i: (0, 0)),
        ],
        out_specs=pl.BlockSpec((1, tq, DA_V_DIM), lambda b, h, i: (b, i, h)),
        out_shape=jax.ShapeDtypeStruct((bsz, s, DA_HEADS * DA_V_DIM), BF16),
        scratch_shapes=[pltpu.VMEM((2, tq, 2 * DA_QK_DIM), BF16),
                        pltpu.VMEM((2, DA_V_DIM + ATT_SUM_ROWS, tq), F32),
                        pltpu.VMEM((2, tk, tq), F32),
                        pltpu.VMEM((2, tk, tq), F32),
                        pltpu.VMEM((2, 2, 1, tq), F32),
                        pltpu.VMEM((2, 1, tq), F32)],
        compiler_params=_params("parallel", "parallel", "arbitrary"),
        name="diff_attention",
    )(dq, dk, dvt, lam_p, ng_col)


def _out_proj_kernel(yc_ref, ym_ref, yd_ref, x_ref, w_ref, g_ref, b_ref, out_ref):
    wc = yc_ref.shape[1]
    wm = ym_ref.shape[1]
    h = jnp.dot(yc_ref[...], w_ref[0:wc, :], preferred_element_type=F32)
    h = h + jnp.dot(ym_ref[...], w_ref[wc:wc + wm, :], preferred_element_type=F32)
    h = h + jnp.dot(yd_ref[...], w_ref[wc + wm:, :], preferred_element_type=F32)
    out_ref[...] = _layer_norm_rows(DEEPNORM_ALPHA * x_ref[...] + h, g_ref[...], b_ref[...])


def _out_proj(yc, ym, yd, x2, w, g, b):
    n, d = x2.shape
    tm = PROJ_ROWS
    const = lambda i: (0, 0)
    return pl.pallas_call(
        _out_proj_kernel,
        grid=(n // tm,),
        in_specs=[pl.BlockSpec((tm, yc.shape[1]), lambda i: (i, 0)),
                  pl.BlockSpec((tm, ym.shape[1]), lambda i: (i, 0)),
                  pl.BlockSpec((tm, yd.shape[1]), lambda i: (i, 0)),
                  pl.BlockSpec((tm, d), lambda i: (i, 0)),
                  pl.BlockSpec(w.shape, const),
                  pl.BlockSpec(g.shape, const),
                  pl.BlockSpec(b.shape, const)],
        out_specs=pl.BlockSpec((tm, d), lambda i: (i, 0)),
        out_shape=jax.ShapeDtypeStruct((n, d), F32),
        compiler_params=_params("parallel"),
        name="out_proj_ln",
    )(yc, ym, yd, x2, w, g, b)


def _mlp_kernel(x_ref, wu_ref, wd_ref, g_ref, b_ref, out_ref, xb_ref, acc_ref):
    f = pl.program_id(1)

    @pl.when(f == 0)
    def _():
        xb_ref[...] = x_ref[...].astype(BF16)
        acc_ref[...] = jnp.zeros_like(acc_ref)

    up = jnp.maximum(jnp.dot(xb_ref[...], wu_ref[...], preferred_element_type=F32), 0.0)
    acc_ref[...] += jnp.dot((up * up).astype(BF16), wd_ref[...], preferred_element_type=F32)

    @pl.when(f == pl.num_programs(1) - 1)
    def _():
        out_ref[...] = _layer_norm_rows(DEEPNORM_ALPHA * x_ref[...] + acc_ref[...],
                                        g_ref[...], b_ref[...])


def _mlp(x2, wu, wd, g, b):
    n, d = x2.shape
    dff = wu.shape[1]
    tm, tf = MLP_ROWS, MLP_FF
    return pl.pallas_call(
        _mlp_kernel,
        grid=(n // tm, dff // tf),
        in_specs=[pl.BlockSpec((tm, d), lambda i, f: (i, 0)),
                  pl.BlockSpec((d, tf), lambda i, f: (0, f)),
                  pl.BlockSpec((tf, d), lambda i, f: (f, 0)),
                  pl.BlockSpec(g.shape, lambda i, f: (0, 0)),
                  pl.BlockSpec(b.shape, lambda i, f: (0, 0))],
        out_specs=pl.BlockSpec((tm, d), lambda i, f: (i, 0)),
        out_shape=jax.ShapeDtypeStruct((n, d), F32),
        scratch_shapes=[pltpu.VMEM((tm, d), BF16), pltpu.VMEM((tm, d), F32)],
        compiler_params=_params("parallel", "arbitrary"),
        name="mlp_ln",
    )(x2, wu, wd, g, b)


def _row(v):
    return v.reshape(1, -1).astype(F32)


def _pad_rows(w, rows):
    return jnp.pad(w.astype(F32), ((0, rows - w.shape[0]), (0, 0)))


def _rearranged_w_in(w):
    d = w.shape[0]
    gate_lo = 256 + 256 + 512 + 256 + 256
    gate_hi = gate_lo + 2 * ML_HEADS
    v_lo = gate_hi + 2 * DA_HEADS * 2 * DA_QK_DIM
    pad = jnp.zeros((d, LANES - 2 * ML_HEADS), w.dtype)
    w_r = jnp.concatenate([w[:, :gate_lo], w[:, gate_hi:v_lo], w[:, gate_lo:gate_hi], pad], axis=1)
    return w_r.astype(BF16), w[:, v_lo:].T.astype(BF16)


def kernel(x, w_in, b_igate, b_fgate, conv_dw_w, conv_dw_b, conv_ln_g, conv_ln_b, conv_pw_w, conv_pw_b, ml_conv_w, ml_conv_b, ml_norm_g, lam_q1, lam_k1, lam_q2, lam_k2, da_norm_g, w_out, ln1_g, ln1_b, w_up, w_down, ln2_g, ln2_b):
    bsz, s, d = x.shape
    n = bsz * s
    x2 = x.reshape(n, d)
    for l in range(DEPTH):
        lambda_init = 0.8 - 0.6 * math.exp(-0.3 * l)
        w_r, w_vt = _rearranged_w_in(w_in[l])
        glu, mqk, mv, mo, gates, dq, dk, dvt = _in_proj(x2, w_r, w_vt, bsz)
        seq = lambda a: a.reshape(bsz, s, a.shape[1])

        y_conv = _conv_module(seq(glu), _pad_rows(conv_dw_w[l], 32), _row(conv_dw_b[l]),
                              _row(conv_ln_g[l]), _row(conv_ln_b[l]),
                              conv_pw_w[l].astype(BF16), _row(conv_pw_b[l]))

        gate_bias = jnp.pad(jnp.concatenate([b_igate[l], b_fgate[l]]).astype(F32),
                            (0, LANES - 2 * ML_HEADS)).reshape(1, LANES)
        y_ml = _mlstm(seq(mqk), seq(mv), seq(mo), seq(gates),
                      _pad_rows(ml_conv_w[l], SUBLANES), _row(ml_conv_b[l]),
                      gate_bias, _row(ml_norm_g[l]))

        lam_p = jnp.stack([lam_q1[l], lam_k1[l], lam_q2[l], lam_k2[l]]).astype(F32)
        y_da = _diff_attention(seq(dq), seq(dk), dvt, lam_p,
                               da_norm_g[l].astype(F32).reshape(-1, 1), lambda_init)

        x2 = _out_proj(y_conv.reshape(n, -1), y_ml.reshape(n, -1), y_da.reshape(n, -1), x2,
                       w_out[l].astype(BF16), _row(ln1_g[l]), _row(ln1_b[l]))
        x2 = _mlp(x2, w_up[l].astype(BF16), w_down[l].astype(BF16),
                  _row(ln2_g[l]), _row(ln2_b[l]))
    return x2.reshape(bsz, s, d)
```

```python
import functools
import math

import jax
import jax.numpy as jnp
from jax import lax
from jax.experimental import pallas as pl
from jax.experimental.pallas import tpu as pltpu

F32 = jnp.float32
BF16 = jnp.bfloat16

LANES = 128
SUBLANES = 8
VMEM_LIMIT_BYTES = 56 * 1024 * 1024

DEPTH = 2
CONV_KSIZE = 31
ML_HEADS = 4
ML_HEAD_DIM = 64
ML_QK_CONV = 4
ML_CHUNK = 128
DA_HEADS = 4
DA_QK_DIM = 64
DA_V_DIM = 128
LN_EPS = 1e-5
DEEPNORM_ALPHA = (2 * DEPTH) ** 0.25

PROJ_ROWS = 512
CONV_ROWS = 512
CONV_HALO = 32
CONV_SUB = 64
ATT_Q = 512
ATT_K = PROJ_ROWS
ML_BATCH = 4
ATT_SUM_ROWS = 16
MLP_ROWS = 1024
MLP_FF = 512


def _params(*semantics):
    return pltpu.CompilerParams(dimension_semantics=semantics,
                                vmem_limit_bytes=VMEM_LIMIT_BYTES)


def _sigmoid(x):
    return 1.0 / (1.0 + jnp.exp(-x))


def _layer_norm_rows(x, g, b):
    mu = jnp.mean(x, axis=-1, keepdims=True)
    xc = x - mu
    var = jnp.mean(xc * xc, axis=-1, keepdims=True)
    return xc * lax.rsqrt(var + LN_EPS) * g + b


def _in_proj_kernel(x_ref, w_ref, wvt_ref, glu_ref, mqk_ref, mv_ref, mo_ref, gate_ref,
                    dq_ref, dk_ref, dvt_ref):
    xb = x_ref[...].astype(BF16)

    def sec(lo, width):
        return jnp.dot(xb, w_ref[:, lo:lo + width], preferred_element_type=F32)

    glu_ref[...] = sec(0, 256) * _sigmoid(sec(256, 256))
    mqk_ref[...] = sec(512, 512)
    mv_ref[...] = sec(1024, 256)
    mo_ref[...] = sec(1280, 256)
    dq_ref[...] = (sec(1536, 512) * (DA_QK_DIM ** -0.5 * math.log2(math.e))).astype(BF16)
    dk_ref[...] = sec(2048, 512).astype(BF16)
    gate_ref[...] = sec(2560, LANES)
    dvt_ref[0, 0] = lax.dot_general(wvt_ref[...], xb, (((1,), (1,)), ((), ())),
                                    preferred_element_type=F32).astype(BF16)


def _in_proj(x2, w_r, w_vt, bsz):
    n, d = x2.shape
    tm = PROJ_ROWS
    tiles_per_seq = n // bsz // tm
    widths = (256, 512, 256, 256, LANES, 512, 512)
    dtypes = (F32, F32, F32, F32, F32, BF16, BF16)
    wv = w_vt.shape[0]
    return pl.pallas_call(
        _in_proj_kernel,
        grid=(n // tm,),
        in_specs=[pl.BlockSpec((tm, d), lambda i: (i, 0)),
                  pl.BlockSpec(w_r.shape, lambda i: (0, 0)),
                  pl.BlockSpec(w_vt.shape, lambda i: (0, 0))],
        out_specs=[pl.BlockSpec((tm, w), lambda i: (i, 0)) for w in widths]
        + [pl.BlockSpec((1, 1, wv, tm),
                        lambda i: (i // tiles_per_seq, i % tiles_per_seq, 0, 0))],
        out_shape=[jax.ShapeDtypeStruct((n, w), dt) for w, dt in zip(widths, dtypes)]
        + [jax.ShapeDtypeStruct((bsz, tiles_per_seq, wv, tm), BF16)],
        compiler_params=_params("parallel"),
        name="in_proj",
    )(x2, w_r, w_vt)


def _conv_kernel(x_ref, halo_ref, dw_ref, dwb_ref, g_ref, b_ref, pw_ref, pwb_ref,
                 out_ref, xs, ybuf):
    s = pl.program_id(1)
    rows = x_ref.shape[1]
    xs[0, 0:CONV_HALO, :] = jnp.where(s > 0, halo_ref[0], 0.0)
    xs[0, CONV_HALO:CONV_HALO + rows, :] = x_ref[0]
    span = CONV_HALO + rows - SUBLANES
    for k in range(1, SUBLANES):
        xs[k, 0:span, :] = xs[0, k:k + span, :]
    first = CONV_HALO - (CONV_KSIZE - 1)
    for r in range(0, rows, CONV_SUB):
        acc = jnp.broadcast_to(dwb_ref[...], (CONV_SUB, x_ref.shape[2]))
        for j in range(CONV_KSIZE):
            k = (first + j) % SUBLANES
            lo = r + first + j - k
            acc = acc + dw_ref[j:j + 1, :] * xs[k, lo:lo + CONV_SUB, :]
        y = _layer_norm_rows(acc, g_ref[...], b_ref[...])
        ybuf[r:r + CONV_SUB, :] = (y * _sigmoid(y)).astype(BF16)
    out = jnp.dot(ybuf[...], pw_ref[...], preferred_element_type=F32) + pwb_ref[...]
    out_ref[0] = out.astype(BF16)


def _conv_module(glu, dw, dwb, g, b, pw, pwb):
    bsz, s, ch = glu.shape
    ts = CONV_ROWS
    halo_per_tile = ts // CONV_HALO
    return pl.pallas_call(
        _conv_kernel,
        grid=(bsz, s // ts),
        in_specs=[
            pl.BlockSpec((1, ts, ch), lambda bi, si: (bi, si, 0)),
            pl.BlockSpec((1, CONV_HALO, ch),
                         lambda bi, si: (bi, jnp.maximum(si * halo_per_tile - 1, 0), 0)),
            pl.BlockSpec(dw.shape, lambda bi, si: (0, 0)),
            pl.BlockSpec(dwb.shape, lambda bi, si: (0, 0)),
            pl.BlockSpec(g.shape, lambda bi, si: (0, 0)),
            pl.BlockSpec(b.shape, lambda bi, si: (0, 0)),
            pl.BlockSpec(pw.shape, lambda bi, si: (0, 0)),
            pl.BlockSpec(pwb.shape, lambda bi, si: (0, 0)),
        ],
        out_specs=pl.BlockSpec((1, ts, ch), lambda bi, si: (bi, si, 0)),
        out_shape=jax.ShapeDtypeStruct((bsz, s, ch), BF16),
        scratch_shapes=[pltpu.VMEM((SUBLANES, CONV_HALO + ts, ch), F32),
                        pltpu.VMEM((ts, ch), BF16)],
        compiler_params=_params("parallel", "parallel"),
        name="conv_module",
    )(glu, glu, dw, dwb, g, b, pw, pwb)


def _mlstm_kernel(qk_ref, halo_ref, v_ref, o_ref, gate_ref, cw_ref, cb_ref, gb_ref,
                  ng_ref, sel_ref, out_ref, qpad, ct_ref, m_ref):
    c = pl.program_id(1)
    nb = qk_ref.shape[0]
    t = ML_CHUNK
    w_pair = 2 * ML_HEAD_DIM
    k_off = ML_HEADS * ML_HEAD_DIM
    n_pairs = ML_HEADS // 2
    nt = (((1,), (1,)), ((), ()))
    tn = (((0,), (0,)), ((), ()))

    @pl.when(c == 0)
    def _():
        ct_ref[...] = jnp.zeros_like(ct_ref)
        m_ref[...] = jnp.zeros_like(m_ref)

    row = lax.broadcasted_iota(jnp.int32, (t, t), 0)
    col = lax.broadcasted_iota(jnp.int32, (t, t), 1)
    causal = row >= col
    tri = causal.astype(F32)
    lane = lax.broadcasted_iota(jnp.int32, (t, w_pair), 1)
    low = lane < ML_HEAD_DIM
    owns = (low, lane >= ML_HEAD_DIM)

    first = SUBLANES - (ML_QK_CONV - 1)
    qk, gpre_t, log_f_t = [], [], []
    for bb in range(nb):
        qpad[bb, 0:SUBLANES, :] = jnp.where(c > 0, halo_ref[bb], 0.0)
        qpad[bb, SUBLANES:SUBLANES + t, :] = qk_ref[bb]
        acc = jnp.broadcast_to(cb_ref[...], (t, qk_ref.shape[2]))
        for j in range(ML_QK_CONV):
            acc = acc + cw_ref[j:j + 1, :] * qpad[bb, first + j:first + j + t, :]
        qk.append(acc * _sigmoid(acc))
        g_t = (gate_ref[bb] + gb_ref[...]).T[0:SUBLANES, :]
        gpre_t.append(g_t)
        log_f_t.append(jnp.minimum(g_t, 0.0) - jnp.log(1.0 + jnp.exp(-jnp.abs(g_t))))

    tri_t = (row <= col).astype(F32)
    head_row = lax.broadcasted_iota(jnp.int32, (SUBLANES, t), 0) < ML_HEADS
    lane_t = lax.broadcasted_iota(jnp.int32, (SUBLANES, t), 1)
    rb_t, cm_t = [], []
    for bb in range(nb):
        b_t = jnp.dot(log_f_t[bb], tri_t, preferred_element_type=F32,
                      precision=lax.Precision.HIGHEST)
        r_t = gpre_t[bb] - pltpu.roll(b_t, ML_HEADS, 0)
        cm = r_t
        shift = 1
        while shift < t:
            cm = jnp.where(lane_t >= shift, jnp.maximum(cm, pltpu.roll(cm, shift, 1)), cm)
            shift *= 2
        rb_t.append(jnp.where(head_row, r_t, b_t))
        cm_t.append(cm)
    rep = {}
    for bb in range(nb):
        parts = []
        for tile in (rb_t[bb], cm_t[bb]):
            hi = tile.astype(BF16)
            parts += [hi, (tile - hi.astype(F32)).astype(BF16)]
        g_hl = jnp.concatenate(parts, axis=0)
        for h in range(ML_HEADS):
            rep[bb, h] = lax.dot_general(g_hl, sel_ref[h], tn, preferred_element_type=F32)

    items = [(bb, h) for bb in range(nb) for h in range(ML_HEADS)]
    st = {}

    def tile_rows(x_row):
        return jnp.tile(jnp.broadcast_to(x_row, (SUBLANES, LANES)), (t // SUBLANES, 1))

    for bb in range(nb):
        for pair in range(n_pairs):
            lo = pair * w_pair
            k_pair = qk[bb][:, k_off + lo:k_off + lo + w_pair] * (ML_HEAD_DIM ** -0.5)
            st[bb, pair, "q"] = qk[bb][:, lo:lo + w_pair]
            st[bb, pair, "k"] = k_pair
            st[bb, pair, "kbt"] = k_pair.T.astype(BF16)
            st[bb, pair, "v"] = v_ref[bb, :, lo:lo + w_pair]
    for (bb, h) in items:
        pair, par = divmod(h, 2)
        own = owns[par]
        b_col = rep[bb, h][:, 0:LANES]
        r_col = rep[bb, h][:, LANES:2 * LANES]
        cm_col = rep[bb, h][:, 2 * LANES:3 * LANES]
        r_row = tile_rows(rb_t[bb][h:h + 1, :])
        m_prev8 = m_ref[bb, h]
        m_prev = jnp.tile(m_prev8, (t // SUBLANES, 1))
        u_col = jnp.maximum(m_prev, cm_col)
        st[bb, h, "a_inter"] = jnp.exp(m_prev - u_col)
        st[bb, h, "dexp"] = jnp.exp(jnp.where(causal, r_row - u_col, -jnp.inf))
        st[bb, h, "floor"] = jnp.exp(-(b_col + u_col))
        st[bb, h, "q_m"] = jnp.where(own, st[bb, pair, "q"], 0.0).astype(BF16)
        st[bb, h, "vx"] = jnp.where(own, st[bb, pair, "v"], 1.0).astype(BF16)
        g_end8 = jnp.broadcast_to(b_col[t - 1:t, :], (SUBLANES, LANES))
        w_end = jnp.tile(g_end8, (t // SUBLANES, 1)) + r_col
        m_loc8 = jnp.broadcast_to(jnp.max(w_end, axis=0, keepdims=True), (SUBLANES, LANES))
        p_end = jnp.exp(w_end - jnp.tile(m_loc8, (t // SUBLANES, 1)))
        m_new8 = jnp.maximum(g_end8 + m_prev8, m_loc8)
        st[bb, h, "a_st"] = jnp.tile(jnp.exp(g_end8 + m_prev8 - m_new8), (t // SUBLANES, 1))
        st[bb, h, "e_st"] = jnp.tile(jnp.exp(m_loc8 - m_new8), (t // SUBLANES, 1))
        st[bb, h, "m_new"] = m_new8
        st[bb, h, "kp"] = (jnp.where(own, st[bb, pair, "k"], 0.0) * p_end).astype(BF16)

    for (bb, h) in items:
        st[bb, h, "s_qk"] = jnp.dot(st[bb, h, "q_m"], st[bb, h // 2, "kbt"],
                                    preferred_element_type=F32)
    for (bb, h) in items:
        ct_prev = ct_ref[bb, h]
        st[bb, h, "inter"] = jnp.dot(st[bb, h, "q_m"], ct_prev.astype(BF16),
                                     preferred_element_type=F32)
        c_loc = lax.dot_general(st[bb, h, "kp"], st[bb, h, "vx"], tn,
                                preferred_element_type=F32)
        ct_ref[bb, h] = st[bb, h, "a_st"] * ct_prev + st[bb, h, "e_st"] * c_loc
        m_ref[bb, h] = st[bb, h, "m_new"]

    for (bb, h) in items:
        sc = (st[bb, h, "s_qk"] * st[bb, h, "dexp"]).astype(BF16)
        st[bb, h, "intra"] = jnp.dot(sc, st[bb, h, "vx"], preferred_element_type=F32)
    for (bb, h) in items:
        numden = st[bb, h, "a_inter"] * st[bb, h, "inter"] + st[bb, h, "intra"]
        den = pltpu.roll(numden, ML_HEAD_DIM, 1)
        st[bb, h, "hh"] = numden / jnp.maximum(jnp.abs(den), st[bb, h, "floor"])

    inv = 1.0 / ML_HEAD_DIM
    for bb in range(nb):
        for pair in range(n_pairs):
            lo = pair * w_pair
            h_pair = jnp.where(low, st[bb, 2 * pair, "hh"], st[bb, 2 * pair + 1, "hh"])
            x = h_pair * _sigmoid(o_ref[bb, :, lo:lo + w_pair])
            s_low = jnp.sum(jnp.where(low, x, 0.0), axis=-1, keepdims=True)
            s_all = jnp.sum(x, axis=-1, keepdims=True)
            xc = x - jnp.where(low, s_low, s_all - s_low) * inv
            sq = xc * xc
            v_low = jnp.sum(jnp.where(low, sq, 0.0), axis=-1, keepdims=True)
            v_all = jnp.sum(sq, axis=-1, keepdims=True)
            var = jnp.where(low, v_low, v_all - v_low) * inv
            y = xc * lax.rsqrt(var + LN_EPS) * ng_ref[:, lo:lo + w_pair]
            out_ref[bb, :, lo:lo + w_pair] = y.astype(BF16)


def _mlstm(mqk, mv, mo, gates, cw, cb, gb, ng):
    bsz, s, wqk = mqk.shape
    wv = mv.shape[2]
    t = ML_CHUNK
    nb = math.gcd(bsz, ML_BATCH)
    halo_per_chunk = t // SUBLANES
    src = jnp.arange(4 * SUBLANES)[:, None]
    dst = jnp.arange(3 * LANES)[None, :] // LANES
    sel = jnp.stack([((dst == 0) & (src // (2 * SUBLANES) == 0) & (src % SUBLANES == ML_HEADS + h)
                      | (dst == 1) & (src // (2 * SUBLANES) == 0) & (src % SUBLANES == h)
                      | (dst == 2) & (src // (2 * SUBLANES) == 1) & (src % SUBLANES == h)
                      ).astype(BF16) for h in range(ML_HEADS)])
    const = lambda bi, ci: (0, 0)
    return pl.pallas_call(
        _mlstm_kernel,
        grid=(bsz // nb, s // t),
        in_specs=[
            pl.BlockSpec((nb, t, wqk), lambda bi, ci: (bi, ci, 0)),
            pl.BlockSpec((nb, SUBLANES, wqk),
                         lambda bi, ci: (bi, jnp.maximum(ci * halo_per_chunk - 1, 0), 0)),
            pl.BlockSpec((nb, t, wv), lambda bi, ci: (bi, ci, 0)),
            pl.BlockSpec((nb, t, wv), lambda bi, ci: (bi, ci, 0)),
            pl.BlockSpec((nb, t, LANES), lambda bi, ci: (bi, ci, 0)),
            pl.BlockSpec(cw.shape, const),
            pl.BlockSpec(cb.shape, const),
            pl.BlockSpec(gb.shape, const),
            pl.BlockSpec(ng.shape, const),
            pl.BlockSpec(sel.shape, lambda bi, ci: (0, 0, 0)),
        ],
        out_specs=pl.BlockSpec((nb, t, wv), lambda bi, ci: (bi, ci, 0)),
        out_shape=jax.ShapeDtypeStruct((bsz, s, wv), BF16),
        scratch_shapes=[pltpu.VMEM((nb, SUBLANES + t, wqk), F32),
                        pltpu.VMEM((nb, ML_HEADS, 2 * ML_HEAD_DIM, LANES), F32),
                        pltpu.VMEM((nb, ML_HEADS, SUBLANES, LANES), F32)],
        compiler_params=_params("parallel", "arbitrary"),
        name="mlstm",
    )(mqk, mqk, mv, mo, gates, cw, cb, gb, ng, sel)


def _attn_kernel(q_ref, k_ref, vt_ref, lam_ref, ng_ref, out_ref, qm_ref, acc_ref,
                 sa_ref, sb_ref, mblk_ref, m_ref, *, lambda_init):
    i = pl.program_id(2)
    tq = q_ref.shape[1]
    tk = vt_ref.shape[3]

    q = q_ref[0]
    lane = lax.broadcasted_iota(jnp.int32, q.shape, 1)
    zero = jnp.zeros_like(q)
    qm_ref[0] = jnp.where(lane < DA_QK_DIM, q, zero)
    qm_ref[1] = jnp.where(lane >= DA_QK_DIM, q, zero)
    acc_ref[...] = jnp.zeros_like(acc_ref)
    m_ref[...] = jnp.full_like(m_ref, -jnp.inf)

    ones_rows = jnp.ones((ATT_SUM_ROWS, tk), BF16)

    def scores(j, s_ref, slot):
        k_blk = k_ref[0, pl.ds(pl.multiple_of(j * tk, tk), tk), :]
        for mp in range(2):
            st = lax.dot_general(k_blk, qm_ref[mp], (((1,), (1,)), ((), ())),
                                 preferred_element_type=F32)
            s_ref[mp] = st
            mblk_ref[slot, mp] = jnp.max(st, axis=0, keepdims=True)

    def softmax_pv(j, s_ref, slot, masked):
        vt_blk = jnp.concatenate([vt_ref[0, j], ones_rows], axis=0)
        if masked:
            key = lax.broadcasted_iota(jnp.int32, (tk, tq), 0)
            qry = lax.broadcasted_iota(jnp.int32, (tk, tq), 1)
            keep = key <= qry
        for mp in range(2):
            st = s_ref[mp]
            if masked:
                st = jnp.where(keep, st, -jnp.inf)
                m_blk = jnp.max(st, axis=0, keepdims=True)
            else:
                m_blk = mblk_ref[slot, mp]
            m_old = m_ref[mp]
            m_new = jnp.maximum(m_old, m_blk)
            alpha = jnp.exp2(m_old - m_new)
            pexp = jnp.exp2(st - m_new).astype(BF16)
            acc_ref[mp] = alpha * acc_ref[mp] + jnp.dot(vt_blk, pexp,
                                                       preferred_element_type=F32)
            m_ref[mp] = m_new

    scores(0, sa_ref, 0)

    def pair(jj, carry):
        j0 = 2 * jj
        scores(j0 + 1, sb_ref, 1)
        softmax_pv(j0, sa_ref, 0, False)
        scores(j0 + 2, sa_ref, 0)
        softmax_pv(j0 + 1, sb_ref, 1, False)
        return carry

    lax.fori_loop(0, i // 2, pair, 0)

    @pl.when(i % 2 == 1)
    def _():
        scores(i, sb_ref, 1)
        softmax_pv(i - 1, sa_ref, 0, False)
        softmax_pv(i, sb_ref, 1, True)

    @pl.when(i % 2 == 0)
    def _():
        softmax_pv(i, sa_ref, 0, True)

    lp = lam_ref[...]
    lam = (jnp.exp(jnp.sum(lp[0:1, :] * lp[1:2, :], axis=-1, keepdims=True))
           - jnp.exp(jnp.sum(lp[2:3, :] * lp[3:4, :], axis=-1, keepdims=True))
           + lambda_init)
    dv = DA_V_DIM
    o_t = (acc_ref[0, 0:dv, :] / acc_ref[0, dv:dv + 1, :]
           - lam * (acc_ref[1, 0:dv, :] / acc_ref[1, dv:dv + 1, :]))
    ms = jnp.mean(o_t * o_t, axis=0, keepdims=True)
    y_t = o_t * lax.rsqrt(ms + LN_EPS) * (ng_ref[...] * (1.0 - lambda_init))
    out_ref[0] = y_t.T.astype(BF16)


def _diff_attention(dq, dk, dvt, lam_p, ng_col, lambda_init):
    bsz, s, _ = dq.shape
    tq = ATT_Q
    nk, tk = dvt.shape[1], dvt.shape[3]
    assert tq == tk and nk * tk == s
    return pl.pallas_call(
        functools.partial(_attn_kernel, lambda_init=lambda_init),
        grid=(bsz, DA_HEADS, s // tq),
        in_specs=[
            pl.BlockSpec((1, tq, 2 * DA_QK_DIM), lambda b, h, i: (b, i, h)),
            pl.BlockSpec((1, s, 2 * DA_QK_DIM), lambda b, h, i: (b, 0, h)),
            pl.BlockSpec((1, nk, DA_V_DIM, tk), lambda b, h, i: (b, 0, h, 0)),
            pl.BlockSpec(lam_p.shape, lambda b, h, i: (0, 0)),
            pl.BlockSpec(ng_col.shape, lambda b, h, i: (0, 0)),
        ],
        out_specs=pl.BlockSpec((1, tq, DA_V_DIM), lambda b, h, i: (b, i, h)),
        out_shape=jax.ShapeDtypeStruct((bsz, s, DA_HEADS * DA_V_DIM), BF16),
        scratch_shapes=[pltpu.VMEM((2, tq, 2 * DA_QK_DIM), BF16),
                        pltpu.VMEM((2, DA_V_DIM + ATT_SUM_ROWS, tq), F32),
                        pltpu.VMEM((2, tk, tq), F32),
                        pltpu.VMEM((2, tk, tq), F32),
                        pltpu.VMEM((2, 2, 1, tq), F32),
                        pltpu.VMEM((2, 1, tq), F32)],
        compiler_params=_params("parallel", "parallel", "arbitrary"),
        name="diff_attention",
    )(dq, dk, dvt, lam_p, ng_col)


def _out_proj_kernel(yc_ref, ym_ref, yd_ref, x_ref, w_ref, g_ref, b_ref, out_ref):
    wc = yc_ref.shape[1]
    wm = ym_ref.shape[1]
    h = jnp.dot(yc_ref[...], w_ref[0:wc, :], preferred_element_type=F32)
    h = h + jnp.dot(ym_ref[...], w_ref[wc:wc + wm, :], preferred_element_type=F32)
    h = h + jnp.dot(yd_ref[...], w_ref[wc + wm:, :], preferred_element_type=F32)
    out_ref[...] = _layer_norm_rows(DEEPNORM_ALPHA * x_ref[...] + h, g_ref[...], b_ref[...])


def _out_proj(yc, ym, yd, x2, w, g, b):
    n, d = x2.shape
    tm = PROJ_ROWS
    const = lambda i: (0, 0)
    return pl.pallas_call(
        _out_proj_kernel,
        grid=(n // tm,),
        in_specs=[pl.BlockSpec((tm, yc.shape[1]), lambda i: (i, 0)),
                  pl.BlockSpec((tm, ym.shape[1]), lambda i: (i, 0)),
                  pl.BlockSpec((tm, yd.shape[1]), lambda i: (i, 0)),
                  pl.BlockSpec((tm, d), lambda i: (i, 0)),
                  pl.BlockSpec(w.shape, const),
                  pl.BlockSpec(g.shape, const),
                  pl.BlockSpec(b.shape, const)],
        out_specs=pl.BlockSpec((tm, d), lambda i: (i, 0)),
        out_shape=jax.ShapeDtypeStruct((n, d), F32),
        compiler_params=_params("parallel"),
        name="out_proj_ln",
    )(yc, ym, yd, x2, w, g, b)


def _mlp_kernel(x_ref, wu_ref, wd_ref, g_ref, b_ref, out_ref, xb_ref, acc_ref):
    f = pl.program_id(1)

    @pl.when(f == 0)
    def _():
        xb_ref[...] = x_ref[...].astype(BF16)
        acc_ref[...] = jnp.zeros_like(acc_ref)

    up = jnp.maximum(jnp.dot(xb_ref[...], wu_ref[...], preferred_element_type=F32), 0.0)
    acc_ref[...] += jnp.dot((up * up).astype(BF16), wd_ref[...], preferred_element_type=F32)

    @pl.when(f == pl.num_programs(1) - 1)
    def _():
        out_ref[...] = _layer_norm_rows(DEEPNORM_ALPHA * x_ref[...] + acc_ref[...],
                                        g_ref[...], b_ref[...])


def _mlp(x2, wu, wd, g, b):
    n, d = x2.shape
    dff = wu.shape[1]
    tm, tf = MLP_ROWS, MLP_FF
    return pl.pallas_call(
        _mlp_kernel,
        grid=(n // tm, dff // tf),
        in_specs=[pl.BlockSpec((tm, d), lambda i, f: (i, 0)),
                  pl.BlockSpec((d, tf), lambda i, f: (0, f)),
                  pl.BlockSpec((tf, d), lambda i, f: (f, 0)),
                  pl.BlockSpec(g.shape, lambda i, f: (0, 0)),
                  pl.BlockSpec(b.shape, lambda i, f: (0, 0))],
        out_specs=pl.BlockSpec((tm, d), lambda i, f: (i, 0)),
        out_shape=jax.ShapeDtypeStruct((n, d), F32),
        scratch_shapes=[pltpu.VMEM((tm, d), BF16), pltpu.VMEM((tm, d), F32)],
        compiler_params=_params("parallel", "arbitrary"),
        name="mlp_ln",
    )(x2, wu, wd, g, b)


def _row(v):
    return v.reshape(1, -1).astype(F32)


def _pad_rows(w, rows):
    return jnp.pad(w.astype(F32), ((0, rows - w.shape[0]), (0, 0)))


def _rearranged_w_in(w):
    d = w.shape[0]
    gate_lo = 256 + 256 + 512 + 256 + 256
    gate_hi = gate_lo + 2 * ML_HEADS
    v_lo = gate_hi + 2 * DA_HEADS * 2 * DA_QK_DIM
    pad = jnp.zeros((d, LANES - 2 * ML_HEADS), w.dtype)
    w_r = jnp.concatenate([w[:, :gate_lo], w[:, gate_hi:v_lo], w[:, gate_lo:gate_hi], pad], axis=1)
    return w_r.astype(BF16), w[:, v_lo:].T.astype(BF16)


def kernel(x, w_in, b_igate, b_fgate, conv_dw_w, conv_dw_b, conv_ln_g, conv_ln_b, conv_pw_w, conv_pw_b, ml_conv_w, ml_conv_b, ml_norm_g, lam_q1, lam_k1, lam_q2, lam_k2, da_norm_g, w_out, ln1_g, ln1_b, w_up, w_down, ln2_g, ln2_b):
    bsz, s, d = x.shape
    n = bsz * s
    x2 = x.reshape(n, d)
    for l in range(DEPTH):
        lambda_init = 0.8 - 0.6 * math.exp(-0.3 * l)
        w_r, w_vt = _rearranged_w_in(w_in[l])
        glu, mqk, mv, mo, gates, dq, dk, dvt = _in_proj(x2, w_r, w_vt, bsz)
        seq = lambda a: a.reshape(bsz, s, a.shape[1])

        y_conv = _conv_module(seq(glu), _pad_rows(conv_dw_w[l], 32), _row(conv_dw_b[l]),
                              _row(conv_ln_g[l]), _row(conv_ln_b[l]),
                              conv_pw_w[l].astype(BF16), _row(conv_pw_b[l]))

        gate_bias = jnp.pad(jnp.concatenate([b_igate[l], b_fgate[l]]).astype(F32),
                            (0, LANES - 2 * ML_HEADS)).reshape(1, LANES)
        y_ml = _mlstm(seq(mqk), seq(mv), seq(mo), seq(gates),
                      _pad_rows(ml_conv_w[l], SUBLANES), _row(ml_conv_b[l]),
                      gate_bias, _row(ml_norm_g[l]))

        lam_p = jnp.stack([lam_q1[l], lam_k1[l], lam_q2[l], lam_k2[l]]).astype(F32)
        y_da = _diff_attention(seq(dq), seq(dk), dvt, lam_p,
                               da_norm_g[l].astype(F32).reshape(-1, 1), lambda_init)

        x2 = _out_proj(y_conv.reshape(n, -1), y_ml.reshape(n, -1), y_da.reshape(n, -1), x2,
                       w_out[l].astype(BF16), _row(ln1_g[l]), _row(ln1_b[l]))
        x2 = _mlp(x2, w_up[l].astype(BF16), w_down[l].astype(BF16),
                  _row(ln2_g[l]), _row(ln2_b[l]))
    return x2.reshape(bsz, s, d)
```

```python
import functools
import math

import jax
import jax.numpy as jnp
from jax import lax
from jax.experimental import pallas as pl
from jax.experimental.pallas import tpu as pltpu

F32 = jnp.float32
BF16 = jnp.bfloat16

LANES = 128
SUBLANES = 8
VMEM_LIMIT_BYTES = 56 * 1024 * 1024

DEPTH = 2
CONV_KSIZE = 31
ML_HEADS = 4
ML_HEAD_DIM = 64
ML_QK_CONV = 4
ML_CHUNK = 128
DA_HEADS = 4
DA_QK_DIM = 64
DA_V_DIM = 128
LN_EPS = 1e-5
DEEPNORM_ALPHA = (2 * DEPTH) ** 0.25

PROJ_ROWS = 512
CONV_ROWS = 512
CONV_HALO = 32
CONV_SUB = 64
ATT_Q = 512
ATT_K = PROJ_ROWS
ML_BATCH = 4
ATT_SUM_ROWS = 16
MLP_ROWS = 1024
MLP_FF = 1024
MLP_SUB = 256


def _params(*semantics):
    return pltpu.CompilerParams(dimension_semantics=semantics,
                                vmem_limit_bytes=VMEM_LIMIT_BYTES)


def _sigmoid(x):
    return 1.0 / (1.0 + jnp.exp(-x))


def _layer_norm_rows(x, g, b):
    mu = jnp.mean(x, axis=-1, keepdims=True)
    xc = x - mu
    var = jnp.mean(xc * xc, axis=-1, keepdims=True)
    return xc * lax.rsqrt(var + LN_EPS) * g + b


def _in_proj_kernel(x_ref, w_ref, wvt_ref, glu_ref, mqk_ref, mv_ref, mo_ref, gate_ref,
                    dq_ref, dk_ref, dvt_ref):
    xb = x_ref[...].astype(BF16)

    def sec(lo, width):
        return jnp.dot(xb, w_ref[:, lo:lo + width], preferred_element_type=F32)

    glu_ref[...] = sec(0, 256) * _sigmoid(sec(256, 256))
    mqk_ref[...] = sec(512, 512)
    mv_ref[...] = sec(1024, 256)
    mo_ref[...] = sec(1280, 256)
    dq_ref[...] = (sec(1536, 512) * (DA_QK_DIM ** -0.5 * math.log2(math.e))).astype(BF16)
    dk_ref[...] = sec(2048, 512).astype(BF16)
    gate_ref[...] = sec(2560, LANES)
    dvt_ref[0, 0] = lax.dot_general(wvt_ref[...], xb, (((1,), (1,)), ((), ())),
                                    preferred_element_type=F32).astype(BF16)


def _in_proj(x2, w_r, w_vt, bsz):
    n, d = x2.shape
    tm = PROJ_ROWS
    tiles_per_seq = n // bsz // tm
    widths = (256, 512, 256, 256, LANES, 512, 512)
    dtypes = (F32, F32, F32, F32, F32, BF16, BF16)
    wv = w_vt.shape[0]
    return pl.pallas_call(
        _in_proj_kernel,
        grid=(n // tm,),
        in_specs=[pl.BlockSpec((tm, d), lambda i: (i, 0)),
                  pl.BlockSpec(w_r.shape, lambda i: (0, 0)),
                  pl.BlockSpec(w_vt.shape, lambda i: (0, 0))],
        out_specs=[pl.BlockSpec((tm, w), lambda i: (i, 0)) for w in widths]
        + [pl.BlockSpec((1, 1, wv, tm),
                        lambda i: (i // tiles_per_seq, i % tiles_per_seq, 0, 0))],
        out_shape=[jax.ShapeDtypeStruct((n, w), dt) for w, dt in zip(widths, dtypes)]
        + [jax.ShapeDtypeStruct((bsz, tiles_per_seq, wv, tm), BF16)],
        compiler_params=_params("parallel"),
        name="in_proj",
    )(x2, w_r, w_vt)


def _conv_kernel(x_ref, halo_ref, dw_ref, dwb_ref, g_ref, b_ref, pw_ref, pwb_ref,
                 out_ref, xs, ybuf):
    s = pl.program_id(1)
    rows = x_ref.shape[1]
    xs[0, 0:CONV_HALO, :] = jnp.where(s > 0, halo_ref[0], 0.0)
    xs[0, CONV_HALO:CONV_HALO + rows, :] = x_ref[0]
    span = CONV_HALO + rows - SUBLANES
    for k in range(1, SUBLANES):
        xs[k, 0:span, :] = xs[0, k:k + span, :]
    first = CONV_HALO - (CONV_KSIZE - 1)
    for r in range(0, rows, CONV_SUB):
        acc = jnp.broadcast_to(dwb_ref[...], (CONV_SUB, x_ref.shape[2]))
        for j in range(CONV_KSIZE):
            k = (first + j) % SUBLANES
            lo = r + first + j - k
            acc = acc + dw_ref[j:j + 1, :] * xs[k, lo:lo + CONV_SUB, :]
        y = _layer_norm_rows(acc, g_ref[...], b_ref[...])
        ybuf[r:r + CONV_SUB, :] = (y * _sigmoid(y)).astype(BF16)
    out = jnp.dot(ybuf[...], pw_ref[...], preferred_element_type=F32) + pwb_ref[...]
    out_ref[0] = out.astype(BF16)


def _conv_module(glu, dw, dwb, g, b, pw, pwb):
    bsz, s, ch = glu.shape
    ts = CONV_ROWS
    halo_per_tile = ts // CONV_HALO
    return pl.pallas_call(
        _conv_kernel,
        grid=(bsz, s // ts),
        in_specs=[
            pl.BlockSpec((1, ts, ch), lambda bi, si: (bi, si, 0)),
            pl.BlockSpec((1, CONV_HALO, ch),
                         lambda bi, si: (bi, jnp.maximum(si * halo_per_tile - 1, 0), 0)),
            pl.BlockSpec(dw.shape, lambda bi, si: (0, 0)),
            pl.BlockSpec(dwb.shape, lambda bi, si: (0, 0)),
            pl.BlockSpec(g.shape, lambda bi, si: (0, 0)),
            pl.BlockSpec(b.shape, lambda bi, si: (0, 0)),
            pl.BlockSpec(pw.shape, lambda bi, si: (0, 0)),
            pl.BlockSpec(pwb.shape, lambda bi, si: (0, 0)),
        ],
        out_specs=pl.BlockSpec((1, ts, ch), lambda bi, si: (bi, si, 0)),
        out_shape=jax.ShapeDtypeStruct((bsz, s, ch), BF16),
        scratch_shapes=[pltpu.VMEM((SUBLANES, CONV_HALO + ts, ch), F32),
                        pltpu.VMEM((ts, ch), BF16)],
        compiler_params=_params("parallel", "parallel"),
        name="conv_module",
    )(glu, glu, dw, dwb, g, b, pw, pwb)


def _mlstm_kernel(qk_ref, halo_ref, v_ref, o_ref, gate_ref, cw_ref, cb_ref, gb_ref,
                  ng_ref, sel_ref, out_ref, qpad, ct_ref, m_ref):
    c = pl.program_id(1)
    nb = qk_ref.shape[0]
    t = ML_CHUNK
    w_pair = 2 * ML_HEAD_DIM
    k_off = ML_HEADS * ML_HEAD_DIM
    n_pairs = ML_HEADS // 2
    nt = (((1,), (1,)), ((), ()))
    tn = (((0,), (0,)), ((), ()))

    @pl.when(c == 0)
    def _():
        ct_ref[...] = jnp.zeros_like(ct_ref)
        m_ref[...] = jnp.zeros_like(m_ref)

    row = lax.broadcasted_iota(jnp.int32, (t, t), 0)
    col = lax.broadcasted_iota(jnp.int32, (t, t), 1)
    causal = row >= col
    tri = causal.astype(F32)
    lane = lax.broadcasted_iota(jnp.int32, (t, w_pair), 1)
    low = lane < ML_HEAD_DIM
    owns = (low, lane >= ML_HEAD_DIM)

    first = SUBLANES - (ML_QK_CONV - 1)
    qk, gpre_t, log_f_t = [], [], []
    for bb in range(nb):
        qpad[bb, 0:SUBLANES, :] = jnp.where(c > 0, halo_ref[bb], 0.0)
        qpad[bb, SUBLANES:SUBLANES + t, :] = qk_ref[bb]
        acc = jnp.broadcast_to(cb_ref[...], (t, qk_ref.shape[2]))
        for j in range(ML_QK_CONV):
            acc = acc + cw_ref[j:j + 1, :] * qpad[bb, first + j:first + j + t, :]
        qk.append(acc * _sigmoid(acc))
        g_t = (gate_ref[bb] + gb_ref[...]).T[0:SUBLANES, :]
        gpre_t.append(g_t)
        log_f_t.append(jnp.minimum(g_t, 0.0) - jnp.log(1.0 + jnp.exp(-jnp.abs(g_t))))

    tri_t = (row <= col).astype(F32)
    head_row = lax.broadcasted_iota(jnp.int32, (SUBLANES, t), 0) < ML_HEADS
    lane_t = lax.broadcasted_iota(jnp.int32, (SUBLANES, t), 1)
    rb_t, cm_t = [], []
    for bb in range(nb):
        b_t = jnp.dot(log_f_t[bb], tri_t, preferred_element_type=F32,
                      precision=lax.Precision.HIGHEST)
        r_t = gpre_t[bb] - pltpu.roll(b_t, ML_HEADS, 0)
        cm = r_t
        shift = 1
        while shift < t:
            cm = jnp.where(lane_t >= shift, jnp.maximum(cm, pltpu.roll(cm, shift, 1)), cm)
            shift *= 2
        rb_t.append(jnp.where(head_row, r_t, b_t))
        cm_t.append(cm)
    rep = {}
    for bb in range(nb):
        parts = []
        for tile in (rb_t[bb], cm_t[bb]):
            hi = tile.astype(BF16)
            parts += [hi, (tile - hi.astype(F32)).astype(BF16)]
        g_hl = jnp.concatenate(parts, axis=0)
        for h in range(ML_HEADS):
            rep[bb, h] = lax.dot_general(g_hl, sel_ref[h], tn, preferred_element_type=F32)

    items = [(bb, h) for bb in range(nb) for h in range(ML_HEADS)]
    st = {}

    def tile_rows(x_row):
        return jnp.tile(jnp.broadcast_to(x_row, (SUBLANES, LANES)), (t // SUBLANES, 1))

    for bb in range(nb):
        for pair in range(n_pairs):
            lo = pair * w_pair
            k_pair = qk[bb][:, k_off + lo:k_off + lo + w_pair] * (ML_HEAD_DIM ** -0.5)
            st[bb, pair, "q"] = qk[bb][:, lo:lo + w_pair]
            st[bb, pair, "k"] = k_pair
            st[bb, pair, "kbt"] = k_pair.T.astype(BF16)
            st[bb, pair, "v"] = v_ref[bb, :, lo:lo + w_pair]
    for (bb, h) in items:
        pair, par = divmod(h, 2)
        own = owns[par]
        b_col = rep[bb, h][:, 0:LANES]
        r_col = rep[bb, h][:, LANES:2 * LANES]
        cm_col = rep[bb, h][:, 2 * LANES:3 * LANES]
        r_row = tile_rows(rb_t[bb][h:h + 1, :])
        m_prev8 = m_ref[bb, h]
        m_prev = jnp.tile(m_prev8, (t // SUBLANES, 1))
        u_col = jnp.maximum(m_prev, cm_col)
        st[bb, h, "a_inter"] = jnp.exp(m_prev - u_col)
        st[bb, h, "dexp"] = jnp.exp(jnp.where(causal, r_row - u_col, -jnp.inf))
        st[bb, h, "floor"] = jnp.exp(-(b_col + u_col))
        st[bb, h, "q_m"] = jnp.where(own, st[bb, pair, "q"], 0.0).astype(BF16)
        st[bb, h, "vx"] = jnp.where(own, st[bb, pair, "v"], 1.0).astype(BF16)
        g_end8 = jnp.broadcast_to(b_col[t - 1:t, :], (SUBLANES, LANES))
        w_end = jnp.tile(g_end8, (t // SUBLANES, 1)) + r_col
        m_loc8 = jnp.broadcast_to(jnp.max(w_end, axis=0, keepdims=True), (SUBLANES, LANES))
        p_end = jnp.exp(w_end - jnp.tile(m_loc8, (t // SUBLANES, 1)))
        m_new8 = jnp.maximum(g_end8 + m_prev8, m_loc8)
        st[bb, h, "a_st"] = jnp.tile(jnp.exp(g_end8 + m_prev8 - m_new8), (t // SUBLANES, 1))
        st[bb, h, "e_st"] = jnp.tile(jnp.exp(m_loc8 - m_new8), (t // SUBLANES, 1))
        st[bb, h, "m_new"] = m_new8
        st[bb, h, "kp"] = (jnp.where(own, st[bb, pair, "k"], 0.0) * p_end).astype(BF16)

    for (bb, h) in items:
        st[bb, h, "s_qk"] = jnp.dot(st[bb, h, "q_m"], st[bb, h // 2, "kbt"],
                                    preferred_element_type=F32)
    for (bb, h) in items:
        ct_prev = ct_ref[bb, h]
        st[bb, h, "inter"] = jnp.dot(st[bb, h, "q_m"], ct_prev.astype(BF16),
                                     preferred_element_type=F32)
        c_loc = lax.dot_general(st[bb, h, "kp"], st[bb, h, "vx"], tn,
                                preferred_element_type=F32)
        ct_ref[bb, h] = st[bb, h, "a_st"] * ct_prev + st[bb, h, "e_st"] * c_loc
        m_ref[bb, h] = st[bb, h, "m_new"]

    for (bb, h) in items:
        sc = (st[bb, h, "s_qk"] * st[bb, h, "dexp"]).astype(BF16)
        st[bb, h, "intra"] = jnp.dot(sc, st[bb, h, "vx"], preferred_element_type=F32)
    for (bb, h) in items:
        numden = st[bb, h, "a_inter"] * st[bb, h, "inter"] + st[bb, h, "intra"]
        den = pltpu.roll(numden, ML_HEAD_DIM, 1)
        st[bb, h, "hh"] = numden / jnp.maximum(jnp.abs(den), st[bb, h, "floor"])

    inv = 1.0 / ML_HEAD_DIM
    for bb in range(nb):
        for pair in range(n_pairs):
            lo = pair * w_pair
            h_pair = jnp.where(low, st[bb, 2 * pair, "hh"], st[bb, 2 * pair + 1, "hh"])
            x = h_pair * _sigmoid(o_ref[bb, :, lo:lo + w_pair])
            s_low = jnp.sum(jnp.where(low, x, 0.0), axis=-1, keepdims=True)
            s_all = jnp.sum(x, axis=-1, keepdims=True)
            xc = x - jnp.where(low, s_low, s_all - s_low) * inv
            sq = xc * xc
            v_low = jnp.sum(jnp.where(low, sq, 0.0), axis=-1, keepdims=True)
            v_all = jnp.sum(sq, axis=-1, keepdims=True)
            var = jnp.where(low, v_low, v_all - v_low) * inv
            y = xc * lax.rsqrt(var + LN_EPS) * ng_ref[:, lo:lo + w_pair]
            out_ref[bb, :, lo:lo + w_pair] = y.astype(BF16)


def _mlstm(mqk, mv, mo, gates, cw, cb, gb, ng):
    bsz, s, wqk = mqk.shape
    wv = mv.shape[2]
    t = ML_CHUNK
    nb = math.gcd(bsz, ML_BATCH)
    halo_per_chunk = t // SUBLANES
    src = jnp.arange(4 * SUBLANES)[:, None]
    dst = jnp.arange(3 * LANES)[None, :] // LANES
    sel = jnp.stack([((dst == 0) & (src // (2 * SUBLANES) == 0) & (src % SUBLANES == ML_HEADS + h)
                      | (dst == 1) & (src // (2 * SUBLANES) == 0) & (src % SUBLANES == h)
                      | (dst == 2) & (src // (2 * SUBLANES) == 1) & (src % SUBLANES == h)
                      ).astype(BF16) for h in range(ML_HEADS)])
    const = lambda bi, ci: (0, 0)
    return pl.pallas_call(
        _mlstm_kernel,
        grid=(bsz // nb, s // t),
        in_specs=[
            pl.BlockSpec((nb, t, wqk), lambda bi, ci: (bi, ci, 0)),
            pl.BlockSpec((nb, SUBLANES, wqk),
                         lambda bi, ci: (bi, jnp.maximum(ci * halo_per_chunk - 1, 0), 0)),
            pl.BlockSpec((nb, t, wv), lambda bi, ci: (bi, ci, 0)),
            pl.BlockSpec((nb, t, wv), lambda bi, ci: (bi, ci, 0)),
            pl.BlockSpec((nb, t, LANES), lambda bi, ci: (bi, ci, 0)),
            pl.BlockSpec(cw.shape, const),
            pl.BlockSpec(cb.shape, const),
            pl.BlockSpec(gb.shape, const),
            pl.BlockSpec(ng.shape, const),
            pl.BlockSpec(sel.shape, lambda bi, ci: (0, 0, 0)),
        ],
        out_specs=pl.BlockSpec((nb, t, wv), lambda bi, ci: (bi, ci, 0)),
        out_shape=jax.ShapeDtypeStruct((bsz, s, wv), BF16),
        scratch_shapes=[pltpu.VMEM((nb, SUBLANES + t, wqk), F32),
                        pltpu.VMEM((nb, ML_HEADS, 2 * ML_HEAD_DIM, LANES), F32),
                        pltpu.VMEM((nb, ML_HEADS, SUBLANES, LANES), F32)],
        compiler_params=_params("parallel", "arbitrary"),
        name="mlstm",
    )(mqk, mqk, mv, mo, gates, cw, cb, gb, ng, sel)


def _attn_kernel(q_ref, k_ref, vt_ref, lam_ref, ng_ref, out_ref, qm_ref, acc_ref,
                 sa_ref, sb_ref, mblk_ref, m_ref, *, lambda_init):
    i = pl.program_id(2)
    tq = q_ref.shape[1]
    tk = vt_ref.shape[3]

    q = q_ref[0]
    lane = lax.broadcasted_iota(jnp.int32, q.shape, 1)
    zero = jnp.zeros_like(q)
    qm_ref[0] = jnp.where(lane < DA_QK_DIM, q, zero)
    qm_ref[1] = jnp.where(lane >= DA_QK_DIM, q, zero)
    acc_ref[...] = jnp.zeros_like(acc_ref)
    m_ref[...] = jnp.full_like(m_ref, -jnp.inf)

    ones_rows = jnp.ones((ATT_SUM_ROWS, tk), BF16)

    def scores(j, s_ref, slot):
        k_blk = k_ref[0, pl.ds(pl.multiple_of(j * tk, tk), tk), :]
        for mp in range(2):
            st = lax.dot_general(k_blk, qm_ref[mp], (((1,), (1,)), ((), ())),
                                 preferred_element_type=F32)
            s_ref[mp] = st
            mblk_ref[slot, mp] = jnp.max(st, axis=0, keepdims=True)

    def softmax_pv(j, s_ref, slot, masked):
        vt_blk = jnp.concatenate([vt_ref[0, j], ones_rows], axis=0)
        if masked:
            key = lax.broadcasted_iota(jnp.int32, (tk, tq), 0)
            qry = lax.broadcasted_iota(jnp.int32, (tk, tq), 1)
            keep = key <= qry
        for mp in range(2):
            st = s_ref[mp]
            if masked:
                st = jnp.where(keep, st, -jnp.inf)
                m_blk = jnp.max(st, axis=0, keepdims=True)
            else:
                m_blk = mblk_ref[slot, mp]
            m_old = m_ref[mp]
            m_new = jnp.maximum(m_old, m_blk)
            alpha = jnp.exp2(m_old - m_new)
            pexp = jnp.exp2(st - m_new).astype(BF16)
            acc_ref[mp] = alpha * acc_ref[mp] + jnp.dot(vt_blk, pexp,
                                                       preferred_element_type=F32)
            m_ref[mp] = m_new

    scores(0, sa_ref, 0)

    def pair(jj, carry):
        j0 = 2 * jj
        scores(j0 + 1, sb_ref, 1)
        softmax_pv(j0, sa_ref, 0, False)
        scores(j0 + 2, sa_ref, 0)
        softmax_pv(j0 + 1, sb_ref, 1, False)
        return carry

    lax.fori_loop(0, i // 2, pair, 0)

    @pl.when(i % 2 == 1)
    def _():
        scores(i, sb_ref, 1)
        softmax_pv(i - 1, sa_ref, 0, False)
        softmax_pv(i, sb_ref, 1, True)

    @pl.when(i % 2 == 0)
    def _():
        softmax_pv(i, sa_ref, 0, True)

    lp = lam_ref[...]
    lam = (jnp.exp(jnp.sum(lp[0:1, :] * lp[1:2, :], axis=-1, keepdims=True))
           - jnp.exp(jnp.sum(lp[2:3, :] * lp[3:4, :], axis=-1, keepdims=True))
           + lambda_init)
    dv = DA_V_DIM
    o_t = (acc_ref[0, 0:dv, :] / acc_ref[0, dv:dv + 1, :]
           - lam * (acc_ref[1, 0:dv, :] / acc_ref[1, dv:dv + 1, :]))
    ms = jnp.mean(o_t * o_t, axis=0, keepdims=True)
    y_t = o_t * lax.rsqrt(ms + LN_EPS) * (ng_ref[...] * (1.0 - lambda_init))
    out_ref[0] = y_t.T.astype(BF16)


def _diff_attention(dq, dk, dvt, lam_p, ng_col, lambda_init):
    bsz, s, _ = dq.shape
    tq = ATT_Q
    nk, tk = dvt.shape[1], dvt.shape[3]
    assert tq == tk and nk * tk == s
    return pl.pallas_call(
        functools.partial(_attn_kernel, lambda_init=lambda_init),
        grid=(bsz, DA_HEADS, s // tq),
        in_specs=[
            pl.BlockSpec((1, tq, 2 * DA_QK_DIM), lambda b, h, i: (b, i, h)),
            pl.BlockSpec((1, s, 2 * DA_QK_DIM), lambda b, h, i: (b, 0, h)),
            pl.BlockSpec((1, nk, DA_V_DIM, tk), lambda b, h, i: (b, 0, h, 0)),
            pl.BlockSpec(lam_p.shape, lambda b, h, i: (0, 0)),
            pl.BlockSpec(ng_col.shape, lambda b, h, i: (0, 0)),
        ],
        out_specs=pl.BlockSpec((1, tq, DA_V_DIM), lambda b, h, i: (b, i, h)),
        out_shape=jax.ShapeDtypeStruct((bsz, s, DA_HEADS * DA_V_DIM), BF16),
        scratch_shapes=[pltpu.VMEM((2, tq, 2 * DA_QK_DIM), BF16),
                        pltpu.VMEM((2, DA_V_DIM + ATT_SUM_ROWS, tq), F32),
                        pltpu.VMEM((2, tk, tq), F32),
                        pltpu.VMEM((2, tk, tq), F32),
                        pltpu.VMEM((2, 2, 1, tq), F32),
                        pltpu.VMEM((2, 1, tq), F32)],
        compiler_params=_params("parallel", "parallel", "arbitrary"),
        name="diff_attention",
    )(dq, dk, dvt, lam_p, ng_col)


def _out_mlp_kernel(yc_ref, ym_ref, yd_ref, x_ref, wo_ref, g1_ref, b1_ref, wu_ref, wd_ref,
                    g2_ref, b2_ref, out_ref, x1_ref, xb_ref, acc_ref):
    f = pl.program_id(1)
    last = pl.num_programs(1) - 1
    wc = yc_ref.shape[1]
    wm = ym_ref.shape[1]

    def step(first, final):
        for r in range(0, x_ref.shape[0], MLP_SUB):
            rows = pl.ds(r, MLP_SUB)
            if first:
                h = jnp.dot(yc_ref[rows, :], wo_ref[0:wc, :], preferred_element_type=F32)
                h = h + jnp.dot(ym_ref[rows, :], wo_ref[wc:wc + wm, :],
                                preferred_element_type=F32)
                h = h + jnp.dot(yd_ref[rows, :], wo_ref[wc + wm:, :],
                                preferred_element_type=F32)
                x1 = _layer_norm_rows(DEEPNORM_ALPHA * x_ref[rows, :] + h,
                                      g1_ref[...], b1_ref[...])
                x1_ref[rows, :] = x1
                xb = x1.astype(BF16)
                xb_ref[rows, :] = xb
            else:
                xb = xb_ref[rows, :]
            up = jnp.maximum(jnp.dot(xb, wu_ref[...], preferred_element_type=F32), 0.0)
            down = jnp.dot((up * up).astype(BF16), wd_ref[...], preferred_element_type=F32)
            acc = down if first else acc_ref[rows, :] + down
            if final:
                out_ref[rows, :] = _layer_norm_rows(DEEPNORM_ALPHA * x1_ref[rows, :] + acc,
                                                    g2_ref[...], b2_ref[...])
            else:
                acc_ref[rows, :] = acc

    @pl.when(f == 0)
    def _():
        step(True, False)

    @pl.when(jnp.logical_and(f > 0, f < last))
    def _():
        step(False, False)

    @pl.when(f == last)
    def _():
        step(False, True)


def _out_mlp(yc, ym, yd, x2, wo, g1, b1, wu, wd, g2, b2):
    n, d = x2.shape
    dff = wu.shape[1]
    tm, tf = MLP_ROWS, MLP_FF
    assert dff // tf >= 2, "the first and last hidden steps are distinct code paths"
    rows = lambda i, f: (i, 0)
    const = lambda i, f: (0, 0)
    return pl.pallas_call(
        _out_mlp_kernel,
        grid=(n // tm, dff // tf),
        in_specs=[pl.BlockSpec((tm, yc.shape[1]), rows),
                  pl.BlockSpec((tm, ym.shape[1]), rows),
                  pl.BlockSpec((tm, yd.shape[1]), rows),
                  pl.BlockSpec((tm, d), rows),
                  pl.BlockSpec(wo.shape, const),
                  pl.BlockSpec(g1.shape, const),
                  pl.BlockSpec(b1.shape, const),
                  pl.BlockSpec((d, tf), lambda i, f: (0, f)),
                  pl.BlockSpec((tf, d), lambda i, f: (f, 0)),
                  pl.BlockSpec(g2.shape, const),
                  pl.BlockSpec(b2.shape, const)],
        out_specs=pl.BlockSpec((tm, d), rows),
        out_shape=jax.ShapeDtypeStruct((n, d), F32),
        scratch_shapes=[pltpu.VMEM((tm, d), F32), pltpu.VMEM((tm, d), BF16),
                        pltpu.VMEM((tm, d), F32)],
        compiler_params=_params("parallel", "arbitrary"),
        name="out_mlp_ln",
    )(yc, ym, yd, x2, wo, g1, b1, wu, wd, g2, b2)


def _row(v):
    return v.reshape(1, -1).astype(F32)


def _pad_rows(w, rows):
    return jnp.pad(w.astype(F32), ((0, rows - w.shape[0]), (0, 0)))


def _rearranged_w_in(w):
    d = w.shape[0]
    gate_lo = 256 + 256 + 512 + 256 + 256
    gate_hi = gate_lo + 2 * ML_HEADS
    v_lo = gate_hi + 2 * DA_HEADS * 2 * DA_QK_DIM
    pad = jnp.zeros((d, LANES - 2 * ML_HEADS), w.dtype)
    w_r = jnp.concatenate([w[:, :gate_lo], w[:, gate_hi:v_lo], w[:, gate_lo:gate_hi], pad], axis=1)
    return w_r.astype(BF16), w[:, v_lo:].T.astype(BF16)


def kernel(x, w_in, b_igate, b_fgate, conv_dw_w, conv_dw_b, conv_ln_g, conv_ln_b, conv_pw_w, conv_pw_b, ml_conv_w, ml_conv_b, ml_norm_g, lam_q1, lam_k1, lam_q2, lam_k2, da_norm_g, w_out, ln1_g, ln1_b, w_up, w_down, ln2_g, ln2_b):
    bsz, s, d = x.shape
    n = bsz * s
    x2 = x.reshape(n, d)
    for l in range(DEPTH):
        lambda_init = 0.8 - 0.6 * math.exp(-0.3 * l)
        w_r, w_vt = _rearranged_w_in(w_in[l])
        glu, mqk, mv, mo, gates, dq, dk, dvt = _in_proj(x2, w_r, w_vt, bsz)
        seq = lambda a: a.reshape(bsz, s, a.shape[1])

        y_conv = _conv_module(seq(glu), _pad_rows(conv_dw_w[l], 32), _row(conv_dw_b[l]),
                              _row(conv_ln_g[l]), _row(conv_ln_b[l]),
                              conv_pw_w[l].astype(BF16), _row(conv_pw_b[l]))

        gate_bias = jnp.pad(jnp.concatenate([b_igate[l], b_fgate[l]]).astype(F32),
                            (0, LANES - 2 * ML_HEADS)).reshape(1, LANES)
        y_ml = _mlstm(seq(mqk), seq(mv), seq(mo), seq(gates),
                      _pad_rows(ml_conv_w[l], SUBLANES), _row(ml_conv_b[l]),
                      gate_bias, _row(ml_norm_g[l]))

        lam_p = jnp.stack([lam_q1[l], lam_k1[l], lam_q2[l], lam_k2[l]]).astype(F32)
        y_da = _diff_attention(seq(dq), seq(dk), dvt, lam_p,
                               da_norm_g[l].astype(F32).reshape(-1, 1), lambda_init)

        x2 = _out_mlp(y_conv.reshape(n, -1), y_ml.reshape(n, -1), y_da.reshape(n, -1), x2,
                      w_out[l].astype(BF16), _row(ln1_g[l]), _row(ln1_b[l]),
                      w_up[l].astype(BF16), w_down[l].astype(BF16),
                      _row(ln2_g[l]), _row(ln2_b[l]))
    return x2.reshape(bsz, s, d)
```

```python
import functools
import math

import jax
import jax.numpy as jnp
from jax import lax
from jax.experimental import pallas as pl
from jax.experimental.pallas import tpu as pltpu

F32 = jnp.float32
BF16 = jnp.bfloat16

LANES = 128
SUBLANES = 8
VMEM_LIMIT_BYTES = 56 * 1024 * 1024

DEPTH = 2
CONV_KSIZE = 31
ML_HEADS = 4
ML_HEAD_DIM = 64
ML_QK_CONV = 4
ML_CHUNK = 128
DA_HEADS = 4
DA_QK_DIM = 64
DA_V_DIM = 128
LN_EPS = 1e-5
DEEPNORM_ALPHA = (2 * DEPTH) ** 0.25

PROJ_ROWS = 512
CONV_ROWS = 512
CONV_HALO = 32
CONV_SUB = 64
ATT_Q = 512
ATT_K = PROJ_ROWS
ML_BATCH = 4
ATT_SUM_ROWS = 16
MLP_ROWS = 1024
MLP_FF = 1024
MLP_SUB = 256


def _params(*semantics):
    return pltpu.CompilerParams(dimension_semantics=semantics,
                                vmem_limit_bytes=VMEM_LIMIT_BYTES)


def _sigmoid(x):
    return 1.0 / (1.0 + jnp.exp(-x))


def _layer_norm_rows(x, g, b):
    mu = jnp.mean(x, axis=-1, keepdims=True)
    xc = x - mu
    var = jnp.mean(xc * xc, axis=-1, keepdims=True)
    return xc * lax.rsqrt(var + LN_EPS) * g + b


def _in_proj_kernel(x_ref, w_ref, wvt_ref, glu_ref, mqk_ref, mv_ref, mo_ref, gate_ref,
                    dq_ref, dk_ref, dvt_ref):
    xb = x_ref[...].astype(BF16)

    def sec(lo, width):
        return jnp.dot(xb, w_ref[:, lo:lo + width], preferred_element_type=F32)

    glu_ref[...] = sec(0, 256) * _sigmoid(sec(256, 256))
    mqk_ref[...] = sec(512, 512)
    mv_ref[...] = sec(1024, 256)
    mo_ref[...] = sec(1280, 256)
    dq_ref[...] = (sec(1536, 512) * (DA_QK_DIM ** -0.5 * math.log2(math.e))).astype(BF16)
    dk_ref[...] = sec(2048, 512).astype(BF16)
    gate_ref[...] = sec(2560, LANES)
    dvt_ref[0, 0] = lax.dot_general(wvt_ref[...], xb, (((1,), (1,)), ((), ())),
                                    preferred_element_type=F32).astype(BF16)


def _in_proj(x2, w_r, w_vt, bsz):
    n, d = x2.shape
    tm = PROJ_ROWS
    tiles_per_seq = n // bsz // tm
    widths = (256, 512, 256, 256, LANES, 512, 512)
    dtypes = (F32, F32, F32, F32, F32, BF16, BF16)
    wv = w_vt.shape[0]
    return pl.pallas_call(
        _in_proj_kernel,
        grid=(n // tm,),
        in_specs=[pl.BlockSpec((tm, d), lambda i: (i, 0)),
                  pl.BlockSpec(w_r.shape, lambda i: (0, 0)),
                  pl.BlockSpec(w_vt.shape, lambda i: (0, 0))],
        out_specs=[pl.BlockSpec((tm, w), lambda i: (i, 0)) for w in widths]
        + [pl.BlockSpec((1, 1, wv, tm),
                        lambda i: (i // tiles_per_seq, i % tiles_per_seq, 0, 0))],
        out_shape=[jax.ShapeDtypeStruct((n, w), dt) for w, dt in zip(widths, dtypes)]
        + [jax.ShapeDtypeStruct((bsz, tiles_per_seq, wv, tm), BF16)],
        compiler_params=_params("parallel"),
        name="in_proj",
    )(x2, w_r, w_vt)


def _conv_kernel(x_ref, halo_ref, dw_ref, dwb_ref, g_ref, b_ref, pw_ref, pwb_ref,
                 out_ref, xs, ybuf):
    s = pl.program_id(1)
    rows = x_ref.shape[1]
    xs[0, 0:CONV_HALO, :] = jnp.where(s > 0, halo_ref[0], 0.0)
    xs[0, CONV_HALO:CONV_HALO + rows, :] = x_ref[0]
    span = CONV_HALO + rows - SUBLANES
    for k in range(1, SUBLANES):
        xs[k, 0:span, :] = xs[0, k:k + span, :]
    first = CONV_HALO - (CONV_KSIZE - 1)
    for r in range(0, rows, CONV_SUB):
        acc = jnp.broadcast_to(dwb_ref[...], (CONV_SUB, x_ref.shape[2]))
        for j in range(CONV_KSIZE):
            k = (first + j) % SUBLANES
            lo = r + first + j - k
            acc = acc + dw_ref[j:j + 1, :] * xs[k, lo:lo + CONV_SUB, :]
        y = _layer_norm_rows(acc, g_ref[...], b_ref[...])
        ybuf[r:r + CONV_SUB, :] = (y * _sigmoid(y)).astype(BF16)
    out = jnp.dot(ybuf[...], pw_ref[...], preferred_element_type=F32) + pwb_ref[...]
    out_ref[0] = out.astype(BF16)


def _conv_module(glu, dw, dwb, g, b, pw, pwb):
    bsz, s, ch = glu.shape
    ts = CONV_ROWS
    halo_per_tile = ts // CONV_HALO
    return pl.pallas_call(
        _conv_kernel,
        grid=(bsz, s // ts),
        in_specs=[
            pl.BlockSpec((1, ts, ch), lambda bi, si: (bi, si, 0)),
            pl.BlockSpec((1, CONV_HALO, ch),
                         lambda bi, si: (bi, jnp.maximum(si * halo_per_tile - 1, 0), 0)),
            pl.BlockSpec(dw.shape, lambda bi, si: (0, 0)),
            pl.BlockSpec(dwb.shape, lambda bi, si: (0, 0)),
            pl.BlockSpec(g.shape, lambda bi, si: (0, 0)),
            pl.BlockSpec(b.shape, lambda bi, si: (0, 0)),
            pl.BlockSpec(pw.shape, lambda bi, si: (0, 0)),
            pl.BlockSpec(pwb.shape, lambda bi, si: (0, 0)),
        ],
        out_specs=pl.BlockSpec((1, ts, ch), lambda bi, si: (bi, si, 0)),
        out_shape=jax.ShapeDtypeStruct((bsz, s, ch), BF16),
        scratch_shapes=[pltpu.VMEM((SUBLANES, CONV_HALO + ts, ch), F32),
                        pltpu.VMEM((ts, ch), BF16)],
        compiler_params=_params("parallel", "parallel"),
        name="conv_module",
    )(glu, glu, dw, dwb, g, b, pw, pwb)


def _mlstm_kernel(qk_ref, halo_ref, v_ref, o_ref, gate_ref, cw_ref, cb_ref, gb_ref,
                  ng_ref, sel_ref, out_ref, qpad, ct_ref, m_ref):
    c = pl.program_id(1)
    nb = qk_ref.shape[0]
    t = ML_CHUNK
    w_pair = 2 * ML_HEAD_DIM
    k_off = ML_HEADS * ML_HEAD_DIM
    n_pairs = ML_HEADS // 2
    nt = (((1,), (1,)), ((), ()))
    tn = (((0,), (0,)), ((), ()))

    @pl.when(c == 0)
    def _():
        ct_ref[...] = jnp.zeros_like(ct_ref)
        m_ref[...] = jnp.zeros_like(m_ref)

    row = lax.broadcasted_iota(jnp.int32, (t, t), 0)
    col = lax.broadcasted_iota(jnp.int32, (t, t), 1)
    causal = row >= col
    tri = causal.astype(F32)
    lane = lax.broadcasted_iota(jnp.int32, (t, w_pair), 1)
    low = lane < ML_HEAD_DIM
    owns = (low, lane >= ML_HEAD_DIM)

    first = SUBLANES - (ML_QK_CONV - 1)
    qk, gpre_t, log_f_t = [], [], []
    for bb in range(nb):
        qpad[bb, 0:SUBLANES, :] = jnp.where(c > 0, halo_ref[bb], 0.0)
        qpad[bb, SUBLANES:SUBLANES + t, :] = qk_ref[bb]
        acc = jnp.broadcast_to(cb_ref[...], (t, qk_ref.shape[2]))
        for j in range(ML_QK_CONV):
            acc = acc + cw_ref[j:j + 1, :] * qpad[bb, first + j:first + j + t, :]
        qk.append(acc * _sigmoid(acc))
        g_t = (gate_ref[bb] + gb_ref[...]).T[0:SUBLANES, :]
        gpre_t.append(g_t)
        log_f_t.append(jnp.minimum(g_t, 0.0) - jnp.log(1.0 + jnp.exp(-jnp.abs(g_t))))

    tri_t = (row <= col).astype(F32)
    head_row = lax.broadcasted_iota(jnp.int32, (SUBLANES, t), 0) < ML_HEADS
    lane_t = lax.broadcasted_iota(jnp.int32, (SUBLANES, t), 1)
    rb_t, cm_t = [], []
    for bb in range(nb):
        b_t = jnp.dot(log_f_t[bb], tri_t, preferred_element_type=F32,
                      precision=lax.Precision.HIGHEST)
        r_t = gpre_t[bb] - pltpu.roll(b_t, ML_HEADS, 0)
        cm = r_t
        shift = 1
        while shift < t:
            cm = jnp.where(lane_t >= shift, jnp.maximum(cm, pltpu.roll(cm, shift, 1)), cm)
            shift *= 2
        rb_t.append(jnp.where(head_row, r_t, b_t))
        cm_t.append(cm)
    rep = {}
    for bb in range(nb):
        parts = []
        for tile in (rb_t[bb], cm_t[bb]):
            hi = tile.astype(BF16)
            parts += [hi, (tile - hi.astype(F32)).astype(BF16)]
        g_hl = jnp.concatenate(parts, axis=0)
        for h in range(ML_HEADS):
            rep[bb, h] = lax.dot_general(g_hl, sel_ref[h], tn, preferred_element_type=F32)

    items = [(bb, h) for bb in range(nb) for h in range(ML_HEADS)]
    st = {}

    def tile_rows(x_row):
        return jnp.tile(jnp.broadcast_to(x_row, (SUBLANES, LANES)), (t // SUBLANES, 1))

    for bb in range(nb):
        for pair in range(n_pairs):
            lo = pair * w_pair
            k_pair = qk[bb][:, k_off + lo:k_off + lo + w_pair] * (ML_HEAD_DIM ** -0.5)
            st[bb, pair, "q"] = qk[bb][:, lo:lo + w_pair]
            st[bb, pair, "k"] = k_pair
            st[bb, pair, "kbt"] = k_pair.T.astype(BF16)
            st[bb, pair, "v"] = v_ref[bb, :, lo:lo + w_pair]
    for (bb, h) in items:
        pair, par = divmod(h, 2)
        own = owns[par]
        b_col = rep[bb, h][:, 0:LANES]
        r_col = rep[bb, h][:, LANES:2 * LANES]
        cm_col = rep[bb, h][:, 2 * LANES:3 * LANES]
        r_row = tile_rows(rb_t[bb][h:h + 1, :])
        m_prev8 = m_ref[bb, h]
        m_prev = jnp.tile(m_prev8, (t // SUBLANES, 1))
        u_col = jnp.maximum(m_prev, cm_col)
        st[bb, h, "a_inter"] = jnp.exp(m_prev - u_col)
        st[bb, h, "dexp"] = jnp.exp(jnp.where(causal, r_row - u_col, -jnp.inf))
        st[bb, h, "floor"] = jnp.exp(-(b_col + u_col))
        st[bb, h, "q_m"] = jnp.where(own, st[bb, pair, "q"], 0.0).astype(BF16)
        st[bb, h, "vx"] = jnp.where(own, st[bb, pair, "v"], 1.0).astype(BF16)
        g_end8 = jnp.broadcast_to(b_col[t - 1:t, :], (SUBLANES, LANES))
        w_end = jnp.tile(g_end8, (t // SUBLANES, 1)) + r_col
        m_loc8 = jnp.broadcast_to(jnp.max(w_end, axis=0, keepdims=True), (SUBLANES, LANES))
        p_end = jnp.exp(w_end - jnp.tile(m_loc8, (t // SUBLANES, 1)))
        m_new8 = jnp.maximum(g_end8 + m_prev8, m_loc8)
        st[bb, h, "a_st"] = jnp.tile(jnp.exp(g_end8 + m_prev8 - m_new8), (t // SUBLANES, 1))
        st[bb, h, "e_st"] = jnp.tile(jnp.exp(m_loc8 - m_new8), (t // SUBLANES, 1))
        st[bb, h, "m_new"] = m_new8
        st[bb, h, "kp"] = (jnp.where(own, st[bb, pair, "k"], 0.0) * p_end).astype(BF16)

    for (bb, h) in items:
        st[bb, h, "s_qk"] = jnp.dot(st[bb, h, "q_m"], st[bb, h // 2, "kbt"],
                                    preferred_element_type=F32)
    for (bb, h) in items:
        ct_prev = ct_ref[bb, h]
        st[bb, h, "inter"] = jnp.dot(st[bb, h, "q_m"], ct_prev.astype(BF16),
                                     preferred_element_type=F32)
        c_loc = lax.dot_general(st[bb, h, "kp"], st[bb, h, "vx"], tn,
                                preferred_element_type=F32)
        ct_ref[bb, h] = st[bb, h, "a_st"] * ct_prev + st[bb, h, "e_st"] * c_loc
        m_ref[bb, h] = st[bb, h, "m_new"]

    for (bb, h) in items:
        sc = (st[bb, h, "s_qk"] * st[bb, h, "dexp"]).astype(BF16)
        st[bb, h, "intra"] = jnp.dot(sc, st[bb, h, "vx"], preferred_element_type=F32)
    for (bb, h) in items:
        numden = st[bb, h, "a_inter"] * st[bb, h, "inter"] + st[bb, h, "intra"]
        den = pltpu.roll(numden, ML_HEAD_DIM, 1)
        st[bb, h, "hh"] = numden / jnp.maximum(jnp.abs(den), st[bb, h, "floor"])

    inv = 1.0 / ML_HEAD_DIM
    for bb in range(nb):
        for pair in range(n_pairs):
            lo = pair * w_pair
            h_pair = jnp.where(low, st[bb, 2 * pair, "hh"], st[bb, 2 * pair + 1, "hh"])
            x = h_pair * _sigmoid(o_ref[bb, :, lo:lo + w_pair])
            s_low = jnp.sum(jnp.where(low, x, 0.0), axis=-1, keepdims=True)
            s_all = jnp.sum(x, axis=-1, keepdims=True)
            xc = x - jnp.where(low, s_low, s_all - s_low) * inv
            sq = xc * xc
            v_low = jnp.sum(jnp.where(low, sq, 0.0), axis=-1, keepdims=True)
            v_all = jnp.sum(sq, axis=-1, keepdims=True)
            var = jnp.where(low, v_low, v_all - v_low) * inv
            y = xc * lax.rsqrt(var + LN_EPS) * ng_ref[:, lo:lo + w_pair]
            out_ref[bb, :, lo:lo + w_pair] = y.astype(BF16)


def _mlstm(mqk, mv, mo, gates, cw, cb, gb, ng):
    bsz, s, wqk = mqk.shape
    wv = mv.shape[2]
    t = ML_CHUNK
    nb = math.gcd(bsz, ML_BATCH)
    halo_per_chunk = t // SUBLANES
    src = jnp.arange(4 * SUBLANES)[:, None]
    dst = jnp.arange(3 * LANES)[None, :] // LANES
    sel = jnp.stack([((dst == 0) & (src // (2 * SUBLANES) == 0) & (src % SUBLANES == ML_HEADS + h)
                      | (dst == 1) & (src // (2 * SUBLANES) == 0) & (src % SUBLANES == h)
                      | (dst == 2) & (src // (2 * SUBLANES) == 1) & (src % SUBLANES == h)
                      ).astype(BF16) for h in range(ML_HEADS)])
    const = lambda bi, ci: (0, 0)
    return pl.pallas_call(
        _mlstm_kernel,
        grid=(bsz // nb, s // t),
        in_specs=[
            pl.BlockSpec((nb, t, wqk), lambda bi, ci: (bi, ci, 0)),
            pl.BlockSpec((nb, SUBLANES, wqk),
                         lambda bi, ci: (bi, jnp.maximum(ci * halo_per_chunk - 1, 0), 0)),
            pl.BlockSpec((nb, t, wv), lambda bi, ci: (bi, ci, 0)),
            pl.BlockSpec((nb, t, wv), lambda bi, ci: (bi, ci, 0)),
            pl.BlockSpec((nb, t, LANES), lambda bi, ci: (bi, ci, 0)),
            pl.BlockSpec(cw.shape, const),
            pl.BlockSpec(cb.shape, const),
            pl.BlockSpec(gb.shape, const),
            pl.BlockSpec(ng.shape, const),
            pl.BlockSpec(sel.shape, lambda bi, ci: (0, 0, 0)),
        ],
        out_specs=pl.BlockSpec((nb, t, wv), lambda bi, ci: (bi, ci, 0)),
        out_shape=jax.ShapeDtypeStruct((bsz, s, wv), BF16),
        scratch_shapes=[pltpu.VMEM((nb, SUBLANES + t, wqk), F32),
                        pltpu.VMEM((nb, ML_HEADS, 2 * ML_HEAD_DIM, LANES), F32),
                        pltpu.VMEM((nb, ML_HEADS, SUBLANES, LANES), F32)],
        compiler_params=_params("parallel", "arbitrary"),
        name="mlstm",
    )(mqk, mqk, mv, mo, gates, cw, cb, gb, ng, sel)


def _attn_kernel(q_ref, qn_ref, k_ref, vt_ref, lam_ref, ng_ref, out_ref, qm_ref, acc_ref,
                 s0_ref, s1_ref, mblk_ref, m_ref, *, lambda_init):
    i = pl.program_id(2)
    tq = q_ref.shape[1]
    tk = vt_ref.shape[3]
    lane = lax.broadcasted_iota(jnp.int32, (tq, 2 * DA_QK_DIM), 1)

    def map_operands(q):
        zero = jnp.zeros_like(q)
        return jnp.where(lane < DA_QK_DIM, q, zero), jnp.where(lane >= DA_QK_DIM, q, zero)

    qm_ref[0], qm_ref[1] = map_operands(q_ref[0])
    acc_ref[...] = jnp.zeros_like(acc_ref)
    m_ref[...] = jnp.full_like(m_ref, -jnp.inf)

    ones_rows = jnp.ones((ATT_SUM_ROWS, tk), BF16)

    def scores(j, s_ref, slot, qms=None):
        k_blk = k_ref[0, pl.ds(pl.multiple_of(j * tk, tk), tk), :]
        for mp in range(2):
            qm = qm_ref[mp] if qms is None else qms[mp]
            st = lax.dot_general(k_blk, qm, (((1,), (1,)), ((), ())),
                                 preferred_element_type=F32)
            s_ref[mp] = st
            mblk_ref[slot, mp] = jnp.max(st, axis=0, keepdims=True)

    def softmax_pv(j, s_ref, slot, masked):
        vt_blk = jnp.concatenate([vt_ref[0, j], ones_rows], axis=0)
        if masked:
            key = lax.broadcasted_iota(jnp.int32, (tk, tq), 0)
            qry = lax.broadcasted_iota(jnp.int32, (tk, tq), 1)
            keep = key <= qry
        for mp in range(2):
            st = s_ref[mp]
            if masked:
                st = jnp.where(keep, st, -jnp.inf)
                m_blk = jnp.max(st, axis=0, keepdims=True)
            else:
                m_blk = mblk_ref[slot, mp]
            m_old = m_ref[mp]
            m_new = jnp.maximum(m_old, m_blk)
            alpha = jnp.exp2(m_old - m_new)
            pexp = jnp.exp2(st - m_new).astype(BF16)
            acc_ref[mp] = alpha * acc_ref[mp] + jnp.dot(vt_blk, pexp,
                                                       preferred_element_type=F32)
            m_ref[mp] = m_new

    def run(sa_ref, a, sb_ref, b):
        @pl.when(i == 0)
        def _():
            scores(0, sa_ref, a)

        def pair(jj, carry):
            j0 = 2 * jj
            scores(j0 + 1, sb_ref, b)
            softmax_pv(j0, sa_ref, a, False)
            scores(j0 + 2, sa_ref, a)
            softmax_pv(j0 + 1, sb_ref, b, False)
            return carry

        lax.fori_loop(0, i // 2, pair, 0)

        @pl.when(i % 2 == 1)
        def _():
            scores(i, sb_ref, b)
            softmax_pv(i - 1, sa_ref, a, False)
            scores(0, sa_ref, a, map_operands(qn_ref[0]))
            softmax_pv(i, sb_ref, b, True)

        @pl.when(i % 2 == 0)
        def _():
            scores(0, sb_ref, b, map_operands(qn_ref[0]))
            softmax_pv(i, sa_ref, a, True)

    start = ((i + 1) // 2) % 2

    @pl.when(start == 0)
    def _():
        run(s0_ref, 0, s1_ref, 1)

    @pl.when(start == 1)
    def _():
        run(s1_ref, 1, s0_ref, 0)

    lp = lam_ref[...]
    lam = (jnp.exp(jnp.sum(lp[0:1, :] * lp[1:2, :], axis=-1, keepdims=True))
           - jnp.exp(jnp.sum(lp[2:3, :] * lp[3:4, :], axis=-1, keepdims=True))
           + lambda_init)
    dv = DA_V_DIM
    o_t = (acc_ref[0, 0:dv, :] / acc_ref[0, dv:dv + 1, :]
           - lam * (acc_ref[1, 0:dv, :] / acc_ref[1, dv:dv + 1, :]))
    ms = jnp.mean(o_t * o_t, axis=0, keepdims=True)
    y_t = o_t * lax.rsqrt(ms + LN_EPS) * (ng_ref[...] * (1.0 - lambda_init))
    out_ref[0] = y_t.T.astype(BF16)


def _diff_attention(dq, dk, dvt, lam_p, ng_col, lambda_init):
    bsz, s, _ = dq.shape
    tq = ATT_Q
    nk, tk = dvt.shape[1], dvt.shape[3]
    assert tq == tk and nk * tk == s
    nq = s // tq
    return pl.pallas_call(
        functools.partial(_attn_kernel, lambda_init=lambda_init),
        grid=(bsz, DA_HEADS, nq),
        in_specs=[
            pl.BlockSpec((1, tq, 2 * DA_QK_DIM), lambda b, h, i: (b, i, h)),
            pl.BlockSpec((1, tq, 2 * DA_QK_DIM),
                         lambda b, h, i: (b, jnp.minimum(i + 1, nq - 1), h)),
            pl.BlockSpec((1, s, 2 * DA_QK_DIM), lambda b, h, i: (b, 0, h)),
            pl.BlockSpec((1, nk, DA_V_DIM, tk), lambda b, h, i: (b, 0, h, 0)),
            pl.BlockSpec(lam_p.shape, lambda b, h, i: (0, 0)),
            pl.BlockSpec(ng_col.shape, lambda b, h, i: (0, 0)),
        ],
        out_specs=pl.BlockSpec((1, tq, DA_V_DIM), lambda b, h, i: (b, i, h)),
        out_shape=jax.ShapeDtypeStruct((bsz, s, DA_HEADS * DA_V_DIM), BF16),
        scratch_shapes=[pltpu.VMEM((2, tq, 2 * DA_QK_DIM), BF16),
                        pltpu.VMEM((2, DA_V_DIM + ATT_SUM_ROWS, tq), F32),
                        pltpu.VMEM((2, tk, tq), F32),
                        pltpu.VMEM((2, tk, tq), F32),
                        pltpu.VMEM((2, 2, 1, tq), F32),
                        pltpu.VMEM((2, 1, tq), F32)],
        compiler_params=_params("parallel", "parallel", "arbitrary"),
        name="diff_attention",
    )(dq, dq, dk, dvt, lam_p, ng_col)


def _out_mlp_kernel(yc_ref, ym_ref, yd_ref, x_ref, wo_ref, g1_ref, b1_ref, wu_ref, wd_ref,
                    g2_ref, b2_ref, out_ref, x1_ref, xb_ref, acc_ref):
    f = pl.program_id(1)
    last = pl.num_programs(1) - 1
    wc = yc_ref.shape[1]
    wm = ym_ref.shape[1]

    def step(first, final):
        for r in range(0, x_ref.shape[0], MLP_SUB):
            rows = pl.ds(r, MLP_SUB)
            if first:
                h = jnp.dot(yc_ref[rows, :], wo_ref[0:wc, :], preferred_element_type=F32)
                h = h + jnp.dot(ym_ref[rows, :], wo_ref[wc:wc + wm, :],
                                preferred_element_type=F32)
                h = h + jnp.dot(yd_ref[rows, :], wo_ref[wc + wm:, :],
                                preferred_element_type=F32)
                x1 = _layer_norm_rows(DEEPNORM_ALPHA * x_ref[rows, :] + h,
                                      g1_ref[...], b1_ref[...])
                x1_ref[rows, :] = x1
                xb = x1.astype(BF16)
                xb_ref[rows, :] = xb
            else:
                xb = xb_ref[rows, :]
            up = jnp.maximum(jnp.dot(xb, wu_ref[...], preferred_element_type=F32), 0.0)
            down = jnp.dot((up * up).astype(BF16), wd_ref[...], preferred_element_type=F32)
            acc = down if first else acc_ref[rows, :] + down
            if final:
                out_ref[rows, :] = _layer_norm_rows(DEEPNORM_ALPHA * x1_ref[rows, :] + acc,
                                                    g2_ref[...], b2_ref[...])
            else:
                acc_ref[rows, :] = acc

    @pl.when(f == 0)
    def _():
        step(True, False)

    @pl.when(jnp.logical_and(f > 0, f < last))
    def _():
        step(False, False)

    @pl.when(f == last)
    def _():
        step(False, True)


def _out_mlp(yc, ym, yd, x2, wo, g1, b1, wu, wd, g2, b2):
    n, d = x2.shape
    dff = wu.shape[1]
    tm, tf = MLP_ROWS, MLP_FF
    assert dff // tf >= 2, "the first and last hidden steps are distinct code paths"
    rows = lambda i, f: (i, 0)
    const = lambda i, f: (0, 0)
    return pl.pallas_call(
        _out_mlp_kernel,
        grid=(n // tm, dff // tf),
        in_specs=[pl.BlockSpec((tm, yc.shape[1]), rows),
                  pl.BlockSpec((tm, ym.shape[1]), rows),
                  pl.BlockSpec((tm, yd.shape[1]), rows),
                  pl.BlockSpec((tm, d), rows),
                  pl.BlockSpec(wo.shape, const),
                  pl.BlockSpec(g1.shape, const),
                  pl.BlockSpec(b1.shape, const),
                  pl.BlockSpec((d, tf), lambda i, f: (0, f)),
                  pl.BlockSpec((tf, d), lambda i, f: (f, 0)),
                  pl.BlockSpec(g2.shape, const),
                  pl.BlockSpec(b2.shape, const)],
        out_specs=pl.BlockSpec((tm, d), rows),
        out_shape=jax.ShapeDtypeStruct((n, d), F32),
        scratch_shapes=[pltpu.VMEM((tm, d), F32), pltpu.VMEM((tm, d), BF16),
                        pltpu.VMEM((tm, d), F32)],
        compiler_params=_params("parallel", "arbitrary"),
        name="out_mlp_ln",
    )(yc, ym, yd, x2, wo, g1, b1, wu, wd, g2, b2)


def _row(v):
    return v.reshape(1, -1).astype(F32)


def _pad_rows(w, rows):
    return jnp.pad(w.astype(F32), ((0, rows - w.shape[0]), (0, 0)))


def _rearranged_w_in(w):
    d = w.shape[0]
    gate_lo = 256 + 256 + 512 + 256 + 256
    gate_hi = gate_lo + 2 * ML_HEADS
    v_lo = gate_hi + 2 * DA_HEADS * 2 * DA_QK_DIM
    pad = jnp.zeros((d, LANES - 2 * ML_HEADS), w.dtype)
    w_r = jnp.concatenate([w[:, :gate_lo], w[:, gate_hi:v_lo], w[:, gate_lo:gate_hi], pad], axis=1)
    return w_r.astype(BF16), w[:, v_lo:].T.astype(BF16)


def kernel(x, w_in, b_igate, b_fgate, conv_dw_w, conv_dw_b, conv_ln_g, conv_ln_b, conv_pw_w, conv_pw_b, ml_conv_w, ml_conv_b, ml_norm_g, lam_q1, lam_k1, lam_q2, lam_k2, da_norm_g, w_out, ln1_g, ln1_b, w_up, w_down, ln2_g, ln2_b):
    bsz, s, d = x.shape
    n = bsz * s
    x2 = x.reshape(n, d)
    for l in range(DEPTH):
        lambda_init = 0.8 - 0.6 * math.exp(-0.3 * l)
        w_r, w_vt = _rearranged_w_in(w_in[l])
        glu, mqk, mv, mo, gates, dq, dk, dvt = _in_proj(x2, w_r, w_vt, bsz)
        seq = lambda a: a.reshape(bsz, s, a.shape[1])

        y_conv = _conv_module(seq(glu), _pad_rows(conv_dw_w[l], 32), _row(conv_dw_b[l]),
                              _row(conv_ln_g[l]), _row(conv_ln_b[l]),
                              conv_pw_w[l].astype(BF16), _row(conv_pw_b[l]))

        gate_bias = jnp.pad(jnp.concatenate([b_igate[l], b_fgate[l]]).astype(F32),
                            (0, LANES - 2 * ML_HEADS)).reshape(1, LANES)
        y_ml = _mlstm(seq(mqk), seq(mv), seq(mo), seq(gates),
                      _pad_rows(ml_conv_w[l], SUBLANES), _row(ml_conv_b[l]),
                      gate_bias, _row(ml_norm_g[l]))

        lam_p = jnp.stack([lam_q1[l], lam_k1[l], lam_q2[l], lam_k2[l]]).astype(F32)
        y_da = _diff_attention(seq(dq), seq(dk), dvt, lam_p,
                               da_norm_g[l].astype(F32).reshape(-1, 1), lambda_init)

        x2 = _out_mlp(y_conv.reshape(n, -1), y_ml.reshape(n, -1), y_da.reshape(n, -1), x2,
                      w_out[l].astype(BF16), _row(ln1_g[l]), _row(ln1_b[l]),
                      w_up[l].astype(BF16), w_down[l].astype(BF16),
                      _row(ln2_g[l]), _row(ln2_b[l]))
    return x2.reshape(bsz, s, d)
```

```python
import functools
import math

import jax
import jax.numpy as jnp
from jax import lax
from jax.experimental import pallas as pl
from jax.experimental.pallas import tpu as pltpu

F32 = jnp.float32
BF16 = jnp.bfloat16

LANES = 128
SUBLANES = 8
VMEM_LIMIT_BYTES = 56 * 1024 * 1024

DEPTH = 2
CONV_KSIZE = 31
ML_HEADS = 4
ML_HEAD_DIM = 64
ML_QK_CONV = 4
ML_CHUNK = 128
DA_HEADS = 4
DA_QK_DIM = 64
DA_V_DIM = 128
LN_EPS = 1e-5
DEEPNORM_ALPHA = (2 * DEPTH) ** 0.25

PROJ_ROWS = 1024
CONV_ROWS = 1024
CONV_HALO = 32
CONV_SUB = 64
ATT_Q = 512
ATT_K = 512
ML_BATCH = 4
ATT_SUM_ROWS = 16
MLP_ROWS = 1024
MLP_FF = 1024
MLP_SUB = 256


def _params(*semantics):
    return pltpu.CompilerParams(dimension_semantics=semantics,
                                vmem_limit_bytes=VMEM_LIMIT_BYTES)


def _sigmoid(x):
    return 1.0 / (1.0 + jnp.exp(-x))


def _layer_norm_rows(x, g, b):
    mu = jnp.mean(x, axis=-1, keepdims=True)
    xc = x - mu
    var = jnp.mean(xc * xc, axis=-1, keepdims=True)
    return xc * lax.rsqrt(var + LN_EPS) * g + b


def _in_proj_kernel(x_ref, w_ref, wvt_ref, glu_ref, mqk_ref, mv_ref, mo_ref, gate_ref,
                    dq_ref, dk_ref, dvt_ref):
    xb = x_ref[...].astype(BF16)

    def sec(lo, width):
        return jnp.dot(xb, w_ref[:, lo:lo + width], preferred_element_type=F32)

    glu_ref[...] = sec(0, 256) * _sigmoid(sec(256, 256))
    mqk_ref[...] = sec(512, 512)
    mv_ref[...] = sec(1024, 256)
    mo_ref[...] = sec(1280, 256)
    dq_ref[...] = (sec(1536, 512) * (DA_QK_DIM ** -0.5 * math.log2(math.e))).astype(BF16)
    dk_ref[...] = sec(2048, 512).astype(BF16)
    gate_ref[...] = sec(2560, LANES)
    vt = lax.dot_general(wvt_ref[...], xb, (((1,), (1,)), ((), ())),
                         preferred_element_type=F32).astype(BF16)
    for kb in range(dvt_ref.shape[1]):
        dvt_ref[0, kb] = vt[:, kb * ATT_K:(kb + 1) * ATT_K]


def _in_proj(x2, w_r, w_vt, bsz):
    n, d = x2.shape
    tm = PROJ_ROWS
    kb_per_tile = tm // ATT_K
    tiles_per_seq = n // bsz // tm
    widths = (256, 512, 256, 256, LANES, 512, 512)
    dtypes = (F32, F32, F32, F32, F32, BF16, BF16)
    wv = w_vt.shape[0]
    return pl.pallas_call(
        _in_proj_kernel,
        grid=(n // tm,),
        in_specs=[pl.BlockSpec((tm, d), lambda i: (i, 0)),
                  pl.BlockSpec(w_r.shape, lambda i: (0, 0)),
                  pl.BlockSpec(w_vt.shape, lambda i: (0, 0))],
        out_specs=[pl.BlockSpec((tm, w), lambda i: (i, 0)) for w in widths]
        + [pl.BlockSpec((1, kb_per_tile, wv, ATT_K),
                        lambda i: (i // tiles_per_seq, i % tiles_per_seq, 0, 0))],
        out_shape=[jax.ShapeDtypeStruct((n, w), dt) for w, dt in zip(widths, dtypes)]
        + [jax.ShapeDtypeStruct((bsz, tiles_per_seq * kb_per_tile, wv, ATT_K), BF16)],
        compiler_params=_params("parallel"),
        name="in_proj",
    )(x2, w_r, w_vt)


def _conv_kernel(x_ref, halo_ref, dw_ref, dwb_ref, g_ref, b_ref, pw_ref, pwb_ref,
                 out_ref, xs, ybuf):
    s = pl.program_id(1)
    rows = x_ref.shape[1]
    xs[0, 0:CONV_HALO, :] = jnp.where(s > 0, halo_ref[0], 0.0)
    xs[0, CONV_HALO:CONV_HALO + rows, :] = x_ref[0]
    span = CONV_HALO + rows - SUBLANES
    for k in range(1, SUBLANES):
        xs[k, 0:span, :] = xs[0, k:k + span, :]
    first = CONV_HALO - (CONV_KSIZE - 1)
    for r in range(0, rows, CONV_SUB):
        acc = jnp.broadcast_to(dwb_ref[...], (CONV_SUB, x_ref.shape[2]))
        for j in range(CONV_KSIZE):
            k = (first + j) % SUBLANES
            lo = r + first + j - k
            acc = acc + dw_ref[j:j + 1, :] * xs[k, lo:lo + CONV_SUB, :]
        y = _layer_norm_rows(acc, g_ref[...], b_ref[...])
        ybuf[r:r + CONV_SUB, :] = (y * _sigmoid(y)).astype(BF16)
    out = jnp.dot(ybuf[...], pw_ref[...], preferred_element_type=F32) + pwb_ref[...]
    out_ref[0] = out.astype(BF16)


def _conv_module(glu, dw, dwb, g, b, pw, pwb):
    bsz, s, ch = glu.shape
    ts = CONV_ROWS
    halo_per_tile = ts // CONV_HALO
    return pl.pallas_call(
        _conv_kernel,
        grid=(bsz, s // ts),
        in_specs=[
            pl.BlockSpec((1, ts, ch), lambda bi, si: (bi, si, 0)),
            pl.BlockSpec((1, CONV_HALO, ch),
                         lambda bi, si: (bi, jnp.maximum(si * halo_per_tile - 1, 0), 0)),
            pl.BlockSpec(dw.shape, lambda bi, si: (0, 0)),
            pl.BlockSpec(dwb.shape, lambda bi, si: (0, 0)),
            pl.BlockSpec(g.shape, lambda bi, si: (0, 0)),
            pl.BlockSpec(b.shape, lambda bi, si: (0, 0)),
            pl.BlockSpec(pw.shape, lambda bi, si: (0, 0)),
            pl.BlockSpec(pwb.shape, lambda bi, si: (0, 0)),
        ],
        out_specs=pl.BlockSpec((1, ts, ch), lambda bi, si: (bi, si, 0)),
        out_shape=jax.ShapeDtypeStruct((bsz, s, ch), BF16),
        scratch_shapes=[pltpu.VMEM((SUBLANES, CONV_HALO + ts, ch), F32),
                        pltpu.VMEM((ts, ch), BF16)],
        compiler_params=_params("parallel", "parallel"),
        name="conv_module",
    )(glu, glu, dw, dwb, g, b, pw, pwb)


def _mlstm_kernel(qk_ref, halo_ref, v_ref, o_ref, gate_ref, cw_ref, cb_ref, gb_ref,
                  ng_ref, sel_ref, out_ref, qpad, ct_ref, m_ref):
    c = pl.program_id(1)
    nb = qk_ref.shape[0]
    t = ML_CHUNK
    w_pair = 2 * ML_HEAD_DIM
    k_off = ML_HEADS * ML_HEAD_DIM
    n_pairs = ML_HEADS // 2
    nt = (((1,), (1,)), ((), ()))
    tn = (((0,), (0,)), ((), ()))

    @pl.when(c == 0)
    def _():
        ct_ref[...] = jnp.zeros_like(ct_ref)
        m_ref[...] = jnp.zeros_like(m_ref)

    row = lax.broadcasted_iota(jnp.int32, (t, t), 0)
    col = lax.broadcasted_iota(jnp.int32, (t, t), 1)
    causal = row >= col
    tri = causal.astype(F32)
    lane = lax.broadcasted_iota(jnp.int32, (t, w_pair), 1)
    low = lane < ML_HEAD_DIM
    owns = (low, lane >= ML_HEAD_DIM)

    first = SUBLANES - (ML_QK_CONV - 1)
    qk, gpre_t, log_f_t = [], [], []
    for bb in range(nb):
        qpad[bb, 0:SUBLANES, :] = jnp.where(c > 0, halo_ref[bb], 0.0)
        qpad[bb, SUBLANES:SUBLANES + t, :] = qk_ref[bb]
        acc = jnp.broadcast_to(cb_ref[...], (t, qk_ref.shape[2]))
        for j in range(ML_QK_CONV):
            acc = acc + cw_ref[j:j + 1, :] * qpad[bb, first + j:first + j + t, :]
        qk.append(acc * _sigmoid(acc))
        g_t = (gate_ref[bb] + gb_ref[...]).T[0:SUBLANES, :]
        gpre_t.append(g_t)
        log_f_t.append(jnp.minimum(g_t, 0.0) - jnp.log(1.0 + jnp.exp(-jnp.abs(g_t))))

    tri_t = (row <= col).astype(F32)
    cums = [jnp.dot(lf, tri_t, preferred_element_type=F32, precision=lax.Precision.HIGHEST)
            for lf in log_f_t]

    items = [(bb, h) for bb in range(nb) for h in range(ML_HEADS)]
    st = {}
    for bb in range(nb):
        for pair in range(n_pairs):
            lo = pair * w_pair
            k_pair = qk[bb][:, k_off + lo:k_off + lo + w_pair] * (ML_HEAD_DIM ** -0.5)
            st[bb, pair, "q"] = qk[bb][:, lo:lo + w_pair]
            st[bb, pair, "k"] = k_pair
            st[bb, pair, "kbt"] = k_pair.T.astype(BF16)
            st[bb, pair, "v"] = v_ref[bb, :, lo:lo + w_pair]
    for (bb, h) in items:
        pair, par = divmod(h, 2)
        st[bb, h, "q_m"] = jnp.where(owns[par], st[bb, pair, "q"], 0.0).astype(BF16)
        st[bb, h, "vx"] = jnp.where(owns[par], st[bb, pair, "v"], 1.0).astype(BF16)
    for (bb, h) in items:
        st[bb, h, "s_qk"] = jnp.dot(st[bb, h, "q_m"], st[bb, h // 2, "kbt"],
                                    preferred_element_type=F32)
    for (bb, h) in items:
        st[bb, h, "ct_prev"] = ct_ref[bb, h]
        st[bb, h, "inter"] = jnp.dot(st[bb, h, "q_m"], st[bb, h, "ct_prev"].astype(BF16),
                                     preferred_element_type=F32)

    head_row = lax.broadcasted_iota(jnp.int32, (SUBLANES, t), 0) < ML_HEADS
    lane_t = lax.broadcasted_iota(jnp.int32, (SUBLANES, t), 1)
    def last_lane(x):
        return jnp.broadcast_to(x[:, t - 1:t], x.shape)

    r_t, a_st_t, e_st_t, g_hl = [], [], [], []
    for bb in range(nb):
        b_t = pltpu.roll(cums[bb], ML_HEADS, 0)
        b_t = jnp.where(head_row, b_t, 0.0)
        r = jnp.where(head_row, gpre_t[bb] - b_t, 0.0)
        cm = r
        shift = 1
        while shift < t:
            cm = jnp.where(lane_t >= shift, jnp.maximum(cm, pltpu.roll(cm, shift, 1)), cm)
            shift *= 2
        m_prev = m_ref[bb]
        u = jnp.maximum(m_prev, cm)
        a_inter = jnp.exp(m_prev - u)
        floor = jnp.exp(-(b_t + u))
        g_end = last_lane(b_t)
        cm_end = last_lane(cm)
        p_end = jnp.exp(r - cm_end)
        m_loc = g_end + cm_end
        m_new = jnp.maximum(g_end + m_prev, m_loc)
        a_st_t.append(jnp.exp(g_end + m_prev - m_new))
        e_st_t.append(jnp.exp(m_loc - m_new))
        m_ref[bb] = jnp.where(head_row, m_new, 0.0)
        r_t.append(r)
        parts = []
        for tile in (u, a_inter, floor, p_end):
            hi = tile.astype(BF16)
            parts += [hi, (tile - hi.astype(F32)).astype(BF16)]
        g_hl.append(jnp.concatenate(parts, axis=0))
    rep = {(bb, h): lax.dot_general(g_hl[bb], sel_ref[h], tn, preferred_element_type=F32)
           for bb in range(nb) for h in range(ML_HEADS)}

    def tile_rows(x_row):
        return jnp.tile(jnp.broadcast_to(x_row, (SUBLANES, LANES)), (t // SUBLANES, 1))

    for (bb, h) in items:
        pair, par = divmod(h, 2)
        u_col = rep[bb, h][:, 0:LANES]
        st[bb, h, "a_inter"] = rep[bb, h][:, LANES:2 * LANES]
        st[bb, h, "floor"] = rep[bb, h][:, 2 * LANES:3 * LANES]
        p_end = rep[bb, h][:, 3 * LANES:4 * LANES]
        r_row = tile_rows(r_t[bb][h:h + 1, :])
        st[bb, h, "dexp"] = jnp.exp(jnp.where(causal, r_row - u_col, -jnp.inf))
        kp = (jnp.where(owns[par], st[bb, pair, "k"], 0.0) * p_end).astype(BF16)
        c_loc = lax.dot_general(kp, st[bb, h, "vx"], tn, preferred_element_type=F32)
        ct_ref[bb, h] = (tile_rows(a_st_t[bb][h:h + 1, :]) * st[bb, h, "ct_prev"]
                         + tile_rows(e_st_t[bb][h:h + 1, :]) * c_loc)

    for (bb, h) in items:
        sc = (st[bb, h, "s_qk"] * st[bb, h, "dexp"]).astype(BF16)
        st[bb, h, "intra"] = jnp.dot(sc, st[bb, h, "vx"], preferred_element_type=F32)
    for (bb, h) in items:
        numden = st[bb, h, "a_inter"] * st[bb, h, "inter"] + st[bb, h, "intra"]
        den = pltpu.roll(numden, ML_HEAD_DIM, 1)
        st[bb, h, "hh"] = numden / jnp.maximum(jnp.abs(den), st[bb, h, "floor"])

    inv = 1.0 / ML_HEAD_DIM
    for bb in range(nb):
        for pair in range(n_pairs):
            lo = pair * w_pair
            h_pair = jnp.where(low, st[bb, 2 * pair, "hh"], st[bb, 2 * pair + 1, "hh"])
            x = h_pair * _sigmoid(o_ref[bb, :, lo:lo + w_pair])
            s_low = jnp.sum(jnp.where(low, x, 0.0), axis=-1, keepdims=True)
            s_all = jnp.sum(x, axis=-1, keepdims=True)
            xc = x - jnp.where(low, s_low, s_all - s_low) * inv
            sq = xc * xc
            v_low = jnp.sum(jnp.where(low, sq, 0.0), axis=-1, keepdims=True)
            v_all = jnp.sum(sq, axis=-1, keepdims=True)
            var = jnp.where(low, v_low, v_all - v_low) * inv
            y = xc * lax.rsqrt(var + LN_EPS) * ng_ref[:, lo:lo + w_pair]
            out_ref[bb, :, lo:lo + w_pair] = y.astype(BF16)


def _mlstm(mqk, mv, mo, gates, cw, cb, gb, ng):
    bsz, s, wqk = mqk.shape
    wv = mv.shape[2]
    t = ML_CHUNK
    nb = math.gcd(bsz, ML_BATCH)
    halo_per_chunk = t // SUBLANES
    n_rep = 4
    src = jnp.arange(n_rep * 2 * SUBLANES)[:, None]
    dst = jnp.arange(n_rep * LANES)[None, :] // LANES
    sel = jnp.stack([((src // (2 * SUBLANES) == dst) & (src % SUBLANES == h)).astype(BF16)
                     for h in range(ML_HEADS)])
    const = lambda bi, ci: (0, 0)
    return pl.pallas_call(
        _mlstm_kernel,
        grid=(bsz // nb, s // t),
        in_specs=[
            pl.BlockSpec((nb, t, wqk), lambda bi, ci: (bi, ci, 0)),
            pl.BlockSpec((nb, SUBLANES, wqk),
                         lambda bi, ci: (bi, jnp.maximum(ci * halo_per_chunk - 1, 0), 0)),
            pl.BlockSpec((nb, t, wv), lambda bi, ci: (bi, ci, 0)),
            pl.BlockSpec((nb, t, wv), lambda bi, ci: (bi, ci, 0)),
            pl.BlockSpec((nb, t, LANES), lambda bi, ci: (bi, ci, 0)),
            pl.BlockSpec(cw.shape, const),
            pl.BlockSpec(cb.shape, const),
            pl.BlockSpec(gb.shape, const),
            pl.BlockSpec(ng.shape, const),
            pl.BlockSpec(sel.shape, lambda bi, ci: (0, 0, 0)),
        ],
        out_specs=pl.BlockSpec((nb, t, wv), lambda bi, ci: (bi, ci, 0)),
        out_shape=jax.ShapeDtypeStruct((bsz, s, wv), BF16),
        scratch_shapes=[pltpu.VMEM((nb, SUBLANES + t, wqk), F32),
                        pltpu.VMEM((nb, ML_HEADS, 2 * ML_HEAD_DIM, LANES), F32),
                        pltpu.VMEM((nb, SUBLANES, LANES), F32)],
        compiler_params=_params("parallel", "arbitrary"),
        name="mlstm",
    )(mqk, mqk, mv, mo, gates, cw, cb, gb, ng, sel)


def _attn_kernel(q_ref, qn_ref, k_ref, vt_ref, lam_ref, ng_ref, out_ref, qm_ref, acc_ref,
                 s0_ref, s1_ref, mblk_ref, m_ref, *, lambda_init):
    i = pl.program_id(2)
    tq = q_ref.shape[1]
    tk = vt_ref.shape[3]
    lane = lax.broadcasted_iota(jnp.int32, (tq, 2 * DA_QK_DIM), 1)

    def map_operands(q):
        zero = jnp.zeros_like(q)
        return jnp.where(lane < DA_QK_DIM, q, zero), jnp.where(lane >= DA_QK_DIM, q, zero)

    qm_ref[0], qm_ref[1] = map_operands(q_ref[0])
    acc_ref[...] = jnp.zeros_like(acc_ref)
    m_ref[...] = jnp.full_like(m_ref, -jnp.inf)

    ones_rows = jnp.ones((ATT_SUM_ROWS, tk), BF16)

    def scores(j, s_ref, slot, qms=None):
        k_blk = k_ref[0, pl.ds(pl.multiple_of(j * tk, tk), tk), :]
        for mp in range(2):
            qm = qm_ref[mp] if qms is None else qms[mp]
            st = lax.dot_general(k_blk, qm, (((1,), (1,)), ((), ())),
                                 preferred_element_type=F32)
            s_ref[mp] = st
            mblk_ref[slot, mp] = jnp.max(st, axis=0, keepdims=True)

    def softmax_pv(j, s_ref, slot, masked):
        vt_blk = jnp.concatenate([vt_ref[0, j], ones_rows], axis=0)
        if masked:
            key = lax.broadcasted_iota(jnp.int32, (tk, tq), 0)
            qry = lax.broadcasted_iota(jnp.int32, (tk, tq), 1)
            keep = key <= qry
        for mp in range(2):
            st = s_ref[mp]
            if masked:
                st = jnp.where(keep, st, -jnp.inf)
                m_blk = jnp.max(st, axis=0, keepdims=True)
            else:
                m_blk = mblk_ref[slot, mp]
            m_old = m_ref[mp]
            m_new = jnp.maximum(m_old, m_blk)
            alpha = jnp.exp2(m_old - m_new)
            pexp = jnp.exp2(st - m_new).astype(BF16)
            acc_ref[mp] = alpha * acc_ref[mp] + jnp.dot(vt_blk, pexp,
                                                       preferred_element_type=F32)
            m_ref[mp] = m_new

    def run(sa_ref, a, sb_ref, b):
        @pl.when(i == 0)
        def _():
            scores(0, sa_ref, a)

        def pair(jj, carry):
            j0 = 2 * jj
            scores(j0 + 1, sb_ref, b)
            softmax_pv(j0, sa_ref, a, False)
            scores(j0 + 2, sa_ref, a)
            softmax_pv(j0 + 1, sb_ref, b, False)
            return carry

        lax.fori_loop(0, i // 2, pair, 0)

        @pl.when(i % 2 == 1)
        def _():
            scores(i, sb_ref, b)
            softmax_pv(i - 1, sa_ref, a, False)
            scores(0, sa_ref, a, map_operands(qn_ref[0]))
            softmax_pv(i, sb_ref, b, True)

        @pl.when(i % 2 == 0)
        def _():
            scores(0, sb_ref, b, map_operands(qn_ref[0]))
            softmax_pv(i, sa_ref, a, True)

    start = ((i + 1) // 2) % 2

    @pl.when(start == 0)
    def _():
        run(s0_ref, 0, s1_ref, 1)

    @pl.when(start == 1)
    def _():
        run(s1_ref, 1, s0_ref, 0)

    lp = lam_ref[...]
    lam = (jnp.exp(jnp.sum(lp[0:1, :] * lp[1:2, :], axis=-1, keepdims=True))
           - jnp.exp(jnp.sum(lp[2:3, :] * lp[3:4, :], axis=-1, keepdims=True))
           + lambda_init)
    dv = DA_V_DIM
    o_t = (acc_ref[0, 0:dv, :] / acc_ref[0, dv:dv + 1, :]
           - lam * (acc_ref[1, 0:dv, :] / acc_ref[1, dv:dv + 1, :]))
    ms = jnp.mean(o_t * o_t, axis=0, keepdims=True)
    y_t = o_t * lax.rsqrt(ms + LN_EPS) * (ng_ref[...] * (1.0 - lambda_init))
    out_ref[0] = y_t.T.astype(BF16)


def _diff_attention(dq, dk, dvt, lam_p, ng_col, lambda_init):
    bsz, s, _ = dq.shape
    tq = ATT_Q
    nk, tk = dvt.shape[1], dvt.shape[3]
    assert tq == tk and nk * tk == s
    nq = s // tq
    return pl.pallas_call(
        functools.partial(_attn_kernel, lambda_init=lambda_init),
        grid=(bsz, DA_HEADS, nq),
        in_specs=[
            pl.BlockSpec((1, tq, 2 * DA_QK_DIM), lambda b, h, i: (b, i, h)),
            pl.BlockSpec((1, tq, 2 * DA_QK_DIM),
                         lambda b, h, i: (b, jnp.minimum(i + 1, nq - 1), h)),
            pl.BlockSpec((1, s, 2 * DA_QK_DIM), lambda b, h, i: (b, 0, h)),
            pl.BlockSpec((1, nk, DA_V_DIM, tk), lambda b, h, i: (b, 0, h, 0)),
            pl.BlockSpec(lam_p.shape, lambda b, h, i: (0, 0)),
            pl.BlockSpec(ng_col.shape, lambda b, h, i: (0, 0)),
        ],
        out_specs=pl.BlockSpec((1, tq, DA_V_DIM), lambda b, h, i: (b, i, h)),
        out_shape=jax.ShapeDtypeStruct((bsz, s, DA_HEADS * DA_V_DIM), BF16),
        scratch_shapes=[pltpu.VMEM((2, tq, 2 * DA_QK_DIM), BF16),
                        pltpu.VMEM((2, DA_V_DIM + ATT_SUM_ROWS, tq), F32),
                        pltpu.VMEM((2, tk, tq), F32),
                        pltpu.VMEM((2, tk, tq), F32),
                        pltpu.VMEM((2, 2, 1, tq), F32),
                        pltpu.VMEM((2, 1, tq), F32)],
        compiler_params=_params("parallel", "parallel", "arbitrary"),
        name="diff_attention",
    )(dq, dq, dk, dvt, lam_p, ng_col)


def _out_mlp_kernel(yc_ref, ym_ref, yd_ref, x_ref, wo_ref, g1_ref, b1_ref, wu_ref, wd_ref,
                    g2_ref, b2_ref, out_ref, x1_ref, xb_ref, acc_ref):
    f = pl.program_id(1)
    last = pl.num_programs(1) - 1
    wc = yc_ref.shape[1]
    wm = ym_ref.shape[1]

    def step(first, final):
        for r in range(0, x_ref.shape[0], MLP_SUB):
            rows = pl.ds(r, MLP_SUB)
            if first:
                h = jnp.dot(yc_ref[rows, :], wo_ref[0:wc, :], preferred_element_type=F32)
                h = h + jnp.dot(ym_ref[rows, :], wo_ref[wc:wc + wm, :],
                                preferred_element_type=F32)
                h = h + jnp.dot(yd_ref[rows, :], wo_ref[wc + wm:, :],
                                preferred_element_type=F32)
                x1 = _layer_norm_rows(DEEPNORM_ALPHA * x_ref[rows, :] + h,
                                      g1_ref[...], b1_ref[...])
                x1_ref[rows, :] = x1
                xb = x1.astype(BF16)
                xb_ref[rows, :] = xb
            else:
                xb = xb_ref[rows, :]
            up = jnp.maximum(jnp.dot(xb, wu_ref[...], preferred_element_type=F32), 0.0)
            down = jnp.dot((up * up).astype(BF16), wd_ref[...], preferred_element_type=F32)
            acc = down if first else acc_ref[rows, :] + down
            if final:
                out_ref[rows, :] = _layer_norm_rows(DEEPNORM_ALPHA * x1_ref[rows, :] + acc,
                                                    g2_ref[...], b2_ref[...])
            else:
                acc_ref[rows, :] = acc

    @pl.when(f == 0)
    def _():
        step(True, False)

    @pl.when(jnp.logical_and(f > 0, f < last))
    def _():
        step(False, False)

    @pl.when(f == last)
    def _():
        step(False, True)


def _out_mlp(yc, ym, yd, x2, wo, g1, b1, wu, wd, g2, b2):
    n, d = x2.shape
    dff = wu.shape[1]
    tm, tf = MLP_ROWS, MLP_FF
    assert dff // tf >= 2, "the first and last hidden steps are distinct code paths"
    rows = lambda i, f: (i, 0)
    const = lambda i, f: (0, 0)
    return pl.pallas_call(
        _out_mlp_kernel,
        grid=(n // tm, dff // tf),
        in_specs=[pl.BlockSpec((tm, yc.shape[1]), rows),
                  pl.BlockSpec((tm, ym.shape[1]), rows),
                  pl.BlockSpec((tm, yd.shape[1]), rows),
                  pl.BlockSpec((tm, d), rows),
                  pl.BlockSpec(wo.shape, const),
                  pl.BlockSpec(g1.shape, const),
                  pl.BlockSpec(b1.shape, const),
                  pl.BlockSpec((d, tf), lambda i, f: (0, f)),
                  pl.BlockSpec((tf, d), lambda i, f: (f, 0)),
                  pl.BlockSpec(g2.shape, const),
                  pl.BlockSpec(b2.shape, const)],
        out_specs=pl.BlockSpec((tm, d), rows),
        out_shape=jax.ShapeDtypeStruct((n, d), F32),
        scratch_shapes=[pltpu.VMEM((tm, d), F32), pltpu.VMEM((tm, d), BF16),
                        pltpu.VMEM((tm, d), F32)],
        compiler_params=_params("parallel", "arbitrary"),
        name="out_mlp_ln",
    )(yc, ym, yd, x2, wo, g1, b1, wu, wd, g2, b2)


def _row(v):
    return v.reshape(1, -1).astype(F32)


def _pad_rows(w, rows):
    return jnp.pad(w.astype(F32), ((0, rows - w.shape[0]), (0, 0)))


def _rearranged_w_in(w):
    d = w.shape[0]
    gate_lo = 256 + 256 + 512 + 256 + 256
    gate_hi = gate_lo + 2 * ML_HEADS
    v_lo = gate_hi + 2 * DA_HEADS * 2 * DA_QK_DIM
    pad = jnp.zeros((d, LANES - 2 * ML_HEADS), w.dtype)
    w_r = jnp.concatenate([w[:, :gate_lo], w[:, gate_hi:v_lo], w[:, gate_lo:gate_hi], pad], axis=1)
    return w_r.astype(BF16), w[:, v_lo:].T.astype(BF16)


def kernel(x, w_in, b_igate, b_fgate, conv_dw_w, conv_dw_b, conv_ln_g, conv_ln_b, conv_pw_w, conv_pw_b, ml_conv_w, ml_conv_b, ml_norm_g, lam_q1, lam_k1, lam_q2, lam_k2, da_norm_g, w_out, ln1_g, ln1_b, w_up, w_down, ln2_g, ln2_b):
    bsz, s, d = x.shape
    n = bsz * s
    x2 = x.reshape(n, d)
    for l in range(DEPTH):
        lambda_init = 0.8 - 0.6 * math.exp(-0.3 * l)
        w_r, w_vt = _rearranged_w_in(w_in[l])
        glu, mqk, mv, mo, gates, dq, dk, dvt = _in_proj(x2, w_r, w_vt, bsz)
        seq = lambda a: a.reshape(bsz, s, a.shape[1])

        y_conv = _conv_module(seq(glu), _pad_rows(conv_dw_w[l], 32), _row(conv_dw_b[l]),
                              _row(conv_ln_g[l]), _row(conv_ln_b[l]),
                              conv_pw_w[l].astype(BF16), _row(conv_pw_b[l]))

        gate_bias = jnp.pad(jnp.concatenate([b_igate[l], b_fgate[l]]).astype(F32),
                            (0, LANES - 2 * ML_HEADS)).reshape(1, LANES)
        y_ml = _mlstm(seq(mqk), seq(mv), seq(mo), seq(gates),
                      _pad_rows(ml_conv_w[l], SUBLANES), _row(ml_conv_b[l]),
                      gate_bias, _row(ml_norm_g[l]))

        lam_p = jnp.stack([lam_q1[l], lam_k1[l], lam_q2[l], lam_k2[l]]).astype(F32)
        y_da = _diff_attention(seq(dq), seq(dk), dvt, lam_p,
                               da_norm_g[l].astype(F32).reshape(-1, 1), lambda_init)

        x2 = _out_mlp(y_conv.reshape(n, -1), y_ml.reshape(n, -1), y_da.reshape(n, -1), x2,
                      w_out[l].astype(BF16), _row(ln1_g[l]), _row(ln1_b[l]),
                      w_up[l].astype(BF16), w_down[l].astype(BF16),
                      _row(ln2_g[l]), _row(ln2_b[l]))
    return x2.reshape(bsz, s, d)
```

```python
import functools
import math

import jax
import jax.numpy as jnp
from jax import lax
from jax.experimental import pallas as pl
from jax.experimental.pallas import tpu as pltpu

F32 = jnp.float32
BF16 = jnp.bfloat16

LANES = 128
SUBLANES = 8
VMEM_LIMIT_BYTES = 56 * 1024 * 1024

DEPTH = 2
CONV_KSIZE = 31
ML_HEADS = 4
ML_HEAD_DIM = 64
ML_QK_CONV = 4
ML_CHUNK = 128
DA_HEADS = 4
DA_QK_DIM = 64
DA_V_DIM = 128
LN_EPS = 1e-5
DEEPNORM_ALPHA = (2 * DEPTH) ** 0.25

PROJ_ROWS = 1024
CONV_ROWS = 1024
CONV_HALO = 32
CONV_SUB = 64
ATT_Q = 512
ATT_K = 512
ML_BATCH = 4
ATT_UNROLL = 4
ATT_SUM_ROWS = 16
MLP_ROWS = 1024
MLP_FF = 1024
MLP_SUB = 256


def _params(*semantics):
    return pltpu.CompilerParams(dimension_semantics=semantics,
                                vmem_limit_bytes=VMEM_LIMIT_BYTES)


def _sigmoid(x):
    return 1.0 / (1.0 + jnp.exp(-x))


def _layer_norm_rows(x, g, b):
    mu = jnp.mean(x, axis=-1, keepdims=True)
    xc = x - mu
    var = jnp.mean(xc * xc, axis=-1, keepdims=True)
    return xc * lax.rsqrt(var + LN_EPS) * g + b


def _in_proj_kernel(x_ref, w_ref, wvt_ref, glu_ref, mqk_ref, mv_ref, mo_ref, gate_ref,
                    dq_ref, dk_ref, dvt_ref):
    xb = x_ref[...].astype(BF16)

    def sec(lo, width):
        return jnp.dot(xb, w_ref[:, lo:lo + width], preferred_element_type=F32)

    glu_ref[...] = sec(0, 256) * _sigmoid(sec(256, 256))
    mqk_ref[...] = sec(512, 512)
    mv_ref[...] = sec(1024, 256)
    mo_ref[...] = sec(1280, 256)
    dq_ref[...] = (sec(1536, 512) * (DA_QK_DIM ** -0.5 * math.log2(math.e))).astype(BF16)
    dk_ref[...] = sec(2048, 512).astype(BF16)
    gate_ref[...] = sec(2560, LANES)
    vt = lax.dot_general(wvt_ref[...], xb, (((1,), (1,)), ((), ())),
                         preferred_element_type=F32).astype(BF16)
    for kb in range(dvt_ref.shape[1]):
        dvt_ref[0, kb] = vt[:, kb * ATT_K:(kb + 1) * ATT_K]


def _in_proj(x2, w_r, w_vt, bsz):
    n, d = x2.shape
    tm = PROJ_ROWS
    kb_per_tile = tm // ATT_K
    tiles_per_seq = n // bsz // tm
    widths = (256, 512, 256, 256, LANES, 512, 512)
    dtypes = (F32, F32, F32, F32, F32, BF16, BF16)
    wv = w_vt.shape[0]
    return pl.pallas_call(
        _in_proj_kernel,
        grid=(n // tm,),
        in_specs=[pl.BlockSpec((tm, d), lambda i: (i, 0)),
                  pl.BlockSpec(w_r.shape, lambda i: (0, 0)),
                  pl.BlockSpec(w_vt.shape, lambda i: (0, 0))],
        out_specs=[pl.BlockSpec((tm, w), lambda i: (i, 0)) for w in widths]
        + [pl.BlockSpec((1, kb_per_tile, wv, ATT_K),
                        lambda i: (i // tiles_per_seq, i % tiles_per_seq, 0, 0))],
        out_shape=[jax.ShapeDtypeStruct((n, w), dt) for w, dt in zip(widths, dtypes)]
        + [jax.ShapeDtypeStruct((bsz, tiles_per_seq * kb_per_tile, wv, ATT_K), BF16)],
        compiler_params=_params("parallel"),
        name="in_proj",
    )(x2, w_r, w_vt)


def _conv_kernel(x_ref, halo_ref, dw_ref, dwb_ref, g_ref, b_ref, pw_ref, pwb_ref,
                 out_ref, xs, ybuf):
    s = pl.program_id(1)
    rows = x_ref.shape[1]
    xs[0, 0:CONV_HALO, :] = jnp.where(s > 0, halo_ref[0], 0.0)
    xs[0, CONV_HALO:CONV_HALO + rows, :] = x_ref[0]
    span = CONV_HALO + rows - SUBLANES
    for k in range(1, SUBLANES):
        xs[k, 0:span, :] = xs[0, k:k + span, :]
    first = CONV_HALO - (CONV_KSIZE - 1)
    for r in range(0, rows, CONV_SUB):
        acc = jnp.broadcast_to(dwb_ref[...], (CONV_SUB, x_ref.shape[2]))
        for j in range(CONV_KSIZE):
            k = (first + j) % SUBLANES
            lo = r + first + j - k
            acc = acc + dw_ref[j:j + 1, :] * xs[k, lo:lo + CONV_SUB, :]
        y = _layer_norm_rows(acc, g_ref[...], b_ref[...])
        ybuf[r:r + CONV_SUB, :] = (y * _sigmoid(y)).astype(BF16)
    out = jnp.dot(ybuf[...], pw_ref[...], preferred_element_type=F32) + pwb_ref[...]
    out_ref[0] = out.astype(BF16)


def _conv_module(glu, dw, dwb, g, b, pw, pwb):
    bsz, s, ch = glu.shape
    ts = CONV_ROWS
    halo_per_tile = ts // CONV_HALO
    return pl.pallas_call(
        _conv_kernel,
        grid=(bsz, s // ts),
        in_specs=[
            pl.BlockSpec((1, ts, ch), lambda bi, si: (bi, si, 0)),
            pl.BlockSpec((1, CONV_HALO, ch),
                         lambda bi, si: (bi, jnp.maximum(si * halo_per_tile - 1, 0), 0)),
            pl.BlockSpec(dw.shape, lambda bi, si: (0, 0)),
            pl.BlockSpec(dwb.shape, lambda bi, si: (0, 0)),
            pl.BlockSpec(g.shape, lambda bi, si: (0, 0)),
            pl.BlockSpec(b.shape, lambda bi, si: (0, 0)),
            pl.BlockSpec(pw.shape, lambda bi, si: (0, 0)),
            pl.BlockSpec(pwb.shape, lambda bi, si: (0, 0)),
        ],
        out_specs=pl.BlockSpec((1, ts, ch), lambda bi, si: (bi, si, 0)),
        out_shape=jax.ShapeDtypeStruct((bsz, s, ch), BF16),
        scratch_shapes=[pltpu.VMEM((SUBLANES, CONV_HALO + ts, ch), F32),
                        pltpu.VMEM((ts, ch), BF16)],
        compiler_params=_params("parallel", "parallel"),
        name="conv_module",
    )(glu, glu, dw, dwb, g, b, pw, pwb)


def _mlstm_kernel(qk_ref, halo_ref, v_ref, o_ref, gate_ref, cw_ref, cb_ref, gb_ref,
                  ng_ref, sel_ref, out_ref, qpad, ct_ref, m_ref):
    c = pl.program_id(1)
    nb = qk_ref.shape[0]
    t = ML_CHUNK
    w_pair = 2 * ML_HEAD_DIM
    k_off = ML_HEADS * ML_HEAD_DIM
    n_pairs = ML_HEADS // 2
    nt = (((1,), (1,)), ((), ()))
    tn = (((0,), (0,)), ((), ()))

    @pl.when(c == 0)
    def _():
        ct_ref[...] = jnp.zeros_like(ct_ref)
        m_ref[...] = jnp.zeros_like(m_ref)

    row = lax.broadcasted_iota(jnp.int32, (t, t), 0)
    col = lax.broadcasted_iota(jnp.int32, (t, t), 1)
    causal = row >= col
    tri = causal.astype(F32)
    lane = lax.broadcasted_iota(jnp.int32, (t, w_pair), 1)
    low = lane < ML_HEAD_DIM
    owns = (low, lane >= ML_HEAD_DIM)

    first = SUBLANES - (ML_QK_CONV - 1)
    qk, gpre_t, log_f_t = [], [], []
    for bb in range(nb):
        qpad[bb, 0:SUBLANES, :] = jnp.where(c > 0, halo_ref[bb], 0.0)
        qpad[bb, SUBLANES:SUBLANES + t, :] = qk_ref[bb]
        acc = jnp.broadcast_to(cb_ref[...], (t, qk_ref.shape[2]))
        for j in range(ML_QK_CONV):
            acc = acc + cw_ref[j:j + 1, :] * qpad[bb, first + j:first + j + t, :]
        qk.append(acc * _sigmoid(acc))
        g_t = (gate_ref[bb] + gb_ref[...]).T[0:SUBLANES, :]
        gpre_t.append(g_t)
        log_f_t.append(jnp.minimum(g_t, 0.0) - jnp.log(1.0 + jnp.exp(-jnp.abs(g_t))))

    tri_t = (row <= col).astype(F32)
    cums = [jnp.dot(lf, tri_t, preferred_element_type=F32, precision=lax.Precision.HIGHEST)
            for lf in log_f_t]

    items = [(bb, h) for bb in range(nb) for h in range(ML_HEADS)]
    st = {}
    for bb in range(nb):
        for pair in range(n_pairs):
            lo = pair * w_pair
            k_pair = qk[bb][:, k_off + lo:k_off + lo + w_pair] * (ML_HEAD_DIM ** -0.5)
            st[bb, pair, "q"] = qk[bb][:, lo:lo + w_pair]
            st[bb, pair, "k"] = k_pair
            st[bb, pair, "kbt"] = k_pair.T.astype(BF16)
            st[bb, pair, "v"] = v_ref[bb, :, lo:lo + w_pair]
    for (bb, h) in items:
        pair, par = divmod(h, 2)
        st[bb, h, "q_m"] = jnp.where(owns[par], st[bb, pair, "q"], 0.0).astype(BF16)
        st[bb, h, "vx"] = jnp.where(owns[par], st[bb, pair, "v"], 1.0).astype(BF16)
    for (bb, h) in items:
        st[bb, h, "s_qk"] = jnp.dot(st[bb, h, "q_m"], st[bb, h // 2, "kbt"],
                                    preferred_element_type=F32)
    for (bb, h) in items:
        st[bb, h, "ct_prev"] = ct_ref[bb, h]
        st[bb, h, "inter"] = jnp.dot(st[bb, h, "q_m"], st[bb, h, "ct_prev"].astype(BF16),
                                     preferred_element_type=F32)

    head_row = lax.broadcasted_iota(jnp.int32, (SUBLANES, t), 0) < ML_HEADS
    lane_t = lax.broadcasted_iota(jnp.int32, (SUBLANES, t), 1)
    def last_lane(x):
        return jnp.broadcast_to(x[:, t - 1:t], x.shape)

    r_t, a_st_t, e_st_t, g_hl = [], [], [], []
    for bb in range(nb):
        b_t = pltpu.roll(cums[bb], ML_HEADS, 0)
        b_t = jnp.where(head_row, b_t, 0.0)
        r = jnp.where(head_row, gpre_t[bb] - b_t, 0.0)
        cm = r
        shift = 1
        while shift < t:
            cm = jnp.where(lane_t >= shift, jnp.maximum(cm, pltpu.roll(cm, shift, 1)), cm)
            shift *= 2
        m_prev = m_ref[bb]
        u = jnp.maximum(m_prev, cm)
        a_inter = jnp.exp(m_prev - u)
        floor = jnp.exp(-(b_t + u))
        g_end = last_lane(b_t)
        cm_end = last_lane(cm)
        p_end = jnp.exp(r - cm_end)
        m_loc = g_end + cm_end
        m_new = jnp.maximum(g_end + m_prev, m_loc)
        a_st_t.append(jnp.exp(g_end + m_prev - m_new))
        e_st_t.append(jnp.exp(m_loc - m_new))
        m_ref[bb] = jnp.where(head_row, m_new, 0.0)
        r_t.append(r)
        parts = []
        for tile in (u, a_inter, floor, p_end):
            hi = tile.astype(BF16)
            parts += [hi, (tile - hi.astype(F32)).astype(BF16)]
        g_hl.append(jnp.concatenate(parts, axis=0))
    rep = {(bb, h): lax.dot_general(g_hl[bb], sel_ref[h], tn, preferred_element_type=F32)
           for bb in range(nb) for h in range(ML_HEADS)}

    def tile_rows(x_row):
        return jnp.tile(jnp.broadcast_to(x_row, (SUBLANES, LANES)), (t // SUBLANES, 1))

    for (bb, h) in items:
        pair, par = divmod(h, 2)
        u_col = rep[bb, h][:, 0:LANES]
        st[bb, h, "a_inter"] = rep[bb, h][:, LANES:2 * LANES]
        st[bb, h, "floor"] = rep[bb, h][:, 2 * LANES:3 * LANES]
        p_end = rep[bb, h][:, 3 * LANES:4 * LANES]
        r_row = tile_rows(r_t[bb][h:h + 1, :])
        st[bb, h, "dexp"] = jnp.exp(jnp.where(causal, r_row - u_col, -jnp.inf))
        kp = (jnp.where(owns[par], st[bb, pair, "k"], 0.0) * p_end).astype(BF16)
        c_loc = lax.dot_general(kp, st[bb, h, "vx"], tn, preferred_element_type=F32)
        ct_ref[bb, h] = (tile_rows(a_st_t[bb][h:h + 1, :]) * st[bb, h, "ct_prev"]
                         + tile_rows(e_st_t[bb][h:h + 1, :]) * c_loc)

    for (bb, h) in items:
        sc = (st[bb, h, "s_qk"] * st[bb, h, "dexp"]).astype(BF16)
        st[bb, h, "intra"] = jnp.dot(sc, st[bb, h, "vx"], preferred_element_type=F32)
    for (bb, h) in items:
        numden = st[bb, h, "a_inter"] * st[bb, h, "inter"] + st[bb, h, "intra"]
        den = pltpu.roll(numden, ML_HEAD_DIM, 1)
        st[bb, h, "hh"] = numden / jnp.maximum(jnp.abs(den), st[bb, h, "floor"])

    inv = 1.0 / ML_HEAD_DIM
    for bb in range(nb):
        for pair in range(n_pairs):
            lo = pair * w_pair
            h_pair = jnp.where(low, st[bb, 2 * pair, "hh"], st[bb, 2 * pair + 1, "hh"])
            x = h_pair * _sigmoid(o_ref[bb, :, lo:lo + w_pair])
            s_low = jnp.sum(jnp.where(low, x, 0.0), axis=-1, keepdims=True)
            s_all = jnp.sum(x, axis=-1, keepdims=True)
            xc = x - jnp.where(low, s_low, s_all - s_low) * inv
            sq = xc * xc
            v_low = jnp.sum(jnp.where(low, sq, 0.0), axis=-1, keepdims=True)
            v_all = jnp.sum(sq, axis=-1, keepdims=True)
            var = jnp.where(low, v_low, v_all - v_low) * inv
            y = xc * lax.rsqrt(var + LN_EPS) * ng_ref[:, lo:lo + w_pair]
            out_ref[bb, :, lo:lo + w_pair] = y.astype(BF16)


def _mlstm(mqk, mv, mo, gates, cw, cb, gb, ng):
    bsz, s, wqk = mqk.shape
    wv = mv.shape[2]
    t = ML_CHUNK
    nb = math.gcd(bsz, ML_BATCH)
    halo_per_chunk = t // SUBLANES
    n_rep = 4
    src = jnp.arange(n_rep * 2 * SUBLANES)[:, None]
    dst = jnp.arange(n_rep * LANES)[None, :] // LANES
    sel = jnp.stack([((src // (2 * SUBLANES) == dst) & (src % SUBLANES == h)).astype(BF16)
                     for h in range(ML_HEADS)])
    const = lambda bi, ci: (0, 0)
    return pl.pallas_call(
        _mlstm_kernel,
        grid=(bsz // nb, s // t),
        in_specs=[
            pl.BlockSpec((nb, t, wqk), lambda bi, ci: (bi, ci, 0)),
            pl.BlockSpec((nb, SUBLANES, wqk),
                         lambda bi, ci: (bi, jnp.maximum(ci * halo_per_chunk - 1, 0), 0)),
            pl.BlockSpec((nb, t, wv), lambda bi, ci: (bi, ci, 0)),
            pl.BlockSpec((nb, t, wv), lambda bi, ci: (bi, ci, 0)),
            pl.BlockSpec((nb, t, LANES), lambda bi, ci: (bi, ci, 0)),
            pl.BlockSpec(cw.shape, const),
            pl.BlockSpec(cb.shape, const),
            pl.BlockSpec(gb.shape, const),
            pl.BlockSpec(ng.shape, const),
            pl.BlockSpec(sel.shape, lambda bi, ci: (0, 0, 0)),
        ],
        out_specs=pl.BlockSpec((nb, t, wv), lambda bi, ci: (bi, ci, 0)),
        out_shape=jax.ShapeDtypeStruct((bsz, s, wv), BF16),
        scratch_shapes=[pltpu.VMEM((nb, SUBLANES + t, wqk), F32),
                        pltpu.VMEM((nb, ML_HEADS, 2 * ML_HEAD_DIM, LANES), F32),
                        pltpu.VMEM((nb, SUBLANES, LANES), F32)],
        compiler_params=_params("parallel", "arbitrary"),
        name="mlstm",
    )(mqk, mqk, mv, mo, gates, cw, cb, gb, ng, sel)


def _attn_kernel(q_ref, qn_ref, k_ref, vt_ref, lam_ref, ng_ref, out_ref, qm_ref, acc_ref,
                 s0_ref, s1_ref, mblk_ref, m_ref, *, lambda_init):
    i = pl.program_id(2)
    tq = q_ref.shape[1]
    tk = vt_ref.shape[3]
    lane = lax.broadcasted_iota(jnp.int32, (tq, 2 * DA_QK_DIM), 1)

    def map_operands(q):
        zero = jnp.zeros_like(q)
        return jnp.where(lane < DA_QK_DIM, q, zero), jnp.where(lane >= DA_QK_DIM, q, zero)

    qm_ref[0], qm_ref[1] = map_operands(q_ref[0])
    acc_ref[...] = jnp.zeros_like(acc_ref)
    m_ref[...] = jnp.full_like(m_ref, -jnp.inf)

    ones_rows = jnp.ones((ATT_SUM_ROWS, tk), BF16)

    def scores(j, s_ref, slot, qms=None):
        k_blk = k_ref[0, pl.ds(pl.multiple_of(j * tk, tk), tk), :]
        for mp in range(2):
            qm = qm_ref[mp] if qms is None else qms[mp]
            st = lax.dot_general(k_blk, qm, (((1,), (1,)), ((), ())),
                                 preferred_element_type=F32)
            s_ref[mp] = st
            mblk_ref[slot, mp] = jnp.max(st, axis=0, keepdims=True)

    def softmax_pv(j, s_ref, slot, masked):
        vt_blk = jnp.concatenate([vt_ref[0, j], ones_rows], axis=0)
        if masked:
            key = lax.broadcasted_iota(jnp.int32, (tk, tq), 0)
            qry = lax.broadcasted_iota(jnp.int32, (tk, tq), 1)
            keep = key <= qry
        for mp in range(2):
            st = s_ref[mp]
            if masked:
                st = jnp.where(keep, st, -jnp.inf)
                m_blk = jnp.max(st, axis=0, keepdims=True)
            else:
                m_blk = mblk_ref[slot, mp]
            m_old = m_ref[mp]
            m_new = jnp.maximum(m_old, m_blk)
            alpha = jnp.exp2(m_old - m_new)
            pexp = jnp.exp2(st - m_new).astype(BF16)
            acc_ref[mp] = alpha * acc_ref[mp] + jnp.dot(vt_blk, pexp,
                                                       preferred_element_type=F32)
            m_ref[mp] = m_new

    def run(bufs):
        @pl.when(i == 0)
        def _():
            scores(0, *bufs[0])

        def advance(first, count):
            for u in range(count):
                scores(first + u + 1, *bufs[(u + 1) % 2])
                softmax_pv(first + u, *bufs[u % 2], False)

        def group(jj, carry):
            advance(ATT_UNROLL * jj, ATT_UNROLL)
            return carry

        lax.fori_loop(0, i // ATT_UNROLL, group, 0)

        for rest in range(ATT_UNROLL):
            @pl.when(i % ATT_UNROLL == rest)
            def _(rest=rest):
                advance(i - rest, rest)
                scores(0, *bufs[(rest + 1) % 2], map_operands(qn_ref[0]))
                softmax_pv(i, *bufs[rest % 2], True)

    start = ((i + 1) // 2) % 2
    buf0, buf1 = (s0_ref, 0), (s1_ref, 1)

    @pl.when(start == 0)
    def _():
        run((buf0, buf1))

    @pl.when(start == 1)
    def _():
        run((buf1, buf0))

    lp = lam_ref[...]
    lam = (jnp.exp(jnp.sum(lp[0:1, :] * lp[1:2, :], axis=-1, keepdims=True))
           - jnp.exp(jnp.sum(lp[2:3, :] * lp[3:4, :], axis=-1, keepdims=True))
           + lambda_init)
    dv = DA_V_DIM
    o_t = (acc_ref[0, 0:dv, :] / acc_ref[0, dv:dv + 1, :]
           - lam * (acc_ref[1, 0:dv, :] / acc_ref[1, dv:dv + 1, :]))
    ms = jnp.mean(o_t * o_t, axis=0, keepdims=True)
    y_t = o_t * lax.rsqrt(ms + LN_EPS) * (ng_ref[...] * (1.0 - lambda_init))
    out_ref[0] = y_t.T.astype(BF16)


def _diff_attention(dq, dk, dvt, lam_p, ng_col, lambda_init):
    bsz, s, _ = dq.shape
    tq = ATT_Q
    nk, tk = dvt.shape[1], dvt.shape[3]
    assert tq == tk and nk * tk == s
    nq = s // tq
    return pl.pallas_call(
        functools.partial(_attn_kernel, lambda_init=lambda_init),
        grid=(bsz, DA_HEADS, nq),
        in_specs=[
            pl.BlockSpec((1, tq, 2 * DA_QK_DIM), lambda b, h, i: (b, i, h)),
            pl.BlockSpec((1, tq, 2 * DA_QK_DIM),
                         lambda b, h, i: (b, jnp.minimum(i + 1, nq - 1), h)),
            pl.BlockSpec((1, s, 2 * DA_QK_DIM), lambda b, h, i: (b, 0, h)),
            pl.BlockSpec((1, nk, DA_V_DIM, tk), lambda b, h, i: (b, 0, h, 0)),
            pl.BlockSpec(lam_p.shape, lambda b, h, i: (0, 0)),
            pl.BlockSpec(ng_col.shape, lambda b, h, i: (0, 0)),
        ],
        out_specs=pl.BlockSpec((1, tq, DA_V_DIM), lambda b, h, i: (b, i, h)),
        out_shape=jax.ShapeDtypeStruct((bsz, s, DA_HEADS * DA_V_DIM), BF16),
        scratch_shapes=[pltpu.VMEM((2, tq, 2 * DA_QK_DIM), BF16),
                        pltpu.VMEM((2, DA_V_DIM + ATT_SUM_ROWS, tq), F32),
                        pltpu.VMEM((2, tk, tq), F32),
                        pltpu.VMEM((2, tk, tq), F32),
                        pltpu.VMEM((2, 2, 1, tq), F32),
                        pltpu.VMEM((2, 1, tq), F32)],
        compiler_params=_params("parallel", "parallel", "arbitrary"),
        name="diff_attention",
    )(dq, dq, dk, dvt, lam_p, ng_col)


def _out_mlp_kernel(yc_ref, ym_ref, yd_ref, x_ref, wo_ref, g1_ref, b1_ref, wu_ref, wd_ref,
                    g2_ref, b2_ref, out_ref, x1_ref, xb_ref, acc_ref):
    f = pl.program_id(1)
    last = pl.num_programs(1) - 1
    wc = yc_ref.shape[1]
    wm = ym_ref.shape[1]

    def step(first, final):
        for r in range(0, x_ref.shape[0], MLP_SUB):
            rows = pl.ds(r, MLP_SUB)
            if first:
                h = jnp.dot(yc_ref[rows, :], wo_ref[0:wc, :], preferred_element_type=F32)
                h = h + jnp.dot(ym_ref[rows, :], wo_ref[wc:wc + wm, :],
                                preferred_element_type=F32)
                h = h + jnp.dot(yd_ref[rows, :], wo_ref[wc + wm:, :],
                                preferred_element_type=F32)
                x1 = _layer_norm_rows(DEEPNORM_ALPHA * x_ref[rows, :] + h,
                                      g1_ref[...], b1_ref[...])
                x1_ref[rows, :] = x1
                xb = x1.astype(BF16)
                xb_ref[rows, :] = xb
            else:
                xb = xb_ref[rows, :]
            up = jnp.maximum(jnp.dot(xb, wu_ref[...], preferred_element_type=F32), 0.0)
            down = jnp.dot((up * up).astype(BF16), wd_ref[...], preferred_element_type=F32)
            acc = down if first else acc_ref[rows, :] + down
            if final:
                out_ref[rows, :] = _layer_norm_rows(DEEPNORM_ALPHA * x1_ref[rows, :] + acc,
                                                    g2_ref[...], b2_ref[...])
            else:
                acc_ref[rows, :] = acc

    @pl.when(f == 0)
    def _():
        step(True, False)

    @pl.when(jnp.logical_and(f > 0, f < last))
    def _():
        step(False, False)

    @pl.when(f == last)
    def _():
        step(False, True)


def _out_mlp(yc, ym, yd, x2, wo, g1, b1, wu, wd, g2, b2):
    n, d = x2.shape
    dff = wu.shape[1]
    tm, tf = MLP_ROWS, MLP_FF
    assert dff // tf >= 2, "the first and last hidden steps are distinct code paths"
    rows = lambda i, f: (i, 0)
    const = lambda i, f: (0, 0)
    return pl.pallas_call(
        _out_mlp_kernel,
        grid=(n // tm, dff // tf),
        in_specs=[pl.BlockSpec((tm, yc.shape[1]), rows),
                  pl.BlockSpec((tm, ym.shape[1]), rows),
                  pl.BlockSpec((tm, yd.shape[1]), rows),
                  pl.BlockSpec((tm, d), rows),
                  pl.BlockSpec(wo.shape, const),
                  pl.BlockSpec(g1.shape, const),
                  pl.BlockSpec(b1.shape, const),
                  pl.BlockSpec((d, tf), lambda i, f: (0, f)),
                  pl.BlockSpec((tf, d), lambda i, f: (f, 0)),
                  pl.BlockSpec(g2.shape, const),
                  pl.BlockSpec(b2.shape, const)],
        out_specs=pl.BlockSpec((tm, d), rows),
        out_shape=jax.ShapeDtypeStruct((n, d), F32),
        scratch_shapes=[pltpu.VMEM((tm, d), F32), pltpu.VMEM((tm, d), BF16),
                        pltpu.VMEM((tm, d), F32)],
        compiler_params=_params("parallel", "arbitrary"),
        name="out_mlp_ln",
    )(yc, ym, yd, x2, wo, g1, b1, wu, wd, g2, b2)


def _row(v):
    return v.reshape(1, -1).astype(F32)


def _pad_rows(w, rows):
    return jnp.pad(w.astype(F32), ((0, rows - w.shape[0]), (0, 0)))


def _rearranged_w_in(w):
    d = w.shape[0]
    gate_lo = 256 + 256 + 512 + 256 + 256
    gate_hi = gate_lo + 2 * ML_HEADS
    v_lo = gate_hi + 2 * DA_HEADS * 2 * DA_QK_DIM
    pad = jnp.zeros((d, LANES - 2 * ML_HEADS), w.dtype)
    w_r = jnp.concatenate([w[:, :gate_lo], w[:, gate_hi:v_lo], w[:, gate_lo:gate_hi], pad], axis=1)
    return w_r.astype(BF16), w[:, v_lo:].T.astype(BF16)


def kernel(x, w_in, b_igate, b_fgate, conv_dw_w, conv_dw_b, conv_ln_g, conv_ln_b, conv_pw_w, conv_pw_b, ml_conv_w, ml_conv_b, ml_norm_g, lam_q1, lam_k1, lam_q2, lam_k2, da_norm_g, w_out, ln1_g, ln1_b, w_up, w_down, ln2_g, ln2_b):
    bsz, s, d = x.shape
    n = bsz * s
    x2 = x.reshape(n, d)
    for l in range(DEPTH):
        lambda_init = 0.8 - 0.6 * math.exp(-0.3 * l)
        w_r, w_vt = _rearranged_w_in(w_in[l])
        glu, mqk, mv, mo, gates, dq, dk, dvt = _in_proj(x2, w_r, w_vt, bsz)
        seq = lambda a: a.reshape(bsz, s, a.shape[1])

        y_conv = _conv_module(seq(glu), _pad_rows(conv_dw_w[l], 32), _row(conv_dw_b[l]),
                              _row(conv_ln_g[l]), _row(conv_ln_b[l]),
                              conv_pw_w[l].astype(BF16), _row(conv_pw_b[l]))

        gate_bias = jnp.pad(jnp.concatenate([b_igate[l], b_fgate[l]]).astype(F32),
                            (0, LANES - 2 * ML_HEADS)).reshape(1, LANES)
        y_ml = _mlstm(seq(mqk), seq(mv), seq(mo), seq(gates),
                      _pad_rows(ml_conv_w[l], SUBLANES), _row(ml_conv_b[l]),
                      gate_bias, _row(ml_norm_g[l]))

        lam_p = jnp.stack([lam_q1[l], lam_k1[l], lam_q2[l], lam_k2[l]]).astype(F32)
        y_da = _diff_attention(seq(dq), seq(dk), dvt, lam_p,
                               da_norm_g[l].astype(F32).reshape(-1, 1), lambda_init)

        x2 = _out_mlp(y_conv.reshape(n, -1), y_ml.reshape(n, -1), y_da.reshape(n, -1), x2,
                      w_out[l].astype(BF16), _row(ln1_g[l]), _row(ln1_b[l]),
                      w_up[l].astype(BF16), w_down[l].astype(BF16),
                      _row(ln2_g[l]), _row(ln2_b[l]))
    return x2.reshape(bsz, s, d)
```

```python
import functools
import math

import jax
import jax.numpy as jnp
from jax import lax
from jax.experimental import pallas as pl
from jax.experimental.pallas import tpu as pltpu

F32 = jnp.float32
BF16 = jnp.bfloat16

LANES = 128
SUBLANES = 8
VMEM_LIMIT_BYTES = 56 * 1024 * 1024

DEPTH = 2
CONV_KSIZE = 31
ML_HEADS = 4
ML_HEAD_DIM = 64
ML_QK_CONV = 4
ML_CHUNK = 128
DA_HEADS = 4
DA_QK_DIM = 64
DA_V_DIM = 128
LN_EPS = 1e-5
DEEPNORM_ALPHA = (2 * DEPTH) ** 0.25

PROJ_ROWS = 1024
CONV_ROWS = 1024
CONV_HALO = 32
CONV_SUB = 64
ATT_Q = 512
ATT_K = 512
ML_BATCH = 4
ATT_UNROLL = 4
ATT_SUM_ROWS = 16
MLP_ROWS = 1024
MLP_FF = 1024
MLP_SUB = 256


def _params(*semantics):
    return pltpu.CompilerParams(dimension_semantics=semantics,
                                vmem_limit_bytes=VMEM_LIMIT_BYTES)


def _sigmoid(x):
    return 1.0 / (1.0 + jnp.exp(-x))


def _layer_norm_rows(x, g, b):
    mu = jnp.mean(x, axis=-1, keepdims=True)
    xc = x - mu
    var = jnp.mean(xc * xc, axis=-1, keepdims=True)
    return xc * lax.rsqrt(var + LN_EPS) * g + b


def _in_proj_kernel(x_ref, w_ref, wvt_ref, glu_ref, mqk_ref, mv_ref, mo_ref, gate_ref,
                    dq_ref, dk_ref, dvt_ref):
    xb = x_ref[...].astype(BF16)

    def sec(lo, width):
        return jnp.dot(xb, w_ref[:, lo:lo + width], preferred_element_type=F32)

    glu_ref[...] = sec(0, 256) * _sigmoid(sec(256, 256))
    mqk_ref[...] = sec(512, 512)
    mv_ref[...] = sec(1024, 256)
    mo_ref[...] = sec(1280, 256)
    dq_ref[...] = (sec(1536, 512) * (DA_QK_DIM ** -0.5 * math.log2(math.e))).astype(BF16)
    dk_ref[...] = sec(2048, 512).astype(BF16)
    gate_ref[...] = sec(2560, LANES)
    vt = lax.dot_general(wvt_ref[...], xb, (((1,), (1,)), ((), ())),
                         preferred_element_type=F32).astype(BF16)
    for kb in range(dvt_ref.shape[1]):
        dvt_ref[0, kb] = vt[:, kb * ATT_K:(kb + 1) * ATT_K]


def _in_proj(x2, w_r, w_vt, bsz):
    n, d = x2.shape
    tm = PROJ_ROWS
    kb_per_tile = tm // ATT_K
    tiles_per_seq = n // bsz // tm
    widths = (256, 512, 256, 256, LANES, 512, 512)
    dtypes = (F32, F32, F32, F32, F32, BF16, BF16)
    wv = w_vt.shape[0]
    return pl.pallas_call(
        _in_proj_kernel,
        grid=(n // tm,),
        in_specs=[pl.BlockSpec((tm, d), lambda i: (i, 0)),
                  pl.BlockSpec(w_r.shape, lambda i: (0, 0)),
                  pl.BlockSpec(w_vt.shape, lambda i: (0, 0))],
        out_specs=[pl.BlockSpec((tm, w), lambda i: (i, 0)) for w in widths]
        + [pl.BlockSpec((1, kb_per_tile, wv, ATT_K),
                        lambda i: (i // tiles_per_seq, i % tiles_per_seq, 0, 0))],
        out_shape=[jax.ShapeDtypeStruct((n, w), dt) for w, dt in zip(widths, dtypes)]
        + [jax.ShapeDtypeStruct((bsz, tiles_per_seq * kb_per_tile, wv, ATT_K), BF16)],
        compiler_params=_params("parallel"),
        name="in_proj",
    )(x2, w_r, w_vt)


def _conv_kernel(x_ref, halo_ref, dw_ref, dwb_ref, g_ref, b_ref, pw_ref, pwb_ref,
                 out_ref, xs, ybuf):
    s = pl.program_id(1)
    rows = x_ref.shape[1]
    xs[0, 0:CONV_HALO, :] = jnp.where(s > 0, halo_ref[0], 0.0)
    xs[0, CONV_HALO:CONV_HALO + rows, :] = x_ref[0]
    span = CONV_HALO + rows - SUBLANES
    for k in range(1, SUBLANES):
        xs[k, 0:span, :] = xs[0, k:k + span, :]
    first = CONV_HALO - (CONV_KSIZE - 1)
    for r in range(0, rows, CONV_SUB):
        acc = jnp.broadcast_to(dwb_ref[...], (CONV_SUB, x_ref.shape[2]))
        for j in range(CONV_KSIZE):
            k = (first + j) % SUBLANES
            lo = r + first + j - k
            acc = acc + dw_ref[j:j + 1, :] * xs[k, lo:lo + CONV_SUB, :]
        y = _layer_norm_rows(acc, g_ref[...], b_ref[...])
        ybuf[r:r + CONV_SUB, :] = (y * _sigmoid(y)).astype(BF16)
    out = jnp.dot(ybuf[...], pw_ref[...], preferred_element_type=F32) + pwb_ref[...]
    out_ref[0] = out.astype(BF16)


def _conv_module(glu, dw, dwb, g, b, pw, pwb):
    bsz, s, ch = glu.shape
    ts = CONV_ROWS
    halo_per_tile = ts // CONV_HALO
    return pl.pallas_call(
        _conv_kernel,
        grid=(bsz, s // ts),
        in_specs=[
            pl.BlockSpec((1, ts, ch), lambda bi, si: (bi, si, 0)),
            pl.BlockSpec((1, CONV_HALO, ch),
                         lambda bi, si: (bi, jnp.maximum(si * halo_per_tile - 1, 0), 0)),
            pl.BlockSpec(dw.shape, lambda bi, si: (0, 0)),
            pl.BlockSpec(dwb.shape, lambda bi, si: (0, 0)),
            pl.BlockSpec(g.shape, lambda bi, si: (0, 0)),
            pl.BlockSpec(b.shape, lambda bi, si: (0, 0)),
            pl.BlockSpec(pw.shape, lambda bi, si: (0, 0)),
            pl.BlockSpec(pwb.shape, lambda bi, si: (0, 0)),
        ],
        out_specs=pl.BlockSpec((1, ts, ch), lambda bi, si: (bi, si, 0)),
        out_shape=jax.ShapeDtypeStruct((bsz, s, ch), BF16),
        scratch_shapes=[pltpu.VMEM((SUBLANES, CONV_HALO + ts, ch), F32),
                        pltpu.VMEM((ts, ch), BF16)],
        compiler_params=_params("parallel", "parallel"),
        name="conv_module",
    )(glu, glu, dw, dwb, g, b, pw, pwb)


def _mlstm_kernel(qk_ref, halo_ref, v_ref, o_ref, gate_ref, cw_ref, cb_ref, gb_ref,
                  ng_ref, sel_ref, avg_ref, out_ref, qpad, ct_ref, m_ref):
    c = pl.program_id(1)
    nb = qk_ref.shape[0]
    t = ML_CHUNK
    w_pair = 2 * ML_HEAD_DIM
    k_off = ML_HEADS * ML_HEAD_DIM
    n_pairs = ML_HEADS // 2
    nt = (((1,), (1,)), ((), ()))
    tn = (((0,), (0,)), ((), ()))

    @pl.when(c == 0)
    def _():
        ct_ref[...] = jnp.zeros_like(ct_ref)
        m_ref[...] = jnp.zeros_like(m_ref)

    row = lax.broadcasted_iota(jnp.int32, (t, t), 0)
    col = lax.broadcasted_iota(jnp.int32, (t, t), 1)
    causal = row >= col
    tri = causal.astype(F32)
    lane = lax.broadcasted_iota(jnp.int32, (t, w_pair), 1)
    low = lane < ML_HEAD_DIM
    owns = (low, lane >= ML_HEAD_DIM)

    first = SUBLANES - (ML_QK_CONV - 1)
    qk, gpre_t, log_f_t = [], [], []
    for bb in range(nb):
        qpad[bb, 0:SUBLANES, :] = jnp.where(c > 0, halo_ref[bb], 0.0)
        qpad[bb, SUBLANES:SUBLANES + t, :] = qk_ref[bb]
        acc = jnp.broadcast_to(cb_ref[...], (t, qk_ref.shape[2]))
        for j in range(ML_QK_CONV):
            acc = acc + cw_ref[j:j + 1, :] * qpad[bb, first + j:first + j + t, :]
        qk.append(acc * _sigmoid(acc))
        g_t = (gate_ref[bb] + gb_ref[...]).T[0:SUBLANES, :]
        gpre_t.append(g_t)
        log_f_t.append(jnp.minimum(g_t, 0.0) - jnp.log(1.0 + jnp.exp(-jnp.abs(g_t))))

    tri_t = (row <= col).astype(F32)
    cums = [jnp.dot(lf, tri_t, preferred_element_type=F32, precision=lax.Precision.HIGHEST)
            for lf in log_f_t]

    items = [(bb, h) for bb in range(nb) for h in range(ML_HEADS)]
    st = {}
    for bb in range(nb):
        for pair in range(n_pairs):
            lo = pair * w_pair
            k_pair = qk[bb][:, k_off + lo:k_off + lo + w_pair] * (ML_HEAD_DIM ** -0.5)
            st[bb, pair, "q"] = qk[bb][:, lo:lo + w_pair]
            st[bb, pair, "k"] = k_pair
            st[bb, pair, "kbt"] = k_pair.T.astype(BF16)
            st[bb, pair, "v"] = v_ref[bb, :, lo:lo + w_pair]
    for (bb, h) in items:
        pair, par = divmod(h, 2)
        st[bb, h, "q_m"] = jnp.where(owns[par], st[bb, pair, "q"], 0.0).astype(BF16)
        st[bb, h, "vx"] = jnp.where(owns[par], st[bb, pair, "v"], 1.0).astype(BF16)
    for (bb, h) in items:
        st[bb, h, "s_qk"] = jnp.dot(st[bb, h, "q_m"], st[bb, h // 2, "kbt"],
                                    preferred_element_type=F32)
    for (bb, h) in items:
        st[bb, h, "ct_prev"] = ct_ref[bb, h]
        st[bb, h, "inter"] = jnp.dot(st[bb, h, "q_m"], st[bb, h, "ct_prev"].astype(BF16),
                                     preferred_element_type=F32)

    head_row = lax.broadcasted_iota(jnp.int32, (SUBLANES, t), 0) < ML_HEADS
    lane_t = lax.broadcasted_iota(jnp.int32, (SUBLANES, t), 1)
    def last_lane(x):
        return jnp.broadcast_to(x[:, t - 1:t], x.shape)

    r_t, a_st_t, e_st_t, g_hl = [], [], [], []
    for bb in range(nb):
        b_t = pltpu.roll(cums[bb], ML_HEADS, 0)
        b_t = jnp.where(head_row, b_t, 0.0)
        r = jnp.where(head_row, gpre_t[bb] - b_t, 0.0)
        cm = r
        shift = 1
        while shift < t:
            cm = jnp.where(lane_t >= shift, jnp.maximum(cm, pltpu.roll(cm, shift, 1)), cm)
            shift *= 2
        m_prev = m_ref[bb]
        u = jnp.maximum(m_prev, cm)
        a_inter = jnp.exp(m_prev - u)
        floor = jnp.exp(-(b_t + u))
        g_end = last_lane(b_t)
        cm_end = last_lane(cm)
        p_end = jnp.exp(r - cm_end)
        m_loc = g_end + cm_end
        m_new = jnp.maximum(g_end + m_prev, m_loc)
        a_st_t.append(jnp.exp(g_end + m_prev - m_new))
        e_st_t.append(jnp.exp(m_loc - m_new))
        m_ref[bb] = jnp.where(head_row, m_new, 0.0)
        r_t.append(r)
        parts = []
        for tile in (u, a_inter, floor, p_end):
            hi = tile.astype(BF16)
            parts += [hi, (tile - hi.astype(F32)).astype(BF16)]
        g_hl.append(jnp.concatenate(parts, axis=0))
    rep = {(bb, h): lax.dot_general(g_hl[bb], sel_ref[h], tn, preferred_element_type=F32)
           for bb in range(nb) for h in range(ML_HEADS)}

    def tile_rows(x_row):
        return jnp.tile(jnp.broadcast_to(x_row, (SUBLANES, LANES)), (t // SUBLANES, 1))

    for (bb, h) in items:
        pair, par = divmod(h, 2)
        u_col = rep[bb, h][:, 0:LANES]
        st[bb, h, "a_inter"] = rep[bb, h][:, LANES:2 * LANES]
        st[bb, h, "floor"] = rep[bb, h][:, 2 * LANES:3 * LANES]
        p_end = rep[bb, h][:, 3 * LANES:4 * LANES]
        r_row = tile_rows(r_t[bb][h:h + 1, :])
        st[bb, h, "dexp"] = jnp.exp(jnp.where(causal, r_row - u_col, -jnp.inf))
        kp = (jnp.where(owns[par], st[bb, pair, "k"], 0.0) * p_end).astype(BF16)
        c_loc = lax.dot_general(kp, st[bb, h, "vx"], tn, preferred_element_type=F32)
        ct_ref[bb, h] = (tile_rows(a_st_t[bb][h:h + 1, :]) * st[bb, h, "ct_prev"]
                         + tile_rows(e_st_t[bb][h:h + 1, :]) * c_loc)

    for (bb, h) in items:
        sc = (st[bb, h, "s_qk"] * st[bb, h, "dexp"]).astype(BF16)
        st[bb, h, "intra"] = jnp.dot(sc, st[bb, h, "vx"], preferred_element_type=F32)
    def half_mean(x):
        hi = x.astype(BF16)
        lo = (x - hi.astype(F32)).astype(BF16)
        return jnp.dot(jnp.concatenate([hi, lo], axis=1), avg_ref[...],
                       preferred_element_type=F32)

    for bb in range(nb):
        for pair in range(n_pairs):
            lo = pair * w_pair
            h0, h1 = 2 * pair, 2 * pair + 1
            nd0 = st[bb, h0, "a_inter"] * st[bb, h0, "inter"] + st[bb, h0, "intra"]
            nd1 = st[bb, h1, "a_inter"] * st[bb, h1, "inter"] + st[bb, h1, "intra"]
            den = pltpu.roll(jnp.where(low, nd1, nd0), ML_HEAD_DIM, 1)
            floor = jnp.where(low, st[bb, h0, "floor"], st[bb, h1, "floor"])
            h_pair = jnp.where(low, nd0, nd1) / jnp.maximum(jnp.abs(den), floor)
            x = h_pair * _sigmoid(o_ref[bb, :, lo:lo + w_pair])
            xc = x - half_mean(x)
            var = half_mean(xc * xc)
            y = xc * lax.rsqrt(var + LN_EPS) * ng_ref[:, lo:lo + w_pair]
            out_ref[bb, :, lo:lo + w_pair] = y.astype(BF16)


def _mlstm(mqk, mv, mo, gates, cw, cb, gb, ng):
    bsz, s, wqk = mqk.shape
    wv = mv.shape[2]
    t = ML_CHUNK
    nb = math.gcd(bsz, ML_BATCH)
    halo_per_chunk = t // SUBLANES
    n_rep = 4
    src = jnp.arange(n_rep * 2 * SUBLANES)[:, None]
    dst = jnp.arange(n_rep * LANES)[None, :] // LANES
    sel = jnp.stack([((src // (2 * SUBLANES) == dst) & (src % SUBLANES == h)).astype(BF16)
                     for h in range(ML_HEADS)])
    grp = lambda idx: (idx % LANES) // ML_HEAD_DIM
    avg = jnp.where(grp(jnp.arange(2 * LANES))[:, None] == grp(jnp.arange(LANES))[None, :],
                    1.0 / ML_HEAD_DIM, 0.0).astype(BF16)
    const = lambda bi, ci: (0, 0)
    return pl.pallas_call(
        _mlstm_kernel,
        grid=(bsz // nb, s // t),
        in_specs=[
            pl.BlockSpec((nb, t, wqk), lambda bi, ci: (bi, ci, 0)),
            pl.BlockSpec((nb, SUBLANES, wqk),
                         lambda bi, ci: (bi, jnp.maximum(ci * halo_per_chunk - 1, 0), 0)),
            pl.BlockSpec((nb, t, wv), lambda bi, ci: (bi, ci, 0)),
            pl.BlockSpec((nb, t, wv), lambda bi, ci: (bi, ci, 0)),
            pl.BlockSpec((nb, t, LANES), lambda bi, ci: (bi, ci, 0)),
            pl.BlockSpec(cw.shape, const),
            pl.BlockSpec(cb.shape, const),
            pl.BlockSpec(gb.shape, const),
            pl.BlockSpec(ng.shape, const),
            pl.BlockSpec(sel.shape, lambda bi, ci: (0, 0, 0)),
            pl.BlockSpec(avg.shape, const),
        ],
        out_specs=pl.BlockSpec((nb, t, wv), lambda bi, ci: (bi, ci, 0)),
        out_shape=jax.ShapeDtypeStruct((bsz, s, wv), BF16),
        scratch_shapes=[pltpu.VMEM((nb, SUBLANES + t, wqk), F32),
                        pltpu.VMEM((nb, ML_HEADS, 2 * ML_HEAD_DIM, LANES), F32),
                        pltpu.VMEM((nb, SUBLANES, LANES), F32)],
        compiler_params=_params("parallel", "arbitrary"),
        name="mlstm",
    )(mqk, mqk, mv, mo, gates, cw, cb, gb, ng, sel, avg)


def _attn_kernel(q_ref, qn_ref, k_ref, vt_ref, lam_ref, ng_ref, out_ref, qm_ref, acc_ref,
                 s0_ref, s1_ref, mblk_ref, m_ref, *, lambda_init):
    i = pl.program_id(2)
    tq = q_ref.shape[1]
    tk = vt_ref.shape[3]
    lane = lax.broadcasted_iota(jnp.int32, (tq, 2 * DA_QK_DIM), 1)

    def map_operands(q):
        zero = jnp.zeros_like(q)
        return jnp.where(lane < DA_QK_DIM, q, zero), jnp.where(lane >= DA_QK_DIM, q, zero)

    qm_ref[0], qm_ref[1] = map_operands(q_ref[0])
    acc_ref[...] = jnp.zeros_like(acc_ref)
    m_ref[...] = jnp.full_like(m_ref, -jnp.inf)

    ones_rows = jnp.ones((ATT_SUM_ROWS, tk), BF16)

    def scores(j, s_ref, slot, qms=None):
        k_blk = k_ref[0, pl.ds(pl.multiple_of(j * tk, tk), tk), :]
        for mp in range(2):
            qm = qm_ref[mp] if qms is None else qms[mp]
            st = lax.dot_general(k_blk, qm, (((1,), (1,)), ((), ())),
                                 preferred_element_type=F32)
            s_ref[mp] = st
            mblk_ref[slot, mp] = jnp.max(st, axis=0, keepdims=True)

    def softmax_pv(j, s_ref, slot, masked):
        vt_blk = jnp.concatenate([vt_ref[0, j], ones_rows], axis=0)
        if masked:
            key = lax.broadcasted_iota(jnp.int32, (tk, tq), 0)
            qry = lax.broadcasted_iota(jnp.int32, (tk, tq), 1)
            keep = key <= qry
        for mp in range(2):
            st = s_ref[mp]
            if masked:
                st = jnp.where(keep, st, -jnp.inf)
                m_blk = jnp.max(st, axis=0, keepdims=True)
            else:
                m_blk = mblk_ref[slot, mp]
            m_old = m_ref[mp]
            m_new = jnp.maximum(m_old, m_blk)
            alpha = jnp.exp2(m_old - m_new)
            pexp = jnp.exp2(st - m_new).astype(BF16)
            acc_ref[mp] = alpha * acc_ref[mp] + jnp.dot(vt_blk, pexp,
                                                       preferred_element_type=F32)
            m_ref[mp] = m_new

    def run(bufs):
        @pl.when(i == 0)
        def _():
            scores(0, *bufs[0])

        def advance(first, count):
            for u in range(count):
                scores(first + u + 1, *bufs[(u + 1) % 2])
                softmax_pv(first + u, *bufs[u % 2], False)

        def group(jj, carry):
            advance(ATT_UNROLL * jj, ATT_UNROLL)
            return carry

        lax.fori_loop(0, i // ATT_UNROLL, group, 0)

        for rest in range(ATT_UNROLL):
            @pl.when(i % ATT_UNROLL == rest)
            def _(rest=rest):
                advance(i - rest, rest)
                scores(0, *bufs[(rest + 1) % 2], map_operands(qn_ref[0]))
                softmax_pv(i, *bufs[rest % 2], True)

    start = ((i + 1) // 2) % 2
    buf0, buf1 = (s0_ref, 0), (s1_ref, 1)

    @pl.when(start == 0)
    def _():
        run((buf0, buf1))

    @pl.when(start == 1)
    def _():
        run((buf1, buf0))

    lp = lam_ref[...]
    lam = (jnp.exp(jnp.sum(lp[0:1, :] * lp[1:2, :], axis=-1, keepdims=True))
           - jnp.exp(jnp.sum(lp[2:3, :] * lp[3:4, :], axis=-1, keepdims=True))
           + lambda_init)
    dv = DA_V_DIM
    o_t = (acc_ref[0, 0:dv, :] / acc_ref[0, dv:dv + 1, :]
           - lam * (acc_ref[1, 0:dv, :] / acc_ref[1, dv:dv + 1, :]))
    ms = jnp.mean(o_t * o_t, axis=0, keepdims=True)
    y_t = o_t * lax.rsqrt(ms + LN_EPS) * (ng_ref[...] * (1.0 - lambda_init))
    out_ref[0] = y_t.T.astype(BF16)


def _diff_attention(dq, dk, dvt, lam_p, ng_col, lambda_init):
    bsz, s, _ = dq.shape
    tq = ATT_Q
    nk, tk = dvt.shape[1], dvt.shape[3]
    assert tq == tk and nk * tk == s
    nq = s // tq
    return pl.pallas_call(
        functools.partial(_attn_kernel, lambda_init=lambda_init),
        grid=(bsz, DA_HEADS, nq),
        in_specs=[
            pl.BlockSpec((1, tq, 2 * DA_QK_DIM), lambda b, h, i: (b, i, h)),
            pl.BlockSpec((1, tq, 2 * DA_QK_DIM),
                         lambda b, h, i: (b, jnp.minimum(i + 1, nq - 1), h)),
            pl.BlockSpec((1, s, 2 * DA_QK_DIM), lambda b, h, i: (b, 0, h)),
            pl.BlockSpec((1, nk, DA_V_DIM, tk), lambda b, h, i: (b, 0, h, 0)),
            pl.BlockSpec(lam_p.shape, lambda b, h, i: (0, 0)),
            pl.BlockSpec(ng_col.shape, lambda b, h, i: (0, 0)),
        ],
        out_specs=pl.BlockSpec((1, tq, DA_V_DIM), lambda b, h, i: (b, i, h)),
        out_shape=jax.ShapeDtypeStruct((bsz, s, DA_HEADS * DA_V_DIM), BF16),
        scratch_shapes=[pltpu.VMEM((2, tq, 2 * DA_QK_DIM), BF16),
                        pltpu.VMEM((2, DA_V_DIM + ATT_SUM_ROWS, tq), F32),
                        pltpu.VMEM((2, tk, tq), F32),
                        pltpu.VMEM((2, tk, tq), F32),
                        pltpu.VMEM((2, 2, 1, tq), F32),
                        pltpu.VMEM((2, 1, tq), F32)],
        compiler_params=_params("parallel", "parallel", "arbitrary"),
        name="diff_attention",
    )(dq, dq, dk, dvt, lam_p, ng_col)


def _out_mlp_kernel(yc_ref, ym_ref, yd_ref, x_ref, wo_ref, g1_ref, b1_ref, wu_ref, wd_ref,
                    g2_ref, b2_ref, out_ref, x1_ref, xb_ref, acc_ref):
    f = pl.program_id(1)
    last = pl.num_programs(1) - 1
    wc = yc_ref.shape[1]
    wm = ym_ref.shape[1]

    def mix(r):
        rows = pl.ds(r, MLP_SUB)
        h = jnp.dot(yc_ref[rows, :], wo_ref[0:wc, :], preferred_element_type=F32)
        h = h + jnp.dot(ym_ref[rows, :], wo_ref[wc:wc + wm, :], preferred_element_type=F32)
        return h + jnp.dot(yd_ref[rows, :], wo_ref[wc + wm:, :], preferred_element_type=F32)

    def step(first, final):
        n_rows = x_ref.shape[0]
        if first:
            h_next = mix(0)
        for r in range(0, n_rows, MLP_SUB):
            rows = pl.ds(r, MLP_SUB)
            if first:
                h = h_next
                if r + MLP_SUB < n_rows:
                    h_next = mix(r + MLP_SUB)
                x1 = _layer_norm_rows(DEEPNORM_ALPHA * x_ref[rows, :] + h,
                                      g1_ref[...], b1_ref[...])
                x1_ref[rows, :] = x1
                xb = x1.astype(BF16)
                xb_ref[rows, :] = xb
            else:
                xb = xb_ref[rows, :]
            up = jnp.maximum(jnp.dot(xb, wu_ref[...], preferred_element_type=F32), 0.0)
            down = jnp.dot((up * up).astype(BF16), wd_ref[...], preferred_element_type=F32)
            acc = down if first else acc_ref[rows, :] + down
            if final:
                out_ref[rows, :] = _layer_norm_rows(DEEPNORM_ALPHA * x1_ref[rows, :] + acc,
                                                    g2_ref[...], b2_ref[...])
            else:
                acc_ref[rows, :] = acc

    @pl.when(f == 0)
    def _():
        step(True, False)

    @pl.when(jnp.logical_and(f > 0, f < last))
    def _():
        step(False, False)

    @pl.when(f == last)
    def _():
        step(False, True)


def _out_mlp(yc, ym, yd, x2, wo, g1, b1, wu, wd, g2, b2):
    n, d = x2.shape
    dff = wu.shape[1]
    tm, tf = MLP_ROWS, MLP_FF
    assert dff // tf >= 2, "the first and last hidden steps are distinct code paths"
    rows = lambda i, f: (i, 0)
    const = lambda i, f: (0, 0)
    return pl.pallas_call(
        _out_mlp_kernel,
        grid=(n // tm, dff // tf),
        in_specs=[pl.BlockSpec((tm, yc.shape[1]), rows),
                  pl.BlockSpec((tm, ym.shape[1]), rows),
                  pl.BlockSpec((tm, yd.shape[1]), rows),
                  pl.BlockSpec((tm, d), rows),
                  pl.BlockSpec(wo.shape, const),
                  pl.BlockSpec(g1.shape, const),
                  pl.BlockSpec(b1.shape, const),
                  pl.BlockSpec((d, tf), lambda i, f: (0, f)),
                  pl.BlockSpec((tf, d), lambda i, f: (f, 0)),
                  pl.BlockSpec(g2.shape, const),
                  pl.BlockSpec(b2.shape, const)],
        out_specs=pl.BlockSpec((tm, d), rows),
        out_shape=jax.ShapeDtypeStruct((n, d), F32),
        scratch_shapes=[pltpu.VMEM((tm, d), F32), pltpu.VMEM((tm, d), BF16),
                        pltpu.VMEM((tm, d), F32)],
        compiler_params=_params("parallel", "arbitrary"),
        name="out_mlp_ln",
    )(yc, ym, yd, x2, wo, g1, b1, wu, wd, g2, b2)


def _row(v):
    return v.reshape(1, -1).astype(F32)


def _pad_rows(w, rows):
    return jnp.pad(w.astype(F32), ((0, rows - w.shape[0]), (0, 0)))


def _rearranged_w_in(w):
    d = w.shape[0]
    gate_lo = 256 + 256 + 512 + 256 + 256
    gate_hi = gate_lo + 2 * ML_HEADS
    v_lo = gate_hi + 2 * DA_HEADS * 2 * DA_QK_DIM
    pad = jnp.zeros((d, LANES - 2 * ML_HEADS), w.dtype)
    w_r = jnp.concatenate([w[:, :gate_lo], w[:, gate_hi:v_lo], w[:, gate_lo:gate_hi], pad], axis=1)
    return w_r.astype(BF16), w[:, v_lo:].T.astype(BF16)


def kernel(x, w_in, b_igate, b_fgate, conv_dw_w, conv_dw_b, conv_ln_g, conv_ln_b, conv_pw_w, conv_pw_b, ml_conv_w, ml_conv_b, ml_norm_g, lam_q1, lam_k1, lam_q2, lam_k2, da_norm_g, w_out, ln1_g, ln1_b, w_up, w_down, ln2_g, ln2_b):
    bsz, s, d = x.shape
    n = bsz * s
    x2 = x.reshape(n, d)
    for l in range(DEPTH):
        lambda_init = 0.8 - 0.6 * math.exp(-0.3 * l)
        w_r, w_vt = _rearranged_w_in(w_in[l])
        glu, mqk, mv, mo, gates, dq, dk, dvt = _in_proj(x2, w_r, w_vt, bsz)
        seq = lambda a: a.reshape(bsz, s, a.shape[1])

        y_conv = _conv_module(seq(glu), _pad_rows(conv_dw_w[l], 32), _row(conv_dw_b[l]),
                              _row(conv_ln_g[l]), _row(conv_ln_b[l]),
                              conv_pw_w[l].astype(BF16), _row(conv_pw_b[l]))

        gate_bias = jnp.pad(jnp.concatenate([b_igate[l], b_fgate[l]]).astype(F32),
                            (0, LANES - 2 * ML_HEADS)).reshape(1, LANES)
        y_ml = _mlstm(seq(mqk), seq(mv), seq(mo), seq(gates),
                      _pad_rows(ml_conv_w[l], SUBLANES), _row(ml_conv_b[l]),
                      gate_bias, _row(ml_norm_g[l]))

        lam_p = jnp.stack([lam_q1[l], lam_k1[l], lam_q2[l], lam_k2[l]]).astype(F32)
        y_da = _diff_attention(seq(dq), seq(dk), dvt, lam_p,
                               da_norm_g[l].astype(F32).reshape(-1, 1), lambda_init)

        x2 = _out_mlp(y_conv.reshape(n, -1), y_ml.reshape(n, -1), y_da.reshape(n, -1), x2,
                      w_out[l].astype(BF16), _row(ln1_g[l]), _row(ln1_b[l]),
                      w_up[l].astype(BF16), w_down[l].astype(BF16),
                      _row(ln2_g[l]), _row(ln2_b[l]))
    return x2.reshape(bsz, s, d)
```

```python
import functools
import math

import jax
import jax.numpy as jnp
from jax import lax
from jax.experimental import pallas as pl
from jax.experimental.pallas import tpu as pltpu

F32 = jnp.float32
BF16 = jnp.bfloat16

LANES = 128
SUBLANES = 8
VMEM_LIMIT_BYTES = 56 * 1024 * 1024

DEPTH = 2
CONV_KSIZE = 31
ML_HEADS = 4
ML_HEAD_DIM = 64
ML_QK_CONV = 4
ML_CHUNK = 128
DA_HEADS = 4
DA_QK_DIM = 64
DA_V_DIM = 128
LN_EPS = 1e-5
DEEPNORM_ALPHA = (2 * DEPTH) ** 0.25

PROJ_ROWS = 1024
CONV_HALO = 32
CONV_SUB = 64
ATT_Q = 512
ATT_K = 512
ML_BATCH = 4
ATT_UNROLL = 4
ATT_SUM_ROWS = 16
MLP_ROWS = 1024
MLP_FF = 1024
MLP_SUB = 256


def _params(*semantics):
    return pltpu.CompilerParams(dimension_semantics=semantics,
                                vmem_limit_bytes=VMEM_LIMIT_BYTES)


def _sigmoid(x):
    return 1.0 / (1.0 + jnp.exp(-x))


def _layer_norm_rows(x, g, b):
    mu = jnp.mean(x, axis=-1, keepdims=True)
    xc = x - mu
    var = jnp.mean(xc * xc, axis=-1, keepdims=True)
    return xc * lax.rsqrt(var + LN_EPS) * g + b


def _in_proj_kernel(x_ref, w_ref, wvt_ref, dw_ref, dwb_ref, cg_ref, cb_ref, pw_ref, pwb_ref,
                    yconv_ref, mqk_ref, mv_ref, mo_ref, gate_ref, dq_ref, dk_ref, dvt_ref,
                    xs, ybuf, *, tiles_per_seq):
    rows = x_ref.shape[0]
    xb = x_ref[...].astype(BF16)

    def sec(lo, width):
        return jnp.dot(xb, w_ref[:, lo:lo + width], preferred_element_type=F32)

    @pl.when(pl.program_id(0) % tiles_per_seq == 0)
    def _():
        xs[0, 0:CONV_HALO, :] = jnp.zeros((CONV_HALO, xs.shape[2]), F32)

    xs[0, CONV_HALO:CONV_HALO + rows, :] = sec(0, 256) * _sigmoid(sec(256, 256))
    span = CONV_HALO + rows - SUBLANES
    for k in range(1, SUBLANES):
        xs[k, 0:span, :] = xs[0, k:k + span, :]

    def conv_block(r, taps):
        first = CONV_HALO - (CONV_KSIZE - 1)
        acc = jnp.broadcast_to(dwb_ref[...], (CONV_SUB, xs.shape[2]))
        for j in range(CONV_KSIZE):
            k = (first + j) % SUBLANES
            lo = r + first + j - k
            acc = acc + taps[j:j + 1, :] * xs[k, lo:lo + CONV_SUB, :]
        y = _layer_norm_rows(acc, cg_ref[...], cb_ref[...])
        ybuf[r:r + CONV_SUB, :] = (y * _sigmoid(y)).astype(BF16)

    def proj_mqk():
        res = sec(512, 512)
        mqk_ref[...] = res
        return res

    def proj_mv_mo():
        mv_ref[...] = sec(1024, 256)
        res = sec(1280, 256)
        mo_ref[...] = res
        return res

    def proj_dq():
        res = sec(1536, 512)
        dq_ref[...] = (res * (DA_QK_DIM ** -0.5 * math.log2(math.e))).astype(BF16)
        return res

    def proj_dk_gate_vt():
        res = sec(2048, 512)
        dk_ref[...] = res.astype(BF16)
        gate_ref[...] = sec(2560, LANES)
        vt = lax.dot_general(wvt_ref[...], xb, (((1,), (1,)), ((), ())),
                             preferred_element_type=F32).astype(BF16)
        for kb in range(dvt_ref.shape[1]):
            dvt_ref[0, kb] = vt[:, kb * ATT_K:(kb + 1) * ATT_K]
        return res

    blocks_per_stage = rows // CONV_SUB // 4
    stages = [(proj, blocks_per_stage)
              for proj in (proj_mqk, proj_mv_mo, proj_dq, proj_dk_gate_vt)]

    def exact_zero(x):
        bits = lax.shift_right_logical(pltpu.bitcast(x, jnp.uint32), jnp.uint32(32))
        return pltpu.bitcast(bits, F32)

    pw_rows = blocks_per_stage * CONV_SUB
    taps = dw_ref[...]
    done = 0
    for idx, (piece, n_blocks) in enumerate(stages):
        res = piece()
        for _ in range(n_blocks):
            conv_block(done, taps)
            done += CONV_SUB
            if done % pw_rows == 0:
                r0 = done - pw_rows
                yconv_ref[r0:done, :] = (jnp.dot(ybuf[r0:done, :], pw_ref[...],
                                                 preferred_element_type=F32)
                                         + pwb_ref[...]).astype(BF16)
        if idx + 1 < len(stages) and stages[idx + 1][1]:
            taps = dw_ref[...] + exact_zero(res[res.shape[0] - 1:, 0:xs.shape[2]])
    xs[0, 0:CONV_HALO, :] = xs[0, rows:rows + CONV_HALO, :]


def _in_proj(x2, w_r, w_vt, conv_params, bsz):
    n, d = x2.shape
    tm = PROJ_ROWS
    kb_per_tile = tm // ATT_K
    tiles_per_seq = n // bsz // tm
    widths = (256, 512, 256, 256, LANES, 512, 512)
    dtypes = (BF16, F32, F32, F32, F32, BF16, BF16)
    wv = w_vt.shape[0]
    const = lambda i: (0, 0)
    return pl.pallas_call(
        functools.partial(_in_proj_kernel, tiles_per_seq=tiles_per_seq),
        grid=(n // tm,),
        in_specs=[pl.BlockSpec((tm, d), lambda i: (i, 0)),
                  pl.BlockSpec(w_r.shape, const),
                  pl.BlockSpec(w_vt.shape, const)]
        + [pl.BlockSpec(p.shape, const) for p in conv_params],
        out_specs=[pl.BlockSpec((tm, w), lambda i: (i, 0)) for w in widths]
        + [pl.BlockSpec((1, kb_per_tile, wv, ATT_K),
                        lambda i: (i // tiles_per_seq, i % tiles_per_seq, 0, 0))],
        out_shape=[jax.ShapeDtypeStruct((n, w), dt) for w, dt in zip(widths, dtypes)]
        + [jax.ShapeDtypeStruct((bsz, tiles_per_seq * kb_per_tile, wv, ATT_K), BF16)],
        scratch_shapes=[pltpu.VMEM((SUBLANES, CONV_HALO + tm, widths[0]), F32),
                        pltpu.VMEM((tm, widths[0]), BF16)],
        compiler_params=_params("arbitrary"),
        name="in_proj_conv",
    )(x2, w_r, w_vt, *conv_params)


def _mlstm_kernel(qk_ref, halo_ref, v_ref, o_ref, gate_ref, cw_ref, cb_ref, gb_ref,
                  ng_ref, sel_ref, avg_ref, out_ref, qpad, ct_ref, m_ref):
    c = pl.program_id(1)
    nb = qk_ref.shape[0]
    t = ML_CHUNK
    w_pair = 2 * ML_HEAD_DIM
    k_off = ML_HEADS * ML_HEAD_DIM
    n_pairs = ML_HEADS // 2
    nt = (((1,), (1,)), ((), ()))
    tn = (((0,), (0,)), ((), ()))

    @pl.when(c == 0)
    def _():
        ct_ref[...] = jnp.zeros_like(ct_ref)
        m_ref[...] = jnp.zeros_like(m_ref)

    row = lax.broadcasted_iota(jnp.int32, (t, t), 0)
    col = lax.broadcasted_iota(jnp.int32, (t, t), 1)
    causal = row >= col
    tri = causal.astype(F32)
    lane = lax.broadcasted_iota(jnp.int32, (t, w_pair), 1)
    low = lane < ML_HEAD_DIM
    owns = (low, lane >= ML_HEAD_DIM)

    first = SUBLANES - (ML_QK_CONV - 1)
    qk, gpre_t, log_f_t = [], [], []
    for bb in range(nb):
        qpad[bb, 0:SUBLANES, :] = jnp.where(c > 0, halo_ref[bb], 0.0)
        qpad[bb, SUBLANES:SUBLANES + t, :] = qk_ref[bb]
        acc = jnp.broadcast_to(cb_ref[...], (t, qk_ref.shape[2]))
        for j in range(ML_QK_CONV):
            acc = acc + cw_ref[j:j + 1, :] * qpad[bb, first + j:first + j + t, :]
        qk.append(acc * _sigmoid(acc))
        g_t = (gate_ref[bb] + gb_ref[...]).T[0:SUBLANES, :]
        gpre_t.append(g_t)
        log_f_t.append(jnp.minimum(g_t, 0.0) - jnp.log(1.0 + jnp.exp(-jnp.abs(g_t))))

    tri_t = (row <= col).astype(F32)
    cums = [jnp.dot(lf, tri_t, preferred_element_type=F32, precision=lax.Precision.HIGHEST)
            for lf in log_f_t]

    items = [(bb, h) for bb in range(nb) for h in range(ML_HEADS)]
    st = {}
    for bb in range(nb):
        for pair in range(n_pairs):
            lo = pair * w_pair
            k_pair = qk[bb][:, k_off + lo:k_off + lo + w_pair] * (ML_HEAD_DIM ** -0.5)
            st[bb, pair, "q"] = qk[bb][:, lo:lo + w_pair]
            st[bb, pair, "k"] = k_pair
            st[bb, pair, "kbt"] = k_pair.T.astype(BF16)
            st[bb, pair, "v"] = v_ref[bb, :, lo:lo + w_pair]
    for (bb, h) in items:
        pair, par = divmod(h, 2)
        st[bb, h, "q_m"] = jnp.where(owns[par], st[bb, pair, "q"], 0.0).astype(BF16)
        st[bb, h, "vx"] = jnp.where(owns[par], st[bb, pair, "v"], 1.0).astype(BF16)
    for (bb, h) in items:
        st[bb, h, "s_qk"] = jnp.dot(st[bb, h, "q_m"], st[bb, h // 2, "kbt"],
                                    preferred_element_type=F32)
    for (bb, h) in items:
        st[bb, h, "ct_prev"] = ct_ref[bb, h]
        st[bb, h, "inter"] = jnp.dot(st[bb, h, "q_m"], st[bb, h, "ct_prev"].astype(BF16),
                                     preferred_element_type=F32)

    head_row = lax.broadcasted_iota(jnp.int32, (SUBLANES, t), 0) < ML_HEADS
    lane_t = lax.broadcasted_iota(jnp.int32, (SUBLANES, t), 1)
    def last_lane(x):
        return jnp.broadcast_to(x[:, t - 1:t], x.shape)

    r_t, a_st_t, e_st_t, g_hl = [], [], [], []
    for bb in range(nb):
        b_t = pltpu.roll(cums[bb], ML_HEADS, 0)
        b_t = jnp.where(head_row, b_t, 0.0)
        r = jnp.where(head_row, gpre_t[bb] - b_t, 0.0)
        cm = r
        shift = 1
        while shift < t:
            cm = jnp.where(lane_t >= shift, jnp.maximum(cm, pltpu.roll(cm, shift, 1)), cm)
            shift *= 2
        m_prev = m_ref[bb]
        u = jnp.maximum(m_prev, cm)
        a_inter = jnp.exp(m_prev - u)
        floor = jnp.exp(-(b_t + u))
        g_end = last_lane(b_t)
        cm_end = last_lane(cm)
        p_end = jnp.exp(r - cm_end)
        m_loc = g_end + cm_end
        m_new = jnp.maximum(g_end + m_prev, m_loc)
        a_st_t.append(jnp.exp(g_end + m_prev - m_new))
        e_st_t.append(jnp.exp(m_loc - m_new))
        m_ref[bb] = jnp.where(head_row, m_new, 0.0)
        r_t.append(r)
        parts = []
        for tile in (u, a_inter, floor, p_end):
            hi = tile.astype(BF16)
            parts += [hi, (tile - hi.astype(F32)).astype(BF16)]
        g_hl.append(jnp.concatenate(parts, axis=0))
    rep = {(bb, h): lax.dot_general(g_hl[bb], sel_ref[h], tn, preferred_element_type=F32)
           for bb in range(nb) for h in range(ML_HEADS)}

    def tile_rows(x_row):
        return jnp.tile(jnp.broadcast_to(x_row, (SUBLANES, LANES)), (t // SUBLANES, 1))

    for (bb, h) in items:
        pair, par = divmod(h, 2)
        u_col = rep[bb, h][:, 0:LANES]
        st[bb, h, "a_inter"] = rep[bb, h][:, LANES:2 * LANES]
        st[bb, h, "floor"] = rep[bb, h][:, 2 * LANES:3 * LANES]
        p_end = rep[bb, h][:, 3 * LANES:4 * LANES]
        r_row = tile_rows(r_t[bb][h:h + 1, :])
        st[bb, h, "dexp"] = jnp.exp(jnp.where(causal, r_row - u_col, -jnp.inf))
        kp = (jnp.where(owns[par], st[bb, pair, "k"], 0.0) * p_end).astype(BF16)
        c_loc = lax.dot_general(kp, st[bb, h, "vx"], tn, preferred_element_type=F32)
        ct_ref[bb, h] = (tile_rows(a_st_t[bb][h:h + 1, :]) * st[bb, h, "ct_prev"]
                         + tile_rows(e_st_t[bb][h:h + 1, :]) * c_loc)

    for (bb, h) in items:
        sc = (st[bb, h, "s_qk"] * st[bb, h, "dexp"]).astype(BF16)
        st[bb, h, "intra"] = jnp.dot(sc, st[bb, h, "vx"], preferred_element_type=F32)
    def half_mean(x):
        hi = x.astype(BF16)
        lo = (x - hi.astype(F32)).astype(BF16)
        return jnp.dot(jnp.concatenate([hi, lo], axis=1), avg_ref[...],
                       preferred_element_type=F32)

    for bb in range(nb):
        for pair in range(n_pairs):
            lo = pair * w_pair
            h0, h1 = 2 * pair, 2 * pair + 1
            nd0 = st[bb, h0, "a_inter"] * st[bb, h0, "inter"] + st[bb, h0, "intra"]
            nd1 = st[bb, h1, "a_inter"] * st[bb, h1, "inter"] + st[bb, h1, "intra"]
            den = pltpu.roll(jnp.where(low, nd1, nd0), ML_HEAD_DIM, 1)
            floor = jnp.where(low, st[bb, h0, "floor"], st[bb, h1, "floor"])
            h_pair = jnp.where(low, nd0, nd1) / jnp.maximum(jnp.abs(den), floor)
            x = h_pair * _sigmoid(o_ref[bb, :, lo:lo + w_pair])
            xc = x - half_mean(x)
            var = half_mean(xc * xc)
            y = xc * lax.rsqrt(var + LN_EPS) * ng_ref[:, lo:lo + w_pair]
            out_ref[bb, :, lo:lo + w_pair] = y.astype(BF16)


def _mlstm(mqk, mv, mo, gates, cw, cb, gb, ng):
    bsz, s, wqk = mqk.shape
    wv = mv.shape[2]
    t = ML_CHUNK
    nb = math.gcd(bsz, ML_BATCH)
    halo_per_chunk = t // SUBLANES
    n_rep = 4
    src = jnp.arange(n_rep * 2 * SUBLANES)[:, None]
    dst = jnp.arange(n_rep * LANES)[None, :] // LANES
    sel = jnp.stack([((src // (2 * SUBLANES) == dst) & (src % SUBLANES == h)).astype(BF16)
                     for h in range(ML_HEADS)])
    grp = lambda idx: (idx % LANES) // ML_HEAD_DIM
    avg = jnp.where(grp(jnp.arange(2 * LANES))[:, None] == grp(jnp.arange(LANES))[None, :],
                    1.0 / ML_HEAD_DIM, 0.0).astype(BF16)
    const = lambda bi, ci: (0, 0)
    return pl.pallas_call(
        _mlstm_kernel,
        grid=(bsz // nb, s // t),
        in_specs=[
            pl.BlockSpec((nb, t, wqk), lambda bi, ci: (bi, ci, 0)),
            pl.BlockSpec((nb, SUBLANES, wqk),
                         lambda bi, ci: (bi, jnp.maximum(ci * halo_per_chunk - 1, 0), 0)),
            pl.BlockSpec((nb, t, wv), lambda bi, ci: (bi, ci, 0)),
            pl.BlockSpec((nb, t, wv), lambda bi, ci: (bi, ci, 0)),
            pl.BlockSpec((nb, t, LANES), lambda bi, ci: (bi, ci, 0)),
            pl.BlockSpec(cw.shape, const),
            pl.BlockSpec(cb.shape, const),
            pl.BlockSpec(gb.shape, const),
            pl.BlockSpec(ng.shape, const),
            pl.BlockSpec(sel.shape, lambda bi, ci: (0, 0, 0)),
            pl.BlockSpec(avg.shape, const),
        ],
        out_specs=pl.BlockSpec((nb, t, wv), lambda bi, ci: (bi, ci, 0)),
        out_shape=jax.ShapeDtypeStruct((bsz, s, wv), BF16),
        scratch_shapes=[pltpu.VMEM((nb, SUBLANES + t, wqk), F32),
                        pltpu.VMEM((nb, ML_HEADS, 2 * ML_HEAD_DIM, LANES), F32),
                        pltpu.VMEM((nb, SUBLANES, LANES), F32)],
        compiler_params=_params("parallel", "arbitrary"),
        name="mlstm",
    )(mqk, mqk, mv, mo, gates, cw, cb, gb, ng, sel, avg)


def _attn_kernel(q_ref, qn_ref, k_ref, vt_ref, lam_ref, ng_ref, out_ref, qm_ref, acc_ref,
                 s0_ref, s1_ref, mblk_ref, m_ref, *, lambda_init):
    i = pl.program_id(2)
    tq = q_ref.shape[1]
    tk = vt_ref.shape[3]
    lane = lax.broadcasted_iota(jnp.int32, (tq, 2 * DA_QK_DIM), 1)

    def map_operands(q):
        zero = jnp.zeros_like(q)
        return jnp.where(lane < DA_QK_DIM, q, zero), jnp.where(lane >= DA_QK_DIM, q, zero)

    qm_ref[0], qm_ref[1] = map_operands(q_ref[0])
    acc_ref[...] = jnp.zeros_like(acc_ref)
    m_ref[...] = jnp.full_like(m_ref, -jnp.inf)

    ones_rows = jnp.ones((ATT_SUM_ROWS, tk), BF16)

    def scores(j, s_ref, slot, qms=None):
        k_blk = k_ref[0, pl.ds(pl.multiple_of(j * tk, tk), tk), :]
        for mp in range(2):
            qm = qm_ref[mp] if qms is None else qms[mp]
            st = lax.dot_general(k_blk, qm, (((1,), (1,)), ((), ())),
                                 preferred_element_type=F32)
            s_ref[mp] = st
            mblk_ref[slot, mp] = jnp.max(st, axis=0, keepdims=True)

    def softmax_pv(j, s_ref, slot, masked):
        vt_blk = jnp.concatenate([vt_ref[0, j], ones_rows], axis=0)
        if masked:
            key = lax.broadcasted_iota(jnp.int32, (tk, tq), 0)
            qry = lax.broadcasted_iota(jnp.int32, (tk, tq), 1)
            keep = key <= qry
        for mp in range(2):
            st = s_ref[mp]
            if masked:
                st = jnp.where(keep, st, -jnp.inf)
                m_blk = jnp.max(st, axis=0, keepdims=True)
            else:
                m_blk = mblk_ref[slot, mp]
            m_old = m_ref[mp]
            m_new = jnp.maximum(m_old, m_blk)
            alpha = jnp.exp2(m_old - m_new)
            pexp = jnp.exp2(st - m_new).astype(BF16)
            acc_ref[mp] = alpha * acc_ref[mp] + jnp.dot(vt_blk, pexp,
                                                       preferred_element_type=F32)
            m_ref[mp] = m_new

    def run(bufs):
        @pl.when(i == 0)
        def _():
            scores(0, *bufs[0])

        def advance(first, count):
            for u in range(count):
                scores(first + u + 1, *bufs[(u + 1) % 2])
                softmax_pv(first + u, *bufs[u % 2], False)

        def group(jj, carry):
            advance(ATT_UNROLL * jj, ATT_UNROLL)
            return carry

        lax.fori_loop(0, i // ATT_UNROLL, group, 0)

        for rest in range(ATT_UNROLL):
            @pl.when(i % ATT_UNROLL == rest)
            def _(rest=rest):
                advance(i - rest, rest)
                scores(0, *bufs[(rest + 1) % 2], map_operands(qn_ref[0]))
                softmax_pv(i, *bufs[rest % 2], True)

    start = ((i + 1) // 2) % 2
    buf0, buf1 = (s0_ref, 0), (s1_ref, 1)

    @pl.when(start == 0)
    def _():
        run((buf0, buf1))

    @pl.when(start == 1)
    def _():
        run((buf1, buf0))

    lp = lam_ref[...]
    lam = (jnp.exp(jnp.sum(lp[0:1, :] * lp[1:2, :], axis=-1, keepdims=True))
           - jnp.exp(jnp.sum(lp[2:3, :] * lp[3:4, :], axis=-1, keepdims=True))
           + lambda_init)
    dv = DA_V_DIM
    o_t = (acc_ref[0, 0:dv, :] / acc_ref[0, dv:dv + 1, :]
           - lam * (acc_ref[1, 0:dv, :] / acc_ref[1, dv:dv + 1, :]))
    ms = jnp.mean(o_t * o_t, axis=0, keepdims=True)
    y_t = o_t * lax.rsqrt(ms + LN_EPS) * (ng_ref[...] * (1.0 - lambda_init))
    out_ref[0] = y_t.T.astype(BF16)


def _diff_attention(dq, dk, dvt, lam_p, ng_col, lambda_init):
    bsz, s, _ = dq.shape
    tq = ATT_Q
    nk, tk = dvt.shape[1], dvt.shape[3]
    assert tq == tk and nk * tk == s
    nq = s // tq
    return pl.pallas_call(
        functools.partial(_attn_kernel, lambda_init=lambda_init),
        grid=(bsz, DA_HEADS, nq),
        in_specs=[
            pl.BlockSpec((1, tq, 2 * DA_QK_DIM), lambda b, h, i: (b, i, h)),
            pl.BlockSpec((1, tq, 2 * DA_QK_DIM),
                         lambda b, h, i: (b, jnp.minimum(i + 1, nq - 1), h)),
            pl.BlockSpec((1, s, 2 * DA_QK_DIM), lambda b, h, i: (b, 0, h)),
            pl.BlockSpec((1, nk, DA_V_DIM, tk), lambda b, h, i: (b, 0, h, 0)),
            pl.BlockSpec(lam_p.shape, lambda b, h, i: (0, 0)),
            pl.BlockSpec(ng_col.shape, lambda b, h, i: (0, 0)),
        ],
        out_specs=pl.BlockSpec((1, tq, DA_V_DIM), lambda b, h, i: (b, i, h)),
        out_shape=jax.ShapeDtypeStruct((bsz, s, DA_HEADS * DA_V_DIM), BF16),
        scratch_shapes=[pltpu.VMEM((2, tq, 2 * DA_QK_DIM), BF16),
                        pltpu.VMEM((2, DA_V_DIM + ATT_SUM_ROWS, tq), F32),
                        pltpu.VMEM((2, tk, tq), F32),
                        pltpu.VMEM((2, tk, tq), F32),
                        pltpu.VMEM((2, 2, 1, tq), F32),
                        pltpu.VMEM((2, 1, tq), F32)],
        compiler_params=_params("parallel", "parallel", "arbitrary"),
        name="diff_attention",
    )(dq, dq, dk, dvt, lam_p, ng_col)


def _out_mlp_kernel(yc_ref, ym_ref, yd_ref, x_ref, wo_ref, g1_ref, b1_ref, wu_ref, wd_ref,
                    g2_ref, b2_ref, out_ref, x1_ref, xb_ref, acc_ref):
    f = pl.program_id(1)
    last = pl.num_programs(1) - 1
    wc = yc_ref.shape[1]
    wm = ym_ref.shape[1]

    def mix(r):
        rows = pl.ds(r, MLP_SUB)
        h = jnp.dot(yc_ref[rows, :], wo_ref[0:wc, :], preferred_element_type=F32)
        h = h + jnp.dot(ym_ref[rows, :], wo_ref[wc:wc + wm, :], preferred_element_type=F32)
        return h + jnp.dot(yd_ref[rows, :], wo_ref[wc + wm:, :], preferred_element_type=F32)

    def step(first, final):
        n_rows = x_ref.shape[0]
        if first:
            h_next = mix(0)
        for r in range(0, n_rows, MLP_SUB):
            rows = pl.ds(r, MLP_SUB)
            if first:
                h = h_next
                if r + MLP_SUB < n_rows:
                    h_next = mix(r + MLP_SUB)
                x1 = _layer_norm_rows(DEEPNORM_ALPHA * x_ref[rows, :] + h,
                                      g1_ref[...], b1_ref[...])
                x1_ref[rows, :] = x1
                xb = x1.astype(BF16)
                xb_ref[rows, :] = xb
            else:
                xb = xb_ref[rows, :]
            up = jnp.maximum(jnp.dot(xb, wu_ref[...], preferred_element_type=F32), 0.0)
            down = jnp.dot((up * up).astype(BF16), wd_ref[...], preferred_element_type=F32)
            acc = down if first else acc_ref[rows, :] + down
            if final:
                out_ref[rows, :] = _layer_norm_rows(DEEPNORM_ALPHA * x1_ref[rows, :] + acc,
                                                    g2_ref[...], b2_ref[...])
            else:
                acc_ref[rows, :] = acc

    @pl.when(f == 0)
    def _():
        step(True, False)

    @pl.when(jnp.logical_and(f > 0, f < last))
    def _():
        step(False, False)

    @pl.when(f == last)
    def _():
        step(False, True)


def _out_mlp(yc, ym, yd, x2, wo, g1, b1, wu, wd, g2, b2):
    n, d = x2.shape
    dff = wu.shape[1]
    tm, tf = MLP_ROWS, MLP_FF
    assert dff // tf >= 2, "the first and last hidden steps are distinct code paths"
    rows = lambda i, f: (i, 0)
    const = lambda i, f: (0, 0)
    return pl.pallas_call(
        _out_mlp_kernel,
        grid=(n // tm, dff // tf),
        in_specs=[pl.BlockSpec((tm, yc.shape[1]), rows),
                  pl.BlockSpec((tm, ym.shape[1]), rows),
                  pl.BlockSpec((tm, yd.shape[1]), rows),
                  pl.BlockSpec((tm, d), rows),
                  pl.BlockSpec(wo.shape, const),
                  pl.BlockSpec(g1.shape, const),
                  pl.BlockSpec(b1.shape, const),
                  pl.BlockSpec((d, tf), lambda i, f: (0, f)),
                  pl.BlockSpec((tf, d), lambda i, f: (f, 0)),
                  pl.BlockSpec(g2.shape, const),
                  pl.BlockSpec(b2.shape, const)],
        out_specs=pl.BlockSpec((tm, d), rows),
        out_shape=jax.ShapeDtypeStruct((n, d), F32),
        scratch_shapes=[pltpu.VMEM((tm, d), F32), pltpu.VMEM((tm, d), BF16),
                        pltpu.VMEM((tm, d), F32)],
        compiler_params=_params("parallel", "arbitrary"),
        name="out_mlp_ln",
    )(yc, ym, yd, x2, wo, g1, b1, wu, wd, g2, b2)


def _row(v):
    return v.reshape(1, -1).astype(F32)


def _pad_rows(w, rows):
    return jnp.pad(w.astype(F32), ((0, rows - w.shape[0]), (0, 0)))


def _rearranged_w_in(w):
    d = w.shape[0]
    gate_lo = 256 + 256 + 512 + 256 + 256
    gate_hi = gate_lo + 2 * ML_HEADS
    v_lo = gate_hi + 2 * DA_HEADS * 2 * DA_QK_DIM
    pad = jnp.zeros((d, LANES - 2 * ML_HEADS), w.dtype)
    w_r = jnp.concatenate([w[:, :gate_lo], w[:, gate_hi:v_lo], w[:, gate_lo:gate_hi], pad], axis=1)
    return w_r.astype(BF16), w[:, v_lo:].T.astype(BF16)


def kernel(x, w_in, b_igate, b_fgate, conv_dw_w, conv_dw_b, conv_ln_g, conv_ln_b, conv_pw_w, conv_pw_b, ml_conv_w, ml_conv_b, ml_norm_g, lam_q1, lam_k1, lam_q2, lam_k2, da_norm_g, w_out, ln1_g, ln1_b, w_up, w_down, ln2_g, ln2_b):
    bsz, s, d = x.shape
    n = bsz * s
    x2 = x.reshape(n, d)
    for l in range(DEPTH):
        lambda_init = 0.8 - 0.6 * math.exp(-0.3 * l)
        w_r, w_vt = _rearranged_w_in(w_in[l])
        conv_params = (_pad_rows(conv_dw_w[l], 32), _row(conv_dw_b[l]), _row(conv_ln_g[l]),
                       _row(conv_ln_b[l]), conv_pw_w[l].astype(BF16), _row(conv_pw_b[l]))
        y_conv, mqk, mv, mo, gates, dq, dk, dvt = _in_proj(x2, w_r, w_vt, conv_params, bsz)
        seq = lambda a: a.reshape(bsz, s, a.shape[1])

        gate_bias = jnp.pad(jnp.concatenate([b_igate[l], b_fgate[l]]).astype(F32),
                            (0, LANES - 2 * ML_HEADS)).reshape(1, LANES)
        y_ml = _mlstm(seq(mqk), seq(mv), seq(mo), seq(gates),
                      _pad_rows(ml_conv_w[l], SUBLANES), _row(ml_conv_b[l]),
                      gate_bias, _row(ml_norm_g[l]))

        lam_p = jnp.stack([lam_q1[l], lam_k1[l], lam_q2[l], lam_k2[l]]).astype(F32)
        y_da = _diff_attention(seq(dq), seq(dk), dvt, lam_p,
                               da_norm_g[l].astype(F32).reshape(-1, 1), lambda_init)

        x2 = _out_mlp(y_conv.reshape(n, -1), y_ml.reshape(n, -1), y_da.reshape(n, -1), x2,
                      w_out[l].astype(BF16), _row(ln1_g[l]), _row(ln1_b[l]),
                      w_up[l].astype(BF16), w_down[l].astype(BF16),
                      _row(ln2_g[l]), _row(ln2_b[l]))
    return x2.reshape(bsz, s, d)
```

```python
import functools
import math

import jax
import jax.numpy as jnp
from jax import lax
from jax.experimental import pallas as pl
from jax.experimental.pallas import tpu as pltpu

F32 = jnp.float32
BF16 = jnp.bfloat16

LANES = 128
SUBLANES = 8
VMEM_LIMIT_BYTES = 56 * 1024 * 1024

DEPTH = 2
CONV_KSIZE = 31
ML_HEADS = 4
ML_HEAD_DIM = 64
ML_QK_CONV = 4
ML_CHUNK = 128
DA_HEADS = 4
DA_QK_DIM = 64
DA_V_DIM = 128
LN_EPS = 1e-5
DEEPNORM_ALPHA = (2 * DEPTH) ** 0.25

PROJ_ROWS = 1024
CONV_HALO = 32
CONV_SUB = 64
ATT_Q = 512
ATT_K = 512
ML_BATCH = 4
ATT_UNROLL = 8
ATT_SUM_ROWS = 16
MLP_ROWS = 1024
MLP_FF = 1024
MLP_SUB = 256


def _params(*semantics):
    return pltpu.CompilerParams(dimension_semantics=semantics,
                                vmem_limit_bytes=VMEM_LIMIT_BYTES)


def _sigmoid(x):
    return 1.0 / (1.0 + jnp.exp(-x))


def _layer_norm_rows(x, g, b):
    mu = jnp.mean(x, axis=-1, keepdims=True)
    xc = x - mu
    var = jnp.mean(xc * xc, axis=-1, keepdims=True)
    return xc * lax.rsqrt(var + LN_EPS) * g + b


def _in_proj_kernel(x_ref, w_ref, wvt_ref, dw_ref, dwb_ref, cg_ref, cb_ref, pw_ref, pwb_ref,
                    yconv_ref, mqk_ref, mv_ref, mo_ref, gate_ref, dq_ref, dk_ref, dvt_ref,
                    xs, ybuf, *, tiles_per_seq):
    rows = x_ref.shape[0]
    xb = x_ref[...].astype(BF16)

    def sec(lo, width):
        return jnp.dot(xb, w_ref[:, lo:lo + width], preferred_element_type=F32)

    @pl.when(pl.program_id(0) % tiles_per_seq == 0)
    def _():
        xs[0, 0:CONV_HALO, :] = jnp.zeros((CONV_HALO, xs.shape[2]), F32)

    xs[0, CONV_HALO:CONV_HALO + rows, :] = sec(0, 256) * _sigmoid(sec(256, 256))
    span = CONV_HALO + rows - SUBLANES
    for k in range(1, SUBLANES):
        xs[k, 0:span, :] = xs[0, k:k + span, :]

    def conv_block(r, taps):
        first = CONV_HALO - (CONV_KSIZE - 1)
        acc = jnp.broadcast_to(dwb_ref[...], (CONV_SUB, xs.shape[2]))
        for j in range(CONV_KSIZE):
            k = (first + j) % SUBLANES
            lo = r + first + j - k
            acc = acc + taps[j:j + 1, :] * xs[k, lo:lo + CONV_SUB, :]
        y = _layer_norm_rows(acc, cg_ref[...], cb_ref[...])
        ybuf[r:r + CONV_SUB, :] = (y * _sigmoid(y)).astype(BF16)

    def proj_mqk():
        res = sec(512, 512)
        mqk_ref[...] = res
        return res

    def proj_mv_mo():
        mv_ref[...] = sec(1024, 256)
        res = sec(1280, 256)
        mo_ref[...] = res
        return res

    def proj_dq():
        res = sec(1536, 512)
        dq_ref[...] = (res * (DA_QK_DIM ** -0.5 * math.log2(math.e))).astype(BF16)
        return res

    def proj_dk_gate_vt():
        res = sec(2048, 512)
        dk_ref[...] = res.astype(BF16)
        gate_ref[...] = sec(2560, LANES)
        vt = lax.dot_general(wvt_ref[...], xb, (((1,), (1,)), ((), ())),
                             preferred_element_type=F32).astype(BF16)
        for kb in range(dvt_ref.shape[1]):
            dvt_ref[0, kb] = vt[:, kb * ATT_K:(kb + 1) * ATT_K]
        return res

    blocks_per_stage = rows // CONV_SUB // 4
    stages = [(proj, blocks_per_stage)
              for proj in (proj_mqk, proj_mv_mo, proj_dq, proj_dk_gate_vt)]

    def exact_zero(x):
        bits = lax.shift_right_logical(pltpu.bitcast(x, jnp.uint32), jnp.uint32(32))
        return pltpu.bitcast(bits, F32)

    pw_rows = blocks_per_stage * CONV_SUB
    taps = dw_ref[...]
    done = 0
    for idx, (piece, n_blocks) in enumerate(stages):
        res = piece()
        for _ in range(n_blocks):
            conv_block(done, taps)
            done += CONV_SUB
            if done % pw_rows == 0:
                r0 = done - pw_rows
                yconv_ref[r0:done, :] = (jnp.dot(ybuf[r0:done, :], pw_ref[...],
                                                 preferred_element_type=F32)
                                         + pwb_ref[...]).astype(BF16)
        if idx + 1 < len(stages) and stages[idx + 1][1]:
            taps = dw_ref[...] + exact_zero(res[res.shape[0] - 1:, 0:xs.shape[2]])
    xs[0, 0:CONV_HALO, :] = xs[0, rows:rows + CONV_HALO, :]


def _in_proj(x2, w_r, w_vt, conv_params, bsz):
    n, d = x2.shape
    tm = PROJ_ROWS
    kb_per_tile = tm // ATT_K
    tiles_per_seq = n // bsz // tm
    widths = (256, 512, 256, 256, LANES, 512, 512)
    dtypes = (BF16, F32, F32, F32, F32, BF16, BF16)
    wv = w_vt.shape[0]
    const = lambda i: (0, 0)
    return pl.pallas_call(
        functools.partial(_in_proj_kernel, tiles_per_seq=tiles_per_seq),
        grid=(n // tm,),
        in_specs=[pl.BlockSpec((tm, d), lambda i: (i, 0)),
                  pl.BlockSpec(w_r.shape, const),
                  pl.BlockSpec(w_vt.shape, const)]
        + [pl.BlockSpec(p.shape, const) for p in conv_params],
        out_specs=[pl.BlockSpec((tm, w), lambda i: (i, 0)) for w in widths]
        + [pl.BlockSpec((1, kb_per_tile, wv, ATT_K),
                        lambda i: (i // tiles_per_seq, i % tiles_per_seq, 0, 0))],
        out_shape=[jax.ShapeDtypeStruct((n, w), dt) for w, dt in zip(widths, dtypes)]
        + [jax.ShapeDtypeStruct((bsz, tiles_per_seq * kb_per_tile, wv, ATT_K), BF16)],
        scratch_shapes=[pltpu.VMEM((SUBLANES, CONV_HALO + tm, widths[0]), F32),
                        pltpu.VMEM((tm, widths[0]), BF16)],
        compiler_params=_params("arbitrary"),
        name="in_proj_conv",
    )(x2, w_r, w_vt, *conv_params)


def _mlstm_kernel(qk_ref, halo_ref, v_ref, o_ref, gate_ref, cw_ref, cb_ref, gb_ref,
                  ng_ref, sel_ref, avg_ref, out_ref, qpad, ct_ref, m_ref):
    c = pl.program_id(1)
    nb = qk_ref.shape[0]
    t = ML_CHUNK
    w_pair = 2 * ML_HEAD_DIM
    k_off = ML_HEADS * ML_HEAD_DIM
    n_pairs = ML_HEADS // 2
    nt = (((1,), (1,)), ((), ()))
    tn = (((0,), (0,)), ((), ()))

    @pl.when(c == 0)
    def _():
        ct_ref[...] = jnp.zeros_like(ct_ref)
        m_ref[...] = jnp.zeros_like(m_ref)

    row = lax.broadcasted_iota(jnp.int32, (t, t), 0)
    col = lax.broadcasted_iota(jnp.int32, (t, t), 1)
    causal = row >= col
    tri = causal.astype(F32)
    lane = lax.broadcasted_iota(jnp.int32, (t, w_pair), 1)
    low = lane < ML_HEAD_DIM
    owns = (low, lane >= ML_HEAD_DIM)

    first = SUBLANES - (ML_QK_CONV - 1)
    qk, gpre_t, log_f_t = [], [], []
    for bb in range(nb):
        qpad[bb, 0:SUBLANES, :] = jnp.where(c > 0, halo_ref[bb], 0.0)
        qpad[bb, SUBLANES:SUBLANES + t, :] = qk_ref[bb]
        acc = jnp.broadcast_to(cb_ref[...], (t, qk_ref.shape[2]))
        for j in range(ML_QK_CONV):
            acc = acc + cw_ref[j:j + 1, :] * qpad[bb, first + j:first + j + t, :]
        qk.append(acc * _sigmoid(acc))
        g_t = (gate_ref[bb] + gb_ref[...]).T[0:SUBLANES, :]
        gpre_t.append(g_t)
        log_f_t.append(jnp.minimum(g_t, 0.0) - jnp.log(1.0 + jnp.exp(-jnp.abs(g_t))))

    tri_t = (row <= col).astype(F32)
    cums = [jnp.dot(lf, tri_t, preferred_element_type=F32, precision=lax.Precision.HIGHEST)
            for lf in log_f_t]

    items = [(bb, h) for bb in range(nb) for h in range(ML_HEADS)]
    st = {}
    for bb in range(nb):
        for pair in range(n_pairs):
            lo = pair * w_pair
            k_pair = qk[bb][:, k_off + lo:k_off + lo + w_pair] * (ML_HEAD_DIM ** -0.5)
            st[bb, pair, "q"] = qk[bb][:, lo:lo + w_pair]
            st[bb, pair, "k"] = k_pair
            st[bb, pair, "kbt"] = k_pair.T.astype(BF16)
            st[bb, pair, "v"] = v_ref[bb, :, lo:lo + w_pair]
    for (bb, h) in items:
        pair, par = divmod(h, 2)
        st[bb, h, "q_m"] = jnp.where(owns[par], st[bb, pair, "q"], 0.0).astype(BF16)
        st[bb, h, "vx"] = jnp.where(owns[par], st[bb, pair, "v"], 1.0).astype(BF16)
    for (bb, h) in items:
        st[bb, h, "s_qk"] = jnp.dot(st[bb, h, "q_m"], st[bb, h // 2, "kbt"],
                                    preferred_element_type=F32)
    for (bb, h) in items:
        st[bb, h, "ct_prev"] = ct_ref[bb, h]
        st[bb, h, "inter"] = jnp.dot(st[bb, h, "q_m"], st[bb, h, "ct_prev"].astype(BF16),
                                     preferred_element_type=F32)

    head_row = lax.broadcasted_iota(jnp.int32, (SUBLANES, t), 0) < ML_HEADS
    lane_t = lax.broadcasted_iota(jnp.int32, (SUBLANES, t), 1)
    def last_lane(x):
        return jnp.broadcast_to(x[:, t - 1:t], x.shape)

    r_t, a_st_t, e_st_t, g_hl = [], [], [], []
    for bb in range(nb):
        b_t = pltpu.roll(cums[bb], ML_HEADS, 0)
        b_t = jnp.where(head_row, b_t, 0.0)
        r = jnp.where(head_row, gpre_t[bb] - b_t, 0.0)
        cm = r
        shift = 1
        while shift < t:
            cm = jnp.where(lane_t >= shift, jnp.maximum(cm, pltpu.roll(cm, shift, 1)), cm)
            shift *= 2
        m_prev = m_ref[bb]
        u = jnp.maximum(m_prev, cm)
        a_inter = jnp.exp(m_prev - u)
        floor = jnp.exp(-(b_t + u))
        g_end = last_lane(b_t)
        cm_end = last_lane(cm)
        p_end = jnp.exp(r - cm_end)
        m_loc = g_end + cm_end
        m_new = jnp.maximum(g_end + m_prev, m_loc)
        a_st_t.append(jnp.exp(g_end + m_prev - m_new))
        e_st_t.append(jnp.exp(m_loc - m_new))
        m_ref[bb] = jnp.where(head_row, m_new, 0.0)
        r_t.append(r)
        parts = []
        for tile in (u, a_inter, floor, p_end):
            hi = tile.astype(BF16)
            parts += [hi, (tile - hi.astype(F32)).astype(BF16)]
        g_hl.append(jnp.concatenate(parts, axis=0))
    rep = {(bb, h): lax.dot_general(g_hl[bb], sel_ref[h], tn, preferred_element_type=F32)
           for bb in range(nb) for h in range(ML_HEADS)}

    def tile_rows(x_row):
        return jnp.tile(jnp.broadcast_to(x_row, (SUBLANES, LANES)), (t // SUBLANES, 1))

    for (bb, h) in items:
        pair, par = divmod(h, 2)
        u_col = rep[bb, h][:, 0:LANES]
        st[bb, h, "a_inter"] = rep[bb, h][:, LANES:2 * LANES]
        st[bb, h, "floor"] = rep[bb, h][:, 2 * LANES:3 * LANES]
        p_end = rep[bb, h][:, 3 * LANES:4 * LANES]
        r_row = tile_rows(r_t[bb][h:h + 1, :])
        st[bb, h, "dexp"] = jnp.exp(jnp.where(causal, r_row - u_col, -jnp.inf))
        kp = (jnp.where(owns[par], st[bb, pair, "k"], 0.0) * p_end).astype(BF16)
        c_loc = lax.dot_general(kp, st[bb, h, "vx"], tn, preferred_element_type=F32)
        ct_ref[bb, h] = (tile_rows(a_st_t[bb][h:h + 1, :]) * st[bb, h, "ct_prev"]
                         + tile_rows(e_st_t[bb][h:h + 1, :]) * c_loc)

    for (bb, h) in items:
        sc = (st[bb, h, "s_qk"] * st[bb, h, "dexp"]).astype(BF16)
        st[bb, h, "intra"] = jnp.dot(sc, st[bb, h, "vx"], preferred_element_type=F32)
    def half_mean(x):
        hi = x.astype(BF16)
        lo = (x - hi.astype(F32)).astype(BF16)
        return jnp.dot(jnp.concatenate([hi, lo], axis=1), avg_ref[...],
                       preferred_element_type=F32)

    for bb in range(nb):
        for pair in range(n_pairs):
            lo = pair * w_pair
            h0, h1 = 2 * pair, 2 * pair + 1
            nd0 = st[bb, h0, "a_inter"] * st[bb, h0, "inter"] + st[bb, h0, "intra"]
            nd1 = st[bb, h1, "a_inter"] * st[bb, h1, "inter"] + st[bb, h1, "intra"]
            den = pltpu.roll(jnp.where(low, nd1, nd0), ML_HEAD_DIM, 1)
            floor = jnp.where(low, st[bb, h0, "floor"], st[bb, h1, "floor"])
            h_pair = jnp.where(low, nd0, nd1) / jnp.maximum(jnp.abs(den), floor)
            x = h_pair * _sigmoid(o_ref[bb, :, lo:lo + w_pair])
            xc = x - half_mean(x)
            var = half_mean(xc * xc)
            y = xc * lax.rsqrt(var + LN_EPS) * ng_ref[:, lo:lo + w_pair]
            out_ref[bb, :, lo:lo + w_pair] = y.astype(BF16)


def _mlstm(mqk, mv, mo, gates, cw, cb, gb, ng):
    bsz, s, wqk = mqk.shape
    wv = mv.shape[2]
    t = ML_CHUNK
    nb = math.gcd(bsz, ML_BATCH)
    halo_per_chunk = t // SUBLANES
    n_rep = 4
    src = jnp.arange(n_rep * 2 * SUBLANES)[:, None]
    dst = jnp.arange(n_rep * LANES)[None, :] // LANES
    sel = jnp.stack([((src // (2 * SUBLANES) == dst) & (src % SUBLANES == h)).astype(BF16)
                     for h in range(ML_HEADS)])
    grp = lambda idx: (idx % LANES) // ML_HEAD_DIM
    avg = jnp.where(grp(jnp.arange(2 * LANES))[:, None] == grp(jnp.arange(LANES))[None, :],
                    1.0 / ML_HEAD_DIM, 0.0).astype(BF16)
    const = lambda bi, ci: (0, 0)
    return pl.pallas_call(
        _mlstm_kernel,
        grid=(bsz // nb, s // t),
        in_specs=[
            pl.BlockSpec((nb, t, wqk), lambda bi, ci: (bi, ci, 0)),
            pl.BlockSpec((nb, SUBLANES, wqk),
                         lambda bi, ci: (bi, jnp.maximum(ci * halo_per_chunk - 1, 0), 0)),
            pl.BlockSpec((nb, t, wv), lambda bi, ci: (bi, ci, 0)),
            pl.BlockSpec((nb, t, wv), lambda bi, ci: (bi, ci, 0)),
            pl.BlockSpec((nb, t, LANES), lambda bi, ci: (bi, ci, 0)),
            pl.BlockSpec(cw.shape, const),
            pl.BlockSpec(cb.shape, const),
            pl.BlockSpec(gb.shape, const),
            pl.BlockSpec(ng.shape, const),
            pl.BlockSpec(sel.shape, lambda bi, ci: (0, 0, 0)),
            pl.BlockSpec(avg.shape, const),
        ],
        out_specs=pl.BlockSpec((nb, t, wv), lambda bi, ci: (bi, ci, 0)),
        out_shape=jax.ShapeDtypeStruct((bsz, s, wv), BF16),
        scratch_shapes=[pltpu.VMEM((nb, SUBLANES + t, wqk), F32),
                        pltpu.VMEM((nb, ML_HEADS, 2 * ML_HEAD_DIM, LANES), F32),
                        pltpu.VMEM((nb, SUBLANES, LANES), F32)],
        compiler_params=_params("parallel", "arbitrary"),
        name="mlstm",
    )(mqk, mqk, mv, mo, gates, cw, cb, gb, ng, sel, avg)


def _attn_kernel(q_ref, qn_ref, k_ref, vt_ref, lam_ref, ng_ref, out_ref, qm_ref, acc_ref,
                 s0_ref, s1_ref, mblk_ref, m_ref, *, lambda_init):
    i = pl.program_id(2)
    tq = q_ref.shape[1]
    tk = vt_ref.shape[3]
    lane = lax.broadcasted_iota(jnp.int32, (tq, 2 * DA_QK_DIM), 1)

    def map_operands(q):
        zero = jnp.zeros_like(q)
        return jnp.where(lane < DA_QK_DIM, q, zero), jnp.where(lane >= DA_QK_DIM, q, zero)

    qm_ref[0], qm_ref[1] = map_operands(q_ref[0])
    acc_ref[...] = jnp.zeros_like(acc_ref)
    m_ref[...] = jnp.full_like(m_ref, -jnp.inf)

    ones_rows = jnp.ones((ATT_SUM_ROWS, tk), BF16)

    def scores(j, s_ref, slot, qms=None):
        k_blk = k_ref[0, pl.ds(pl.multiple_of(j * tk, tk), tk), :]
        for mp in range(2):
            qm = qm_ref[mp] if qms is None else qms[mp]
            st = lax.dot_general(k_blk, qm, (((1,), (1,)), ((), ())),
                                 preferred_element_type=F32)
            s_ref[mp] = st
            mblk_ref[slot, mp] = jnp.max(st, axis=0, keepdims=True)

    def softmax_pv(j, s_ref, slot, masked):
        vt_blk = jnp.concatenate([vt_ref[0, j], ones_rows], axis=0)
        if masked:
            key = lax.broadcasted_iota(jnp.int32, (tk, tq), 0)
            qry = lax.broadcasted_iota(jnp.int32, (tk, tq), 1)
            keep = key <= qry
        for mp in range(2):
            st = s_ref[mp]
            if masked:
                st = jnp.where(keep, st, -jnp.inf)
                m_blk = jnp.max(st, axis=0, keepdims=True)
            else:
                m_blk = mblk_ref[slot, mp]
            m_old = m_ref[mp]
            m_new = jnp.maximum(m_old, m_blk)
            alpha = jnp.exp2(m_old - m_new)
            pexp = jnp.exp2(st - m_new).astype(BF16)
            acc_ref[mp] = alpha * acc_ref[mp] + jnp.dot(vt_blk, pexp,
                                                       preferred_element_type=F32)
            m_ref[mp] = m_new

    def run(bufs):
        @pl.when(i == 0)
        def _():
            scores(0, *bufs[0])

        def advance(first, count):
            for u in range(count):
                scores(first + u + 1, *bufs[(u + 1) % 2])
                softmax_pv(first + u, *bufs[u % 2], False)

        def group(jj, carry):
            advance(ATT_UNROLL * jj, ATT_UNROLL)
            return carry

        lax.fori_loop(0, i // ATT_UNROLL, group, 0)

        for rest in range(ATT_UNROLL):
            @pl.when(i % ATT_UNROLL == rest)
            def _(rest=rest):
                advance(i - rest, rest)
                scores(0, *bufs[(rest + 1) % 2], map_operands(qn_ref[0]))
                softmax_pv(i, *bufs[rest % 2], True)

    start = ((i + 1) // 2) % 2
    buf0, buf1 = (s0_ref, 0), (s1_ref, 1)

    @pl.when(start == 0)
    def _():
        run((buf0, buf1))

    @pl.when(start == 1)
    def _():
        run((buf1, buf0))

    lp = lam_ref[...]
    lam = (jnp.exp(jnp.sum(lp[0:1, :] * lp[1:2, :], axis=-1, keepdims=True))
           - jnp.exp(jnp.sum(lp[2:3, :] * lp[3:4, :], axis=-1, keepdims=True))
           + lambda_init)
    dv = DA_V_DIM
    o_t = (acc_ref[0, 0:dv, :] / acc_ref[0, dv:dv + 1, :]
           - lam * (acc_ref[1, 0:dv, :] / acc_ref[1, dv:dv + 1, :]))
    ms = jnp.mean(o_t * o_t, axis=0, keepdims=True)
    y_t = o_t * lax.rsqrt(ms + LN_EPS) * (ng_ref[...] * (1.0 - lambda_init))
    out_ref[0] = y_t.T.astype(BF16)


def _diff_attention(dq, dk, dvt, lam_p, ng_col, lambda_init):
    bsz, s, _ = dq.shape
    tq = ATT_Q
    nk, tk = dvt.shape[1], dvt.shape[3]
    assert tq == tk and nk * tk == s
    nq = s // tq
    return pl.pallas_call(
        functools.partial(_attn_kernel, lambda_init=lambda_init),
        grid=(bsz, DA_HEADS, nq),
        in_specs=[
            pl.BlockSpec((1, tq, 2 * DA_QK_DIM), lambda b, h, i: (b, i, h)),
            pl.BlockSpec((1, tq, 2 * DA_QK_DIM),
                         lambda b, h, i: (b, jnp.minimum(i + 1, nq - 1), h)),
            pl.BlockSpec((1, s, 2 * DA_QK_DIM), lambda b, h, i: (b, 0, h)),
            pl.BlockSpec((1, nk, DA_V_DIM, tk), lambda b, h, i: (b, 0, h, 0)),
            pl.BlockSpec(lam_p.shape, lambda b, h, i: (0, 0)),
            pl.BlockSpec(ng_col.shape, lambda b, h, i: (0, 0)),
        ],
        out_specs=pl.BlockSpec((1, tq, DA_V_DIM), lambda b, h, i: (b, i, h)),
        out_shape=jax.ShapeDtypeStruct((bsz, s, DA_HEADS * DA_V_DIM), BF16),
        scratch_shapes=[pltpu.VMEM((2, tq, 2 * DA_QK_DIM), BF16),
                        pltpu.VMEM((2, DA_V_DIM + ATT_SUM_ROWS, tq), F32),
                        pltpu.VMEM((2, tk, tq), F32),
                        pltpu.VMEM((2, tk, tq), F32),
                        pltpu.VMEM((2, 2, 1, tq), F32),
                        pltpu.VMEM((2, 1, tq), F32)],
        compiler_params=_params("parallel", "parallel", "arbitrary"),
        name="diff_attention",
    )(dq, dq, dk, dvt, lam_p, ng_col)


def _out_mlp_kernel(yc_ref, ym_ref, yd_ref, x_ref, wo_ref, g1_ref, b1_ref, wu_ref, wd_ref,
                    g2_ref, b2_ref, out_ref, x1_ref, xb_ref, acc_ref):
    f = pl.program_id(1)
    last = pl.num_programs(1) - 1
    wc = yc_ref.shape[1]
    wm = ym_ref.shape[1]

    def mix(r):
        rows = pl.ds(r, MLP_SUB)
        h = jnp.dot(yc_ref[rows, :], wo_ref[0:wc, :], preferred_element_type=F32)
        h = h + jnp.dot(ym_ref[rows, :], wo_ref[wc:wc + wm, :], preferred_element_type=F32)
        return h + jnp.dot(yd_ref[rows, :], wo_ref[wc + wm:, :], preferred_element_type=F32)

    def step(first, final):
        n_rows = x_ref.shape[0]
        if first:
            h_next = mix(0)
        for r in range(0, n_rows, MLP_SUB):
            rows = pl.ds(r, MLP_SUB)
            if first:
                h = h_next
                if r + MLP_SUB < n_rows:
                    h_next = mix(r + MLP_SUB)
                x1 = _layer_norm_rows(DEEPNORM_ALPHA * x_ref[rows, :] + h,
                                      g1_ref[...], b1_ref[...])
                x1_ref[rows, :] = x1
                xb = x1.astype(BF16)
                xb_ref[rows, :] = xb
            else:
                xb = xb_ref[rows, :]
            up = jnp.maximum(jnp.dot(xb, wu_ref[...], preferred_element_type=F32), 0.0)
            down = jnp.dot((up * up).astype(BF16), wd_ref[...], preferred_element_type=F32)
            acc = down if first else acc_ref[rows, :] + down
            if final:
                out_ref[rows, :] = _layer_norm_rows(DEEPNORM_ALPHA * x1_ref[rows, :] + acc,
                                                    g2_ref[...], b2_ref[...])
            else:
                acc_ref[rows, :] = acc

    @pl.when(f == 0)
    def _():
        step(True, False)

    @pl.when(jnp.logical_and(f > 0, f < last))
    def _():
        step(False, False)

    @pl.when(f == last)
    def _():
        step(False, True)


def _out_mlp(yc, ym, yd, x2, wo, g1, b1, wu, wd, g2, b2):
    n, d = x2.shape
    dff = wu.shape[1]
    tm, tf = MLP_ROWS, MLP_FF
    assert dff // tf >= 2, "the first and last hidden steps are distinct code paths"
    rows = lambda i, f: (i, 0)
    const = lambda i, f: (0, 0)
    return pl.pallas_call(
        _out_mlp_kernel,
        grid=(n // tm, dff // tf),
        in_specs=[pl.BlockSpec((tm, yc.shape[1]), rows),
                  pl.BlockSpec((tm, ym.shape[1]), rows),
                  pl.BlockSpec((tm, yd.shape[1]), rows),
                  pl.BlockSpec((tm, d), rows),
                  pl.BlockSpec(wo.shape, const),
                  pl.BlockSpec(g1.shape, const),
                  pl.BlockSpec(b1.shape, const),
                  pl.BlockSpec((d, tf), lambda i, f: (0, f)),
                  pl.BlockSpec((tf, d), lambda i, f: (f, 0)),
                  pl.BlockSpec(g2.shape, const),
                  pl.BlockSpec(b2.shape, const)],
        out_specs=pl.BlockSpec((tm, d), rows),
        out_shape=jax.ShapeDtypeStruct((n, d), F32),
        scratch_shapes=[pltpu.VMEM((tm, d), F32), pltpu.VMEM((tm, d), BF16),
                        pltpu.VMEM((tm, d), F32)],
        compiler_params=_params("parallel", "arbitrary"),
        name="out_mlp_ln",
    )(yc, ym, yd, x2, wo, g1, b1, wu, wd, g2, b2)


def _row(v):
    return v.reshape(1, -1).astype(F32)


def _pad_rows(w, rows):
    return jnp.pad(w.astype(F32), ((0, rows - w.shape[0]), (0, 0)))


def _rearranged_w_in(w):
    d = w.shape[0]
    gate_lo = 256 + 256 + 512 + 256 + 256
    gate_hi = gate_lo + 2 * ML_HEADS
    v_lo = gate_hi + 2 * DA_HEADS * 2 * DA_QK_DIM
    pad = jnp.zeros((d, LANES - 2 * ML_HEADS), w.dtype)
    w_r = jnp.concatenate([w[:, :gate_lo], w[:, gate_hi:v_lo], w[:, gate_lo:gate_hi], pad], axis=1)
    return w_r.astype(BF16), w[:, v_lo:].T.astype(BF16)


def kernel(x, w_in, b_igate, b_fgate, conv_dw_w, conv_dw_b, conv_ln_g, conv_ln_b, conv_pw_w, conv_pw_b, ml_conv_w, ml_conv_b, ml_norm_g, lam_q1, lam_k1, lam_q2, lam_k2, da_norm_g, w_out, ln1_g, ln1_b, w_up, w_down, ln2_g, ln2_b):
    bsz, s, d = x.shape
    n = bsz * s
    x2 = x.reshape(n, d)
    for l in range(DEPTH):
        lambda_init = 0.8 - 0.6 * math.exp(-0.3 * l)
        w_r, w_vt = _rearranged_w_in(w_in[l])
        conv_params = (_pad_rows(conv_dw_w[l], 32), _row(conv_dw_b[l]), _row(conv_ln_g[l]),
                       _row(conv_ln_b[l]), conv_pw_w[l].astype(BF16), _row(conv_pw_b[l]))
        y_conv, mqk, mv, mo, gates, dq, dk, dvt = _in_proj(x2, w_r, w_vt, conv_params, bsz)
        seq = lambda a: a.reshape(bsz, s, a.shape[1])

        gate_bias = jnp.pad(jnp.concatenate([b_igate[l], b_fgate[l]]).astype(F32),
                            (0, LANES - 2 * ML_HEADS)).reshape(1, LANES)
        y_ml = _mlstm(seq(mqk), seq(mv), seq(mo), seq(gates),
                      _pad_rows(ml_conv_w[l], SUBLANES), _row(ml_conv_b[l]),
                      gate_bias, _row(ml_norm_g[l]))

        lam_p = jnp.stack([lam_q1[l], lam_k1[l], lam_q2[l], lam_k2[l]]).astype(F32)
        y_da = _diff_attention(seq(dq), seq(dk), dvt, lam_p,
                               da_norm_g[l].astype(F32).reshape(-1, 1), lambda_init)

        x2 = _out_mlp(y_conv.reshape(n, -1), y_ml.reshape(n, -1), y_da.reshape(n, -1), x2,
                      w_out[l].astype(BF16), _row(ln1_g[l]), _row(ln1_b[l]),
                      w_up[l].astype(BF16), w_down[l].astype(BF16),
                      _row(ln2_g[l]), _row(ln2_b[l]))
    return x2.reshape(bsz, s, d)
```

```python
import functools
import math

import jax
import jax.numpy as jnp
from jax import lax
from jax.experimental import pallas as pl
from jax.experimental.pallas import tpu as pltpu

F32 = jnp.float32
BF16 = jnp.bfloat16

LANES = 128
SUBLANES = 8
VMEM_LIMIT_BYTES = 56 * 1024 * 1024

DEPTH = 2
CONV_KSIZE = 31
ML_HEADS = 4
ML_HEAD_DIM = 64
ML_QK_CONV = 4
ML_CHUNK = 128
DA_HEADS = 4
DA_QK_DIM = 64
DA_V_DIM = 128
LN_EPS = 1e-5
DEEPNORM_ALPHA = (2 * DEPTH) ** 0.25

PROJ_ROWS = 1024
CONV_HALO = 32
CONV_SUB = 64
ATT_Q = 512
ATT_K = 512
ML_BATCH = 4
ATT_STEP_BLOCKS = 4
ATT_UNROLL = 4
ATT_SUM_ROWS = 16
MLP_ROWS = 1024
MLP_FF = 1024
MLP_SUB = 256


def _params(*semantics):
    return pltpu.CompilerParams(dimension_semantics=semantics,
                                vmem_limit_bytes=VMEM_LIMIT_BYTES)


def _sigmoid(x):
    return 1.0 / (1.0 + jnp.exp(-x))


def _layer_norm_rows(x, g, b):
    mu = jnp.mean(x, axis=-1, keepdims=True)
    xc = x - mu
    var = jnp.mean(xc * xc, axis=-1, keepdims=True)
    return xc * lax.rsqrt(var + LN_EPS) * g + b


def _in_proj_kernel(x_ref, w_ref, wvt_ref, dw_ref, dwb_ref, cg_ref, cb_ref, pw_ref, pwb_ref,
                    yconv_ref, mqk_ref, mv_ref, mo_ref, gate_ref, dq_ref, dk_ref, dvt_ref,
                    xs, ybuf, *, tiles_per_seq):
    rows = x_ref.shape[0]
    xb = x_ref[...].astype(BF16)

    def sec(lo, width):
        return jnp.dot(xb, w_ref[:, lo:lo + width], preferred_element_type=F32)

    @pl.when(pl.program_id(0) % tiles_per_seq == 0)
    def _():
        xs[0, 0:CONV_HALO, :] = jnp.zeros((CONV_HALO, xs.shape[2]), F32)

    xs[0, CONV_HALO:CONV_HALO + rows, :] = sec(0, 256) * _sigmoid(sec(256, 256))
    span = CONV_HALO + rows - SUBLANES
    for k in range(1, SUBLANES):
        xs[k, 0:span, :] = xs[0, k:k + span, :]

    def conv_block(r, taps):
        first = CONV_HALO - (CONV_KSIZE - 1)
        acc = jnp.broadcast_to(dwb_ref[...], (CONV_SUB, xs.shape[2]))
        for j in range(CONV_KSIZE):
            k = (first + j) % SUBLANES
            lo = r + first + j - k
            acc = acc + taps[j:j + 1, :] * xs[k, lo:lo + CONV_SUB, :]
        y = _layer_norm_rows(acc, cg_ref[...], cb_ref[...])
        ybuf[r:r + CONV_SUB, :] = (y * _sigmoid(y)).astype(BF16)

    def proj_mqk():
        res = sec(512, 512)
        mqk_ref[...] = res
        return res

    def proj_mv_mo():
        mv_ref[...] = sec(1024, 256)
        res = sec(1280, 256)
        mo_ref[...] = res
        return res

    def proj_dq():
        res = sec(1536, 512)
        dq_ref[...] = (res * (DA_QK_DIM ** -0.5 * math.log2(math.e))).astype(BF16)
        return res

    def proj_dk_gate_vt():
        res = sec(2048, 512)
        dk_ref[...] = res.astype(BF16)
        gate_ref[...] = sec(2560, LANES)
        vt = lax.dot_general(wvt_ref[...], xb, (((1,), (1,)), ((), ())),
                             preferred_element_type=F32).astype(BF16)
        for kb in range(dvt_ref.shape[1]):
            dvt_ref[0, kb] = vt[:, kb * ATT_K:(kb + 1) * ATT_K]
        return res

    blocks_per_stage = rows // CONV_SUB // 4
    stages = [(proj, blocks_per_stage)
              for proj in (proj_mqk, proj_mv_mo, proj_dq, proj_dk_gate_vt)]

    def exact_zero(x):
        bits = lax.shift_right_logical(pltpu.bitcast(x, jnp.uint32), jnp.uint32(32))
        return pltpu.bitcast(bits, F32)

    pw_rows = blocks_per_stage * CONV_SUB
    taps = dw_ref[...]
    done = 0
    for idx, (piece, n_blocks) in enumerate(stages):
        res = piece()
        for _ in range(n_blocks):
            conv_block(done, taps)
            done += CONV_SUB
            if done % pw_rows == 0:
                r0 = done - pw_rows
                yconv_ref[r0:done, :] = (jnp.dot(ybuf[r0:done, :], pw_ref[...],
                                                 preferred_element_type=F32)
                                         + pwb_ref[...]).astype(BF16)
        if idx + 1 < len(stages) and stages[idx + 1][1]:
            taps = dw_ref[...] + exact_zero(res[res.shape[0] - 1:, 0:xs.shape[2]])
    xs[0, 0:CONV_HALO, :] = xs[0, rows:rows + CONV_HALO, :]


def _in_proj(x2, w_r, w_vt, conv_params, bsz):
    n, d = x2.shape
    tm = PROJ_ROWS
    kb_per_tile = tm // ATT_K
    tiles_per_seq = n // bsz // tm
    widths = (256, 512, 256, 256, LANES, 512, 512)
    dtypes = (BF16, F32, F32, F32, F32, BF16, BF16)
    wv = w_vt.shape[0]
    const = lambda i: (0, 0)
    return pl.pallas_call(
        functools.partial(_in_proj_kernel, tiles_per_seq=tiles_per_seq),
        grid=(n // tm,),
        in_specs=[pl.BlockSpec((tm, d), lambda i: (i, 0)),
                  pl.BlockSpec(w_r.shape, const),
                  pl.BlockSpec(w_vt.shape, const)]
        + [pl.BlockSpec(p.shape, const) for p in conv_params],
        out_specs=[pl.BlockSpec((tm, w), lambda i: (i, 0)) for w in widths]
        + [pl.BlockSpec((1, kb_per_tile, wv, ATT_K),
                        lambda i: (i // tiles_per_seq, i % tiles_per_seq, 0, 0))],
        out_shape=[jax.ShapeDtypeStruct((n, w), dt) for w, dt in zip(widths, dtypes)]
        + [jax.ShapeDtypeStruct((bsz, tiles_per_seq * kb_per_tile, wv, ATT_K), BF16)],
        scratch_shapes=[pltpu.VMEM((SUBLANES, CONV_HALO + tm, widths[0]), F32),
                        pltpu.VMEM((tm, widths[0]), BF16)],
        compiler_params=_params("arbitrary"),
        name="in_proj_conv",
    )(x2, w_r, w_vt, *conv_params)


def _mlstm_kernel(qk_ref, halo_ref, v_ref, o_ref, gate_ref, cw_ref, cb_ref, gb_ref,
                  ng_ref, sel_ref, avg_ref, out_ref, qpad, ct_ref, m_ref):
    c = pl.program_id(1)
    nb = qk_ref.shape[0]
    t = ML_CHUNK
    w_pair = 2 * ML_HEAD_DIM
    k_off = ML_HEADS * ML_HEAD_DIM
    n_pairs = ML_HEADS // 2
    nt = (((1,), (1,)), ((), ()))
    tn = (((0,), (0,)), ((), ()))

    @pl.when(c == 0)
    def _():
        ct_ref[...] = jnp.zeros_like(ct_ref)
        m_ref[...] = jnp.zeros_like(m_ref)

    row = lax.broadcasted_iota(jnp.int32, (t, t), 0)
    col = lax.broadcasted_iota(jnp.int32, (t, t), 1)
    causal = row >= col
    tri = causal.astype(F32)
    lane = lax.broadcasted_iota(jnp.int32, (t, w_pair), 1)
    low = lane < ML_HEAD_DIM
    owns = (low, lane >= ML_HEAD_DIM)

    first = SUBLANES - (ML_QK_CONV - 1)
    qk, gpre_t, log_f_t = [], [], []
    for bb in range(nb):
        qpad[bb, 0:SUBLANES, :] = jnp.where(c > 0, halo_ref[bb], 0.0)
        qpad[bb, SUBLANES:SUBLANES + t, :] = qk_ref[bb]
        acc = jnp.broadcast_to(cb_ref[...], (t, qk_ref.shape[2]))
        for j in range(ML_QK_CONV):
            acc = acc + cw_ref[j:j + 1, :] * qpad[bb, first + j:first + j + t, :]
        qk.append(acc * _sigmoid(acc))
        g_t = (gate_ref[bb] + gb_ref[...]).T[0:SUBLANES, :]
        gpre_t.append(g_t)
        log_f_t.append(jnp.minimum(g_t, 0.0) - jnp.log(1.0 + jnp.exp(-jnp.abs(g_t))))

    tri_t = (row <= col).astype(F32)
    cums = [jnp.dot(lf, tri_t, preferred_element_type=F32, precision=lax.Precision.HIGHEST)
            for lf in log_f_t]

    items = [(bb, h) for bb in range(nb) for h in range(ML_HEADS)]
    st = {}
    for bb in range(nb):
        for pair in range(n_pairs):
            lo = pair * w_pair
            k_pair = qk[bb][:, k_off + lo:k_off + lo + w_pair] * (ML_HEAD_DIM ** -0.5)
            st[bb, pair, "q"] = qk[bb][:, lo:lo + w_pair]
            st[bb, pair, "k"] = k_pair
            st[bb, pair, "kbt"] = k_pair.T.astype(BF16)
            st[bb, pair, "v"] = v_ref[bb, :, lo:lo + w_pair]
    for (bb, h) in items:
        pair, par = divmod(h, 2)
        st[bb, h, "q_m"] = jnp.where(owns[par], st[bb, pair, "q"], 0.0).astype(BF16)
        st[bb, h, "vx"] = jnp.where(owns[par], st[bb, pair, "v"], 1.0).astype(BF16)
    for (bb, h) in items:
        st[bb, h, "s_qk"] = jnp.dot(st[bb, h, "q_m"], st[bb, h // 2, "kbt"],
                                    preferred_element_type=F32)
    for (bb, h) in items:
        st[bb, h, "ct_prev"] = ct_ref[bb, h]
        st[bb, h, "inter"] = jnp.dot(st[bb, h, "q_m"], st[bb, h, "ct_prev"].astype(BF16),
                                     preferred_element_type=F32)

    head_row = lax.broadcasted_iota(jnp.int32, (SUBLANES, t), 0) < ML_HEADS
    lane_t = lax.broadcasted_iota(jnp.int32, (SUBLANES, t), 1)
    def last_lane(x):
        return jnp.broadcast_to(x[:, t - 1:t], x.shape)

    r_t, a_st_t, e_st_t, g_hl = [], [], [], []
    for bb in range(nb):
        b_t = pltpu.roll(cums[bb], ML_HEADS, 0)
        b_t = jnp.where(head_row, b_t, 0.0)
        r = jnp.where(head_row, gpre_t[bb] - b_t, 0.0)
        cm = r
        shift = 1
        while shift < t:
            cm = jnp.where(lane_t >= shift, jnp.maximum(cm, pltpu.roll(cm, shift, 1)), cm)
            shift *= 2
        m_prev = m_ref[bb]
        u = jnp.maximum(m_prev, cm)
        a_inter = jnp.exp(m_prev - u)
        floor = jnp.exp(-(b_t + u))
        g_end = last_lane(b_t)
        cm_end = last_lane(cm)
        p_end = jnp.exp(r - cm_end)
        m_loc = g_end + cm_end
        m_new = jnp.maximum(g_end + m_prev, m_loc)
        a_st_t.append(jnp.exp(g_end + m_prev - m_new))
        e_st_t.append(jnp.exp(m_loc - m_new))
        m_ref[bb] = jnp.where(head_row, m_new, 0.0)
        r_t.append(r)
        parts = []
        for tile in (u, a_inter, floor, p_end):
            hi = tile.astype(BF16)
            parts += [hi, (tile - hi.astype(F32)).astype(BF16)]
        g_hl.append(jnp.concatenate(parts, axis=0))
    rep = {(bb, h): lax.dot_general(g_hl[bb], sel_ref[h], tn, preferred_element_type=F32)
           for bb in range(nb) for h in range(ML_HEADS)}

    def tile_rows(x_row):
        return jnp.tile(jnp.broadcast_to(x_row, (SUBLANES, LANES)), (t // SUBLANES, 1))

    for (bb, h) in items:
        pair, par = divmod(h, 2)
        u_col = rep[bb, h][:, 0:LANES]
        st[bb, h, "a_inter"] = rep[bb, h][:, LANES:2 * LANES]
        st[bb, h, "floor"] = rep[bb, h][:, 2 * LANES:3 * LANES]
        p_end = rep[bb, h][:, 3 * LANES:4 * LANES]
        r_row = tile_rows(r_t[bb][h:h + 1, :])
        st[bb, h, "dexp"] = jnp.exp(jnp.where(causal, r_row - u_col, -jnp.inf))
        kp = (jnp.where(owns[par], st[bb, pair, "k"], 0.0) * p_end).astype(BF16)
        c_loc = lax.dot_general(kp, st[bb, h, "vx"], tn, preferred_element_type=F32)
        ct_ref[bb, h] = (tile_rows(a_st_t[bb][h:h + 1, :]) * st[bb, h, "ct_prev"]
                         + tile_rows(e_st_t[bb][h:h + 1, :]) * c_loc)

    for (bb, h) in items:
        sc = (st[bb, h, "s_qk"] * st[bb, h, "dexp"]).astype(BF16)
        st[bb, h, "intra"] = jnp.dot(sc, st[bb, h, "vx"], preferred_element_type=F32)
    def half_mean(x):
        hi = x.astype(BF16)
        lo = (x - hi.astype(F32)).astype(BF16)
        return jnp.dot(jnp.concatenate([hi, lo], axis=1), avg_ref[...],
                       preferred_element_type=F32)

    for bb in range(nb):
        for pair in range(n_pairs):
            lo = pair * w_pair
            h0, h1 = 2 * pair, 2 * pair + 1
            nd0 = st[bb, h0, "a_inter"] * st[bb, h0, "inter"] + st[bb, h0, "intra"]
            nd1 = st[bb, h1, "a_inter"] * st[bb, h1, "inter"] + st[bb, h1, "intra"]
            den = pltpu.roll(jnp.where(low, nd1, nd0), ML_HEAD_DIM, 1)
            floor = jnp.where(low, st[bb, h0, "floor"], st[bb, h1, "floor"])
            h_pair = jnp.where(low, nd0, nd1) / jnp.maximum(jnp.abs(den), floor)
            x = h_pair * _sigmoid(o_ref[bb, :, lo:lo + w_pair])
            xc = x - half_mean(x)
            var = half_mean(xc * xc)
            y = xc * lax.rsqrt(var + LN_EPS) * ng_ref[:, lo:lo + w_pair]
            out_ref[bb, :, lo:lo + w_pair] = y.astype(BF16)


def _mlstm(mqk, mv, mo, gates, cw, cb, gb, ng):
    bsz, s, wqk = mqk.shape
    wv = mv.shape[2]
    t = ML_CHUNK
    nb = math.gcd(bsz, ML_BATCH)
    halo_per_chunk = t // SUBLANES
    n_rep = 4
    src = jnp.arange(n_rep * 2 * SUBLANES)[:, None]
    dst = jnp.arange(n_rep * LANES)[None, :] // LANES
    sel = jnp.stack([((src // (2 * SUBLANES) == dst) & (src % SUBLANES == h)).astype(BF16)
                     for h in range(ML_HEADS)])
    grp = lambda idx: (idx % LANES) // ML_HEAD_DIM
    avg = jnp.where(grp(jnp.arange(2 * LANES))[:, None] == grp(jnp.arange(LANES))[None, :],
                    1.0 / ML_HEAD_DIM, 0.0).astype(BF16)
    const = lambda bi, ci: (0, 0)
    return pl.pallas_call(
        _mlstm_kernel,
        grid=(bsz // nb, s // t),
        in_specs=[
            pl.BlockSpec((nb, t, wqk), lambda bi, ci: (bi, ci, 0)),
            pl.BlockSpec((nb, SUBLANES, wqk),
                         lambda bi, ci: (bi, jnp.maximum(ci * halo_per_chunk - 1, 0), 0)),
            pl.BlockSpec((nb, t, wv), lambda bi, ci: (bi, ci, 0)),
            pl.BlockSpec((nb, t, wv), lambda bi, ci: (bi, ci, 0)),
            pl.BlockSpec((nb, t, LANES), lambda bi, ci: (bi, ci, 0)),
            pl.BlockSpec(cw.shape, const),
            pl.BlockSpec(cb.shape, const),
            pl.BlockSpec(gb.shape, const),
            pl.BlockSpec(ng.shape, const),
            pl.BlockSpec(sel.shape, lambda bi, ci: (0, 0, 0)),
            pl.BlockSpec(avg.shape, const),
        ],
        out_specs=pl.BlockSpec((nb, t, wv), lambda bi, ci: (bi, ci, 0)),
        out_shape=jax.ShapeDtypeStruct((bsz, s, wv), BF16),
        scratch_shapes=[pltpu.VMEM((nb, SUBLANES + t, wqk), F32),
                        pltpu.VMEM((nb, ML_HEADS, 2 * ML_HEAD_DIM, LANES), F32),
                        pltpu.VMEM((nb, SUBLANES, LANES), F32)],
        compiler_params=_params("parallel", "arbitrary"),
        name="mlstm",
    )(mqk, mqk, mv, mo, gates, cw, cb, gb, ng, sel, avg)


def _attn_kernel(q_ref, qn_ref, k_ref, vt_ref, lam_ref, ng_ref, out_ref, qall_ref, qm_ref,
                 acc_ref, s0_ref, s1_ref, mblk_ref, m_ref, *, lambda_init):
    tq = qn_ref.shape[1]
    tk = vt_ref.shape[3]
    n_sub = q_ref.shape[1] // tq
    for t in range(n_sub):
        qall_ref[t] = q_ref[0, t * tq:(t + 1) * tq, :]
    qall_ref[n_sub] = qn_ref[0]

    def query_block(t, carry):
        _attn_query_block(pl.program_id(2) * n_sub + t, qall_ref.at[t], qall_ref.at[t + 1],
                          k_ref, vt_ref, lam_ref, ng_ref,
                          out_ref.at[0, pl.ds(pl.multiple_of(t * tq, tq), tq), :],
                          qm_ref, acc_ref, s0_ref, s1_ref, mblk_ref, m_ref, lambda_init)
        return carry

    lax.fori_loop(0, n_sub, query_block, 0)


def _attn_query_block(i, q_ref, qn_ref, k_ref, vt_ref, lam_ref, ng_ref, out_ref, qm_ref,
                      acc_ref, s0_ref, s1_ref, mblk_ref, m_ref, lambda_init):
    tq = q_ref.shape[0]
    tk = vt_ref.shape[3]
    lane = lax.broadcasted_iota(jnp.int32, (tq, 2 * DA_QK_DIM), 1)

    def map_operands(q):
        zero = jnp.zeros_like(q)
        return jnp.where(lane < DA_QK_DIM, q, zero), jnp.where(lane >= DA_QK_DIM, q, zero)

    qm_ref[0], qm_ref[1] = map_operands(q_ref[...])
    acc_ref[...] = jnp.zeros_like(acc_ref)
    m_ref[...] = jnp.full_like(m_ref, -jnp.inf)

    ones_rows = jnp.ones((ATT_SUM_ROWS, tk), BF16)

    def scores(j, s_ref, slot, qms=None):
        k_blk = k_ref[0, pl.ds(pl.multiple_of(j * tk, tk), tk), :]
        for mp in range(2):
            qm = qm_ref[mp] if qms is None else qms[mp]
            st = lax.dot_general(k_blk, qm, (((1,), (1,)), ((), ())),
                                 preferred_element_type=F32)
            s_ref[mp] = st
            mblk_ref[slot, mp] = jnp.max(st, axis=0, keepdims=True)

    def softmax_pv(j, s_ref, slot, masked):
        vt_blk = jnp.concatenate([vt_ref[0, j], ones_rows], axis=0)
        if masked:
            key = lax.broadcasted_iota(jnp.int32, (tk, tq), 0)
            qry = lax.broadcasted_iota(jnp.int32, (tk, tq), 1)
            keep = key <= qry
        for mp in range(2):
            st = s_ref[mp]
            if masked:
                st = jnp.where(keep, st, -jnp.inf)
                m_blk = jnp.max(st, axis=0, keepdims=True)
            else:
                m_blk = mblk_ref[slot, mp]
            m_old = m_ref[mp]
            m_new = jnp.maximum(m_old, m_blk)
            alpha = jnp.exp2(m_old - m_new)
            pexp = jnp.exp2(st - m_new).astype(BF16)
            acc_ref[mp] = alpha * acc_ref[mp] + jnp.dot(vt_blk, pexp,
                                                       preferred_element_type=F32)
            m_ref[mp] = m_new

    def run(bufs):
        @pl.when(i == 0)
        def _():
            scores(0, *bufs[0])

        def advance(first, count):
            for u in range(count):
                scores(first + u + 1, *bufs[(u + 1) % 2])
                softmax_pv(first + u, *bufs[u % 2], False)

        def group(jj, carry):
            advance(ATT_UNROLL * jj, ATT_UNROLL)
            return carry

        lax.fori_loop(0, i // ATT_UNROLL, group, 0)

        for rest in range(ATT_UNROLL):
            @pl.when(i % ATT_UNROLL == rest)
            def _(rest=rest):
                advance(i - rest, rest)
                scores(0, *bufs[(rest + 1) % 2], map_operands(qn_ref[...]))
                softmax_pv(i, *bufs[rest % 2], True)

    start = ((i + 1) // 2) % 2
    buf0, buf1 = (s0_ref, 0), (s1_ref, 1)

    @pl.when(start == 0)
    def _():
        run((buf0, buf1))

    @pl.when(start == 1)
    def _():
        run((buf1, buf0))

    lp = lam_ref[...]
    lam = (jnp.exp(jnp.sum(lp[0:1, :] * lp[1:2, :], axis=-1, keepdims=True))
           - jnp.exp(jnp.sum(lp[2:3, :] * lp[3:4, :], axis=-1, keepdims=True))
           + lambda_init)
    dv = DA_V_DIM
    o_t = (acc_ref[0, 0:dv, :] / acc_ref[0, dv:dv + 1, :]
           - lam * (acc_ref[1, 0:dv, :] / acc_ref[1, dv:dv + 1, :]))
    ms = jnp.mean(o_t * o_t, axis=0, keepdims=True)
    y_t = o_t * lax.rsqrt(ms + LN_EPS) * (ng_ref[...] * (1.0 - lambda_init))
    out_ref[...] = y_t.T.astype(BF16)


def _diff_attention(dq, dk, dvt, lam_p, ng_col, lambda_init):
    bsz, s, _ = dq.shape
    tq = ATT_Q
    nk, tk = dvt.shape[1], dvt.shape[3]
    assert tq == tk and nk * tk == s
    nq = s // tq
    g = math.gcd(nq, ATT_STEP_BLOCKS)
    return pl.pallas_call(
        functools.partial(_attn_kernel, lambda_init=lambda_init),
        grid=(bsz, DA_HEADS, nq // g),
        in_specs=[
            pl.BlockSpec((1, g * tq, 2 * DA_QK_DIM), lambda b, h, i: (b, i, h)),
            pl.BlockSpec((1, tq, 2 * DA_QK_DIM),
                         lambda b, h, i: (b, jnp.minimum((i + 1) * g, nq - 1), h)),
            pl.BlockSpec((1, s, 2 * DA_QK_DIM), lambda b, h, i: (b, 0, h)),
            pl.BlockSpec((1, nk, DA_V_DIM, tk), lambda b, h, i: (b, 0, h, 0)),
            pl.BlockSpec(lam_p.shape, lambda b, h, i: (0, 0)),
            pl.BlockSpec(ng_col.shape, lambda b, h, i: (0, 0)),
        ],
        out_specs=pl.BlockSpec((1, g * tq, DA_V_DIM), lambda b, h, i: (b, i, h)),
        out_shape=jax.ShapeDtypeStruct((bsz, s, DA_HEADS * DA_V_DIM), BF16),
        scratch_shapes=[pltpu.VMEM((g + 1, tq, 2 * DA_QK_DIM), BF16),
                        pltpu.VMEM((2, tq, 2 * DA_QK_DIM), BF16),
                        pltpu.VMEM((2, DA_V_DIM + ATT_SUM_ROWS, tq), F32),
                        pltpu.VMEM((2, tk, tq), F32),
                        pltpu.VMEM((2, tk, tq), F32),
                        pltpu.VMEM((2, 2, 1, tq), F32),
                        pltpu.VMEM((2, 1, tq), F32)],
        compiler_params=_params("parallel", "parallel", "arbitrary"),
        name="diff_attention",
    )(dq, dq, dk, dvt, lam_p, ng_col)


def _out_mlp_kernel(yc_ref, ym_ref, yd_ref, x_ref, wo_ref, g1_ref, b1_ref, wu_ref, wd_ref,
                    g2_ref, b2_ref, out_ref, x1_ref, xb_ref, acc_ref):
    f = pl.program_id(1)
    last = pl.num_programs(1) - 1
    wc = yc_ref.shape[1]
    wm = ym_ref.shape[1]

    def mix(r):
        rows = pl.ds(r, MLP_SUB)
        h = jnp.dot(yc_ref[rows, :], wo_ref[0:wc, :], preferred_element_type=F32)
        h = h + jnp.dot(ym_ref[rows, :], wo_ref[wc:wc + wm, :], preferred_element_type=F32)
        return h + jnp.dot(yd_ref[rows, :], wo_ref[wc + wm:, :], preferred_element_type=F32)

    def step(first, final):
        n_rows = x_ref.shape[0]
        if first:
            h_next = mix(0)
        for r in range(0, n_rows, MLP_SUB):
            rows = pl.ds(r, MLP_SUB)
            if first:
                h = h_next
                if r + MLP_SUB < n_rows:
                    h_next = mix(r + MLP_SUB)
                x1 = _layer_norm_rows(DEEPNORM_ALPHA * x_ref[rows, :] + h,
                                      g1_ref[...], b1_ref[...])
                x1_ref[rows, :] = x1
                xb = x1.astype(BF16)
                xb_ref[rows, :] = xb
            else:
                xb = xb_ref[rows, :]
            up = jnp.maximum(jnp.dot(xb, wu_ref[...], preferred_element_type=F32), 0.0)
            down = jnp.dot((up * up).astype(BF16), wd_ref[...], preferred_element_type=F32)
            acc = down if first else acc_ref[rows, :] + down
            if final:
                out_ref[rows, :] = _layer_norm_rows(DEEPNORM_ALPHA * x1_ref[rows, :] + acc,
                                                    g2_ref[...], b2_ref[...])
            else:
                acc_ref[rows, :] = acc

    @pl.when(f == 0)
    def _():
        step(True, False)

    @pl.when(jnp.logical_and(f > 0, f < last))
    def _():
        step(False, False)

    @pl.when(f == last)
    def _():
        step(False, True)


def _out_mlp(yc, ym, yd, x2, wo, g1, b1, wu, wd, g2, b2):
    n, d = x2.shape
    dff = wu.shape[1]
    tm, tf = MLP_ROWS, MLP_FF
    assert dff // tf >= 2, "the first and last hidden steps are distinct code paths"
    rows = lambda i, f: (i, 0)
    const = lambda i, f: (0, 0)
    return pl.pallas_call(
        _out_mlp_kernel,
        grid=(n // tm, dff // tf),
        in_specs=[pl.BlockSpec((tm, yc.shape[1]), rows),
                  pl.BlockSpec((tm, ym.shape[1]), rows),
                  pl.BlockSpec((tm, yd.shape[1]), rows),
                  pl.BlockSpec((tm, d), rows),
                  pl.BlockSpec(wo.shape, const),
                  pl.BlockSpec(g1.shape, const),
                  pl.BlockSpec(b1.shape, const),
                  pl.BlockSpec((d, tf), lambda i, f: (0, f)),
                  pl.BlockSpec((tf, d), lambda i, f: (f, 0)),
                  pl.BlockSpec(g2.shape, const),
                  pl.BlockSpec(b2.shape, const)],
        out_specs=pl.BlockSpec((tm, d), rows),
        out_shape=jax.ShapeDtypeStruct((n, d), F32),
        scratch_shapes=[pltpu.VMEM((tm, d), F32), pltpu.VMEM((tm, d), BF16),
                        pltpu.VMEM((tm, d), F32)],
        compiler_params=_params("parallel", "arbitrary"),
        name="out_mlp_ln",
    )(yc, ym, yd, x2, wo, g1, b1, wu, wd, g2, b2)


def _row(v):
    return v.reshape(1, -1).astype(F32)


def _pad_rows(w, rows):
    return jnp.pad(w.astype(F32), ((0, rows - w.shape[0]), (0, 0)))


def _rearranged_w_in(w):
    d = w.shape[0]
    gate_lo = 256 + 256 + 512 + 256 + 256
    gate_hi = gate_lo + 2 * ML_HEADS
    v_lo = gate_hi + 2 * DA_HEADS * 2 * DA_QK_DIM
    pad = jnp.zeros((d, LANES - 2 * ML_HEADS), w.dtype)
    w_r = jnp.concatenate([w[:, :gate_lo], w[:, gate_hi:v_lo], w[:, gate_lo:gate_hi], pad], axis=1)
    return w_r.astype(BF16), w[:, v_lo:].T.astype(BF16)


def kernel(x, w_in, b_igate, b_fgate, conv_dw_w, conv_dw_b, conv_ln_g, conv_ln_b, conv_pw_w, conv_pw_b, ml_conv_w, ml_conv_b, ml_norm_g, lam_q1, lam_k1, lam_q2, lam_k2, da_norm_g, w_out, ln1_g, ln1_b, w_up, w_down, ln2_g, ln2_b):
    bsz, s, d = x.shape
    n = bsz * s
    x2 = x.reshape(n, d)
    for l in range(DEPTH):
        lambda_init = 0.8 - 0.6 * math.exp(-0.3 * l)
        w_r, w_vt = _rearranged_w_in(w_in[l])
        conv_params = (_pad_rows(conv_dw_w[l], 32), _row(conv_dw_b[l]), _row(conv_ln_g[l]),
                       _row(conv_ln_b[l]), conv_pw_w[l].astype(BF16), _row(conv_pw_b[l]))
        y_conv, mqk, mv, mo, gates, dq, dk, dvt = _in_proj(x2, w_r, w_vt, conv_params, bsz)
        seq = lambda a: a.reshape(bsz, s, a.shape[1])

        gate_bias = jnp.pad(jnp.concatenate([b_igate[l], b_fgate[l]]).astype(F32),
                            (0, LANES - 2 * ML_HEADS)).reshape(1, LANES)
        y_ml = _mlstm(seq(mqk), seq(mv), seq(mo), seq(gates),
                      _pad_rows(ml_conv_w[l], SUBLANES), _row(ml_conv_b[l]),
                      gate_bias, _row(ml_norm_g[l]))

        lam_p = jnp.stack([lam_q1[l], lam_k1[l], lam_q2[l], lam_k2[l]]).astype(F32)
        y_da = _diff_attention(seq(dq), seq(dk), dvt, lam_p,
                               da_norm_g[l].astype(F32).reshape(-1, 1), lambda_init)

        x2 = _out_mlp(y_conv.reshape(n, -1), y_ml.reshape(n, -1), y_da.reshape(n, -1), x2,
                      w_out[l].astype(BF16), _row(ln1_g[l]), _row(ln1_b[l]),
                      w_up[l].astype(BF16), w_down[l].astype(BF16),
                      _row(ln2_g[l]), _row(ln2_b[l]))
    return x2.reshape(bsz, s, d)
```

```python
import functools
import math

import jax
import jax.numpy as jnp
from jax import lax
from jax.experimental import pallas as pl
from jax.experimental.pallas import tpu as pltpu

F32 = jnp.float32
BF16 = jnp.bfloat16

LANES = 128
SUBLANES = 8
VMEM_LIMIT_BYTES = 56 * 1024 * 1024

DEPTH = 2
CONV_KSIZE = 31
ML_HEADS = 4
ML_HEAD_DIM = 64
ML_QK_CONV = 4
ML_CHUNK = 128
DA_HEADS = 4
DA_QK_DIM = 64
DA_V_DIM = 128
LN_EPS = 1e-5
DEEPNORM_ALPHA = (2 * DEPTH) ** 0.25

PROJ_ROWS = 1024
CONV_HALO = 32
CONV_SUB = 64
ATT_Q = 512
ATT_K = 512
ML_BATCH = 4
ATT_STEP_BLOCKS = 4
ATT_UNROLL = 4
ATT_SUM_ROWS = 16
MLP_ROWS = 1024
MLP_FF = 1024
MLP_SUB = 256


def _params(*semantics):
    return pltpu.CompilerParams(dimension_semantics=semantics,
                                vmem_limit_bytes=VMEM_LIMIT_BYTES)


def _sigmoid(x):
    return 1.0 / (1.0 + jnp.exp(-x))


def _layer_norm_rows(x, g, b):
    mu = jnp.mean(x, axis=-1, keepdims=True)
    xc = x - mu
    var = jnp.mean(xc * xc, axis=-1, keepdims=True)
    return xc * lax.rsqrt(var + LN_EPS) * g + b


def _in_proj_kernel(x_ref, w_ref, wvt_ref, wkt_ref, dw_ref, dwb_ref, cg_ref, cb_ref, pw_ref,
                    pwb_ref, yconv_ref, mq_ref, mv_ref, mo_ref, gate_ref, dq_ref, dk_ref,
                    dvt_ref, mkt_ref, xs, ybuf, *, tiles_per_seq):
    rows = x_ref.shape[0]
    xb = x_ref[...].astype(BF16)

    def sec(lo, width):
        return jnp.dot(xb, w_ref[:, lo:lo + width], preferred_element_type=F32)

    @pl.when(pl.program_id(0) % tiles_per_seq == 0)
    def _():
        xs[0, 0:CONV_HALO, :] = jnp.zeros((CONV_HALO, xs.shape[2]), F32)

    xs[0, CONV_HALO:CONV_HALO + rows, :] = sec(0, 256) * _sigmoid(sec(256, 256))
    span = CONV_HALO + rows - SUBLANES
    for k in range(1, SUBLANES):
        xs[k, 0:span, :] = xs[0, k:k + span, :]

    def conv_block(r, taps):
        first = CONV_HALO - (CONV_KSIZE - 1)
        acc = jnp.broadcast_to(dwb_ref[...], (CONV_SUB, xs.shape[2]))
        for j in range(CONV_KSIZE):
            k = (first + j) % SUBLANES
            lo = r + first + j - k
            acc = acc + taps[j:j + 1, :] * xs[k, lo:lo + CONV_SUB, :]
        y = _layer_norm_rows(acc, cg_ref[...], cb_ref[...])
        ybuf[r:r + CONV_SUB, :] = (y * _sigmoid(y)).astype(BF16)

    def proj_mqk():
        res = sec(512, 256)
        mq_ref[...] = res
        kt = lax.dot_general(wkt_ref[...], xb, (((1,), (1,)), ((), ())),
                             preferred_element_type=F32)
        for ck in range(mkt_ref.shape[1]):
            mkt_ref[0, ck] = kt[:, ck * ML_CHUNK:(ck + 1) * ML_CHUNK]
        return res

    def proj_mv_mo():
        mv_ref[...] = sec(1024, 256)
        res = sec(1280, 256)
        mo_ref[...] = res
        return res

    def proj_dq():
        res = sec(1536, 512)
        dq_ref[...] = (res * (DA_QK_DIM ** -0.5 * math.log2(math.e))).astype(BF16)
        return res

    def proj_dk_gate_vt():
        res = sec(2048, 512)
        dk_ref[...] = res.astype(BF16)
        gate_ref[...] = sec(2560, LANES)
        vt = lax.dot_general(wvt_ref[...], xb, (((1,), (1,)), ((), ())),
                             preferred_element_type=F32).astype(BF16)
        for kb in range(dvt_ref.shape[1]):
            dvt_ref[0, kb] = vt[:, kb * ATT_K:(kb + 1) * ATT_K]
        return res

    blocks_per_stage = rows // CONV_SUB // 4
    stages = [(proj, blocks_per_stage)
              for proj in (proj_mqk, proj_mv_mo, proj_dq, proj_dk_gate_vt)]

    def exact_zero(x):
        bits = lax.shift_right_logical(pltpu.bitcast(x, jnp.uint32), jnp.uint32(32))
        return pltpu.bitcast(bits, F32)

    pw_rows = blocks_per_stage * CONV_SUB
    taps = dw_ref[...]
    done = 0
    for idx, (piece, n_blocks) in enumerate(stages):
        res = piece()
        for _ in range(n_blocks):
            conv_block(done, taps)
            done += CONV_SUB
            if done % pw_rows == 0:
                r0 = done - pw_rows
                yconv_ref[r0:done, :] = (jnp.dot(ybuf[r0:done, :], pw_ref[...],
                                                 preferred_element_type=F32)
                                         + pwb_ref[...]).astype(BF16)
        if idx + 1 < len(stages) and stages[idx + 1][1]:
            taps = dw_ref[...] + exact_zero(res[res.shape[0] - 1:, 0:xs.shape[2]])
    xs[0, 0:CONV_HALO, :] = xs[0, rows:rows + CONV_HALO, :]


def _in_proj(x2, w_r, w_vt, w_kt, conv_params, bsz):
    n, d = x2.shape
    tm = PROJ_ROWS
    kb_per_tile = tm // ATT_K
    ck_per_tile = tm // ML_CHUNK
    tiles_per_seq = n // bsz // tm
    widths = (256, 256, 256, 256, LANES, 512, 512)
    dtypes = (BF16, F32, F32, F32, F32, BF16, BF16)
    wv = w_vt.shape[0]
    wk = w_kt.shape[0]
    const = lambda i: (0, 0)
    slabs = lambda i: (i // tiles_per_seq, i % tiles_per_seq, 0, 0)
    return pl.pallas_call(
        functools.partial(_in_proj_kernel, tiles_per_seq=tiles_per_seq),
        grid=(n // tm,),
        in_specs=[pl.BlockSpec((tm, d), lambda i: (i, 0)),
                  pl.BlockSpec(w_r.shape, const),
                  pl.BlockSpec(w_vt.shape, const),
                  pl.BlockSpec(w_kt.shape, const)]
        + [pl.BlockSpec(p.shape, const) for p in conv_params],
        out_specs=[pl.BlockSpec((tm, w), lambda i: (i, 0)) for w in widths]
        + [pl.BlockSpec((1, kb_per_tile, wv, ATT_K), slabs),
           pl.BlockSpec((1, ck_per_tile, wk, ML_CHUNK), slabs)],
        out_shape=[jax.ShapeDtypeStruct((n, w), dt) for w, dt in zip(widths, dtypes)]
        + [jax.ShapeDtypeStruct((bsz, tiles_per_seq * kb_per_tile, wv, ATT_K), BF16),
           jax.ShapeDtypeStruct((bsz, tiles_per_seq * ck_per_tile, wk, ML_CHUNK), F32)],
        scratch_shapes=[pltpu.VMEM((SUBLANES, CONV_HALO + tm, widths[0]), F32),
                        pltpu.VMEM((tm, widths[0]), BF16)],
        compiler_params=_params("arbitrary"),
        name="in_proj_conv",
    )(x2, w_r, w_vt, w_kt, *conv_params)


def _mlstm_kernel(qk_ref, halo_ref, kt_ref, ktp_ref, v_ref, o_ref, gate_ref, cw_ref, cb_ref,
                  cwk_ref, cbk_ref, gb_ref, ng_ref, sel_ref, avg_ref, out_ref, qpad, ct_ref,
                  m_ref):
    c = pl.program_id(1)
    nb = qk_ref.shape[0]
    t = ML_CHUNK
    w_pair = 2 * ML_HEAD_DIM
    n_pairs = ML_HEADS // 2
    tn = (((0,), (0,)), ((), ()))

    @pl.when(c == 0)
    def _():
        ct_ref[...] = jnp.zeros_like(ct_ref)
        m_ref[...] = jnp.zeros_like(m_ref)

    row = lax.broadcasted_iota(jnp.int32, (t, t), 0)
    col = lax.broadcasted_iota(jnp.int32, (t, t), 1)
    causal = row >= col
    tri = causal.astype(F32)
    lane = lax.broadcasted_iota(jnp.int32, (t, w_pair), 1)
    low = lane < ML_HEAD_DIM
    owns = (low, lane >= ML_HEAD_DIM)
    own_rows = (row < ML_HEAD_DIM, row >= ML_HEAD_DIM)

    first = SUBLANES - (ML_QK_CONV - 1)
    lane_k = lax.broadcasted_iota(jnp.int32, (kt_ref.shape[2], t), 1)
    qk, kt, gpre_t, log_f_t = [], [], [], []
    for bb in range(nb):
        qpad[bb, 0:SUBLANES, :] = jnp.where(c > 0, halo_ref[bb], 0.0)
        qpad[bb, SUBLANES:SUBLANES + t, :] = qk_ref[bb]
        acc = jnp.broadcast_to(cb_ref[...], (t, qk_ref.shape[2]))
        for j in range(ML_QK_CONV):
            acc = acc + cw_ref[j:j + 1, :] * qpad[bb, first + j:first + j + t, :]
        qk.append(acc * _sigmoid(acc))
        cur = kt_ref[bb, 0]
        prev = jnp.where(c > 0, ktp_ref[bb, 0], 0.0)
        acc = cbk_ref[...] + cwk_ref[ML_QK_CONV - 1] * cur
        for s in range(1, ML_QK_CONV):
            shifted = jnp.where(lane_k >= s, pltpu.roll(cur, s, 1), pltpu.roll(prev, s, 1))
            acc = acc + cwk_ref[ML_QK_CONV - 1 - s] * shifted
        kt.append(acc * _sigmoid(acc) * (ML_HEAD_DIM ** -0.5))
        g_t = (gate_ref[bb] + gb_ref[...]).T[0:SUBLANES, :]
        gpre_t.append(g_t)
        log_f_t.append(jnp.minimum(g_t, 0.0) - jnp.log(1.0 + jnp.exp(-jnp.abs(g_t))))

    tri_t = (row <= col).astype(F32)
    cums = [jnp.dot(lf, tri_t, preferred_element_type=F32, precision=lax.Precision.HIGHEST)
            for lf in log_f_t]

    items = [(bb, h) for bb in range(nb) for h in range(ML_HEADS)]
    st = {}
    for bb in range(nb):
        for pair in range(n_pairs):
            lo = pair * w_pair
            st[bb, pair, "q"] = qk[bb][:, lo:lo + w_pair]
            st[bb, pair, "kt"] = kt[bb][lo:lo + w_pair, :]
            st[bb, pair, "kbt"] = st[bb, pair, "kt"].astype(BF16)
            st[bb, pair, "v"] = v_ref[bb, :, lo:lo + w_pair]
    for (bb, h) in items:
        pair, par = divmod(h, 2)
        st[bb, h, "q_m"] = jnp.where(owns[par], st[bb, pair, "q"], 0.0).astype(BF16)
        st[bb, h, "vx"] = jnp.where(owns[par], st[bb, pair, "v"], 1.0).astype(BF16)
    for (bb, h) in items:
        st[bb, h, "s_qk"] = jnp.dot(st[bb, h, "q_m"], st[bb, h // 2, "kbt"],
                                    preferred_element_type=F32)
    for (bb, h) in items:
        st[bb, h, "ct_prev"] = ct_ref[bb, h]
        st[bb, h, "inter"] = jnp.dot(st[bb, h, "q_m"], st[bb, h, "ct_prev"].astype(BF16),
                                     preferred_element_type=F32)

    head_row = lax.broadcasted_iota(jnp.int32, (SUBLANES, t), 0) < ML_HEADS
    lane_t = lax.broadcasted_iota(jnp.int32, (SUBLANES, t), 1)
    def last_lane(x):
        return jnp.broadcast_to(x[:, t - 1:t], x.shape)

    r_t, p_end_t, a_st_t, e_st_t, g_hl = [], [], [], [], []
    for bb in range(nb):
        b_t = pltpu.roll(cums[bb], ML_HEADS, 0)
        b_t = jnp.where(head_row, b_t, 0.0)
        r = jnp.where(head_row, gpre_t[bb] - b_t, 0.0)
        cm = r
        shift = 1
        while shift < t:
            cm = jnp.where(lane_t >= shift, jnp.maximum(cm, pltpu.roll(cm, shift, 1)), cm)
            shift *= 2
        m_prev = m_ref[bb]
        u = jnp.maximum(m_prev, cm)
        a_inter = jnp.exp(m_prev - u)
        floor = jnp.exp(-(b_t + u))
        g_end = last_lane(b_t)
        cm_end = last_lane(cm)
        p_end = jnp.exp(r - cm_end)
        m_loc = g_end + cm_end
        m_new = jnp.maximum(g_end + m_prev, m_loc)
        a_st_t.append(jnp.exp(g_end + m_prev - m_new))
        e_st_t.append(jnp.exp(m_loc - m_new))
        m_ref[bb] = jnp.where(head_row, m_new, 0.0)
        r_t.append(r)
        p_end_t.append(p_end)
        parts = []
        for tile in (u, a_inter, floor):
            hi = tile.astype(BF16)
            parts += [hi, (tile - hi.astype(F32)).astype(BF16)]
        g_hl.append(jnp.concatenate(parts, axis=0))
    rep = {(bb, h): lax.dot_general(g_hl[bb], sel_ref[h], tn, preferred_element_type=F32)
           for bb in range(nb) for h in range(ML_HEADS)}

    def tile_rows(x_row):
        return jnp.tile(jnp.broadcast_to(x_row, (SUBLANES, LANES)), (t // SUBLANES, 1))

    for (bb, h) in items:
        pair, par = divmod(h, 2)
        u_col = rep[bb, h][:, 0:LANES]
        st[bb, h, "a_inter"] = rep[bb, h][:, LANES:2 * LANES]
        st[bb, h, "floor"] = rep[bb, h][:, 2 * LANES:3 * LANES]
        r_row = tile_rows(r_t[bb][h:h + 1, :])
        st[bb, h, "dexp"] = jnp.exp(jnp.where(causal, r_row - u_col, -jnp.inf))
        kp_t = (jnp.where(own_rows[par], st[bb, pair, "kt"], 0.0)
                * tile_rows(p_end_t[bb][h:h + 1, :])).astype(BF16)
        c_loc = jnp.dot(kp_t, st[bb, h, "vx"], preferred_element_type=F32)
        ct_ref[bb, h] = (tile_rows(a_st_t[bb][h:h + 1, :]) * st[bb, h, "ct_prev"]
                         + tile_rows(e_st_t[bb][h:h + 1, :]) * c_loc)

    for (bb, h) in items:
        sc = (st[bb, h, "s_qk"] * st[bb, h, "dexp"]).astype(BF16)
        st[bb, h, "intra"] = jnp.dot(sc, st[bb, h, "vx"], preferred_element_type=F32)
    def half_mean(x):
        hi = x.astype(BF16)
        lo = (x - hi.astype(F32)).astype(BF16)
        return jnp.dot(jnp.concatenate([hi, lo], axis=1), avg_ref[...],
                       preferred_element_type=F32)

    for bb in range(nb):
        for pair in range(n_pairs):
            lo = pair * w_pair
            h0, h1 = 2 * pair, 2 * pair + 1
            nd0 = st[bb, h0, "a_inter"] * st[bb, h0, "inter"] + st[bb, h0, "intra"]
            nd1 = st[bb, h1, "a_inter"] * st[bb, h1, "inter"] + st[bb, h1, "intra"]
            den = pltpu.roll(jnp.where(low, nd1, nd0), ML_HEAD_DIM, 1)
            floor = jnp.where(low, st[bb, h0, "floor"], st[bb, h1, "floor"])
            h_pair = jnp.where(low, nd0, nd1) / jnp.maximum(jnp.abs(den), floor)
            x = h_pair * _sigmoid(o_ref[bb, :, lo:lo + w_pair])
            xc = x - half_mean(x)
            var = half_mean(xc * xc)
            y = xc * lax.rsqrt(var + LN_EPS) * ng_ref[:, lo:lo + w_pair]
            out_ref[bb, :, lo:lo + w_pair] = y.astype(BF16)


def _mlstm(mq, mkt, mv, mo, gates, cw, cb, cwk, cbk, gb, ng):
    bsz, s, wqk = mq.shape
    wv = mv.shape[2]
    wk = mkt.shape[2]
    t = ML_CHUNK
    nb = math.gcd(bsz, ML_BATCH)
    halo_per_chunk = t // SUBLANES
    n_rep = 3
    src = jnp.arange(n_rep * 2 * SUBLANES)[:, None]
    dst = jnp.arange(n_rep * LANES)[None, :] // LANES
    sel = jnp.stack([((src // (2 * SUBLANES) == dst) & (src % SUBLANES == h)).astype(BF16)
                     for h in range(ML_HEADS)])
    grp = lambda idx: (idx % LANES) // ML_HEAD_DIM
    avg = jnp.where(grp(jnp.arange(2 * LANES))[:, None] == grp(jnp.arange(LANES))[None, :],
                    1.0 / ML_HEAD_DIM, 0.0).astype(BF16)
    const = lambda bi, ci: (0, 0)
    return pl.pallas_call(
        _mlstm_kernel,
        grid=(bsz // nb, s // t),
        in_specs=[
            pl.BlockSpec((nb, t, wqk), lambda bi, ci: (bi, ci, 0)),
            pl.BlockSpec((nb, SUBLANES, wqk),
                         lambda bi, ci: (bi, jnp.maximum(ci * halo_per_chunk - 1, 0), 0)),
            pl.BlockSpec((nb, 1, wk, t), lambda bi, ci: (bi, ci, 0, 0)),
            pl.BlockSpec((nb, 1, wk, t), lambda bi, ci: (bi, jnp.maximum(ci - 1, 0), 0, 0)),
            pl.BlockSpec((nb, t, wv), lambda bi, ci: (bi, ci, 0)),
            pl.BlockSpec((nb, t, wv), lambda bi, ci: (bi, ci, 0)),
            pl.BlockSpec((nb, t, LANES), lambda bi, ci: (bi, ci, 0)),
            pl.BlockSpec(cw.shape, const),
            pl.BlockSpec(cb.shape, const),
            pl.BlockSpec(cwk.shape, lambda bi, ci: (0, 0, 0)),
            pl.BlockSpec(cbk.shape, const),
            pl.BlockSpec(gb.shape, const),
            pl.BlockSpec(ng.shape, const),
            pl.BlockSpec(sel.shape, lambda bi, ci: (0, 0, 0)),
            pl.BlockSpec(avg.shape, const),
        ],
        out_specs=pl.BlockSpec((nb, t, wv), lambda bi, ci: (bi, ci, 0)),
        out_shape=jax.ShapeDtypeStruct((bsz, s, wv), BF16),
        scratch_shapes=[pltpu.VMEM((nb, SUBLANES + t, wqk), F32),
                        pltpu.VMEM((nb, ML_HEADS, 2 * ML_HEAD_DIM, LANES), F32),
                        pltpu.VMEM((nb, SUBLANES, LANES), F32)],
        compiler_params=_params("parallel", "arbitrary"),
        name="mlstm",
    )(mq, mq, mkt, mkt, mv, mo, gates, cw, cb, cwk, cbk, gb, ng, sel, avg)


def _attn_kernel(q_ref, qn_ref, k_ref, vt_ref, lam_ref, ng_ref, out_ref, qall_ref, qm_ref,
                 acc_ref, s0_ref, s1_ref, mblk_ref, m_ref, *, lambda_init):
    tq = qn_ref.shape[1]
    tk = vt_ref.shape[3]
    n_sub = q_ref.shape[1] // tq
    for t in range(n_sub):
        qall_ref[t] = q_ref[0, t * tq:(t + 1) * tq, :]
    qall_ref[n_sub] = qn_ref[0]

    def query_block(t, carry):
        _attn_query_block(pl.program_id(2) * n_sub + t, qall_ref.at[t], qall_ref.at[t + 1],
                          k_ref, vt_ref, lam_ref, ng_ref,
                          out_ref.at[0, pl.ds(pl.multiple_of(t * tq, tq), tq), :],
                          qm_ref, acc_ref, s0_ref, s1_ref, mblk_ref, m_ref, lambda_init)
        return carry

    lax.fori_loop(0, n_sub, query_block, 0)


def _attn_query_block(i, q_ref, qn_ref, k_ref, vt_ref, lam_ref, ng_ref, out_ref, qm_ref,
                      acc_ref, s0_ref, s1_ref, mblk_ref, m_ref, lambda_init):
    tq = q_ref.shape[0]
    tk = vt_ref.shape[3]
    lane = lax.broadcasted_iota(jnp.int32, (tq, 2 * DA_QK_DIM), 1)

    def map_operands(q):
        zero = jnp.zeros_like(q)
        return jnp.where(lane < DA_QK_DIM, q, zero), jnp.where(lane >= DA_QK_DIM, q, zero)

    qm_ref[0], qm_ref[1] = map_operands(q_ref[...])
    acc_ref[...] = jnp.zeros_like(acc_ref)
    m_ref[...] = jnp.full_like(m_ref, -jnp.inf)

    ones_rows = jnp.ones((ATT_SUM_ROWS, tk), BF16)

    def scores(j, s_ref, slot, qms=None):
        k_blk = k_ref[0, pl.ds(pl.multiple_of(j * tk, tk), tk), :]
        for mp in range(2):
            qm = qm_ref[mp] if qms is None else qms[mp]
            st = lax.dot_general(k_blk, qm, (((1,), (1,)), ((), ())),
                                 preferred_element_type=F32)
            s_ref[mp] = st
            mblk_ref[slot, mp] = jnp.max(st, axis=0, keepdims=True)

    def softmax_pv(j, s_ref, slot, masked):
        vt_blk = jnp.concatenate([vt_ref[0, j], ones_rows], axis=0)
        if masked:
            key = lax.broadcasted_iota(jnp.int32, (tk, tq), 0)
            qry = lax.broadcasted_iota(jnp.int32, (tk, tq), 1)
            keep = key <= qry
        for mp in range(2):
            st = s_ref[mp]
            if masked:
                st = jnp.where(keep, st, -jnp.inf)
                m_blk = jnp.max(st, axis=0, keepdims=True)
            else:
                m_blk = mblk_ref[slot, mp]
            m_old = m_ref[mp]
            m_new = jnp.maximum(m_old, m_blk)
            alpha = jnp.exp2(m_old - m_new)
            pexp = jnp.exp2(st - m_new).astype(BF16)
            acc_ref[mp] = alpha * acc_ref[mp] + jnp.dot(vt_blk, pexp,
                                                       preferred_element_type=F32)
            m_ref[mp] = m_new

    def run(bufs):
        @pl.when(i == 0)
        def _():
            scores(0, *bufs[0])

        def advance(first, count):
            for u in range(count):
                scores(first + u + 1, *bufs[(u + 1) % 2])
                softmax_pv(first + u, *bufs[u % 2], False)

        def group(jj, carry):
            advance(ATT_UNROLL * jj, ATT_UNROLL)
            return carry

        lax.fori_loop(0, i // ATT_UNROLL, group, 0)

        for rest in range(ATT_UNROLL):
            @pl.when(i % ATT_UNROLL == rest)
            def _(rest=rest):
                advance(i - rest, rest)
                scores(0, *bufs[(rest + 1) % 2], map_operands(qn_ref[...]))
                softmax_pv(i, *bufs[rest % 2], True)

    start = ((i + 1) // 2) % 2
    buf0, buf1 = (s0_ref, 0), (s1_ref, 1)

    @pl.when(start == 0)
    def _():
        run((buf0, buf1))

    @pl.when(start == 1)
    def _():
        run((buf1, buf0))

    lp = lam_ref[...]
    lam = (jnp.exp(jnp.sum(lp[0:1, :] * lp[1:2, :], axis=-1, keepdims=True))
           - jnp.exp(jnp.sum(lp[2:3, :] * lp[3:4, :], axis=-1, keepdims=True))
           + lambda_init)
    dv = DA_V_DIM
    o_t = (acc_ref[0, 0:dv, :] / acc_ref[0, dv:dv + 1, :]
           - lam * (acc_ref[1, 0:dv, :] / acc_ref[1, dv:dv + 1, :]))
    ms = jnp.mean(o_t * o_t, axis=0, keepdims=True)
    y_t = o_t * lax.rsqrt(ms + LN_EPS) * (ng_ref[...] * (1.0 - lambda_init))
    out_ref[...] = y_t.T.astype(BF16)


def _diff_attention(dq, dk, dvt, lam_p, ng_col, lambda_init):
    bsz, s, _ = dq.shape
    tq = ATT_Q
    nk, tk = dvt.shape[1], dvt.shape[3]
    assert tq == tk and nk * tk == s
    nq = s // tq
    g = math.gcd(nq, ATT_STEP_BLOCKS)
    return pl.pallas_call(
        functools.partial(_attn_kernel, lambda_init=lambda_init),
        grid=(bsz, DA_HEADS, nq // g),
        in_specs=[
            pl.BlockSpec((1, g * tq, 2 * DA_QK_DIM), lambda b, h, i: (b, i, h)),
            pl.BlockSpec((1, tq, 2 * DA_QK_DIM),
                         lambda b, h, i: (b, jnp.minimum((i + 1) * g, nq - 1), h)),
            pl.BlockSpec((1, s, 2 * DA_QK_DIM), lambda b, h, i: (b, 0, h)),
            pl.BlockSpec((1, nk, DA_V_DIM, tk), lambda b, h, i: (b, 0, h, 0)),
            pl.BlockSpec(lam_p.shape, lambda b, h, i: (0, 0)),
            pl.BlockSpec(ng_col.shape, lambda b, h, i: (0, 0)),
        ],
        out_specs=pl.BlockSpec((1, g * tq, DA_V_DIM), lambda b, h, i: (b, i, h)),
        out_shape=jax.ShapeDtypeStruct((bsz, s, DA_HEADS * DA_V_DIM), BF16),
        scratch_shapes=[pltpu.VMEM((g + 1, tq, 2 * DA_QK_DIM), BF16),
                        pltpu.VMEM((2, tq, 2 * DA_QK_DIM), BF16),
                        pltpu.VMEM((2, DA_V_DIM + ATT_SUM_ROWS, tq), F32),
                        pltpu.VMEM((2, tk, tq), F32),
                        pltpu.VMEM((2, tk, tq), F32),
                        pltpu.VMEM((2, 2, 1, tq), F32),
                        pltpu.VMEM((2, 1, tq), F32)],
        compiler_params=_params("parallel", "parallel", "arbitrary"),
        name="diff_attention",
    )(dq, dq, dk, dvt, lam_p, ng_col)


def _out_mlp_kernel(yc_ref, ym_ref, yd_ref, x_ref, wo_ref, g1_ref, b1_ref, wu_ref, wd_ref,
                    g2_ref, b2_ref, out_ref, x1_ref, xb_ref, acc_ref):
    f = pl.program_id(1)
    last = pl.num_programs(1) - 1
    wc = yc_ref.shape[1]
    wm = ym_ref.shape[1]

    def mix(r):
        rows = pl.ds(r, MLP_SUB)
        h = jnp.dot(yc_ref[rows, :], wo_ref[0:wc, :], preferred_element_type=F32)
        h = h + jnp.dot(ym_ref[rows, :], wo_ref[wc:wc + wm, :], preferred_element_type=F32)
        return h + jnp.dot(yd_ref[rows, :], wo_ref[wc + wm:, :], preferred_element_type=F32)

    def step(first, final):
        n_rows = x_ref.shape[0]
        if first:
            h_next = mix(0)
        for r in range(0, n_rows, MLP_SUB):
            rows = pl.ds(r, MLP_SUB)
            if first:
                h = h_next
                if r + MLP_SUB < n_rows:
                    h_next = mix(r + MLP_SUB)
                x1 = _layer_norm_rows(DEEPNORM_ALPHA * x_ref[rows, :] + h,
                                      g1_ref[...], b1_ref[...])
                x1_ref[rows, :] = x1
                xb = x1.astype(BF16)
                xb_ref[rows, :] = xb
            else:
                xb = xb_ref[rows, :]
            up = jnp.maximum(jnp.dot(xb, wu_ref[...], preferred_element_type=F32), 0.0)
            down = jnp.dot((up * up).astype(BF16), wd_ref[...], preferred_element_type=F32)
            acc = down if first else acc_ref[rows, :] + down
            if final:
                out_ref[rows, :] = _layer_norm_rows(DEEPNORM_ALPHA * x1_ref[rows, :] + acc,
                                                    g2_ref[...], b2_ref[...])
            else:
                acc_ref[rows, :] = acc

    @pl.when(f == 0)
    def _():
        step(True, False)

    @pl.when(jnp.logical_and(f > 0, f < last))
    def _():
        step(False, False)

    @pl.when(f == last)
    def _():
        step(False, True)


def _out_mlp(yc, ym, yd, x2, wo, g1, b1, wu, wd, g2, b2):
    n, d = x2.shape
    dff = wu.shape[1]
    tm, tf = MLP_ROWS, MLP_FF
    assert dff // tf >= 2, "the first and last hidden steps are distinct code paths"
    rows = lambda i, f: (i, 0)
    const = lambda i, f: (0, 0)
    return pl.pallas_call(
        _out_mlp_kernel,
        grid=(n // tm, dff // tf),
        in_specs=[pl.BlockSpec((tm, yc.shape[1]), rows),
                  pl.BlockSpec((tm, ym.shape[1]), rows),
                  pl.BlockSpec((tm, yd.shape[1]), rows),
                  pl.BlockSpec((tm, d), rows),
                  pl.BlockSpec(wo.shape, const),
                  pl.BlockSpec(g1.shape, const),
                  pl.BlockSpec(b1.shape, const),
                  pl.BlockSpec((d, tf), lambda i, f: (0, f)),
                  pl.BlockSpec((tf, d), lambda i, f: (f, 0)),
                  pl.BlockSpec(g2.shape, const),
                  pl.BlockSpec(b2.shape, const)],
        out_specs=pl.BlockSpec((tm, d), rows),
        out_shape=jax.ShapeDtypeStruct((n, d), F32),
        scratch_shapes=[pltpu.VMEM((tm, d), F32), pltpu.VMEM((tm, d), BF16),
                        pltpu.VMEM((tm, d), F32)],
        compiler_params=_params("parallel", "arbitrary"),
        name="out_mlp_ln",
    )(yc, ym, yd, x2, wo, g1, b1, wu, wd, g2, b2)


def _row(v):
    return v.reshape(1, -1).astype(F32)


def _pad_rows(w, rows):
    return jnp.pad(w.astype(F32), ((0, rows - w.shape[0]), (0, 0)))


def _rearranged_w_in(w):
    d = w.shape[0]
    mk_lo = 256 + 256 + ML_HEADS * ML_HEAD_DIM
    mk_hi = mk_lo + ML_HEADS * ML_HEAD_DIM
    gate_lo = 256 + 256 + 512 + 256 + 256
    gate_hi = gate_lo + 2 * ML_HEADS
    v_lo = gate_hi + 2 * DA_HEADS * 2 * DA_QK_DIM
    pad = jnp.zeros((d, LANES - 2 * ML_HEADS), w.dtype)
    w_r = jnp.concatenate([w[:, :gate_lo], w[:, gate_hi:v_lo], w[:, gate_lo:gate_hi], pad], axis=1)
    return w_r.astype(BF16), w[:, v_lo:].T.astype(BF16), w[:, mk_lo:mk_hi].T.astype(BF16)


def kernel(x, w_in, b_igate, b_fgate, conv_dw_w, conv_dw_b, conv_ln_g, conv_ln_b, conv_pw_w, conv_pw_b, ml_conv_w, ml_conv_b, ml_norm_g, lam_q1, lam_k1, lam_q2, lam_k2, da_norm_g, w_out, ln1_g, ln1_b, w_up, w_down, ln2_g, ln2_b):
    bsz, s, d = x.shape
    n = bsz * s
    x2 = x.reshape(n, d)
    for l in range(DEPTH):
        lambda_init = 0.8 - 0.6 * math.exp(-0.3 * l)
        w_r, w_vt, w_kt = _rearranged_w_in(w_in[l])
        conv_params = (_pad_rows(conv_dw_w[l], 32), _row(conv_dw_b[l]), _row(conv_ln_g[l]),
                       _row(conv_ln_b[l]), conv_pw_w[l].astype(BF16), _row(conv_pw_b[l]))
        y_conv, mq, mv, mo, gates, dq, dk, dvt, mkt = _in_proj(x2, w_r, w_vt, w_kt,
                                                               conv_params, bsz)
        seq = lambda a: a.reshape(bsz, s, a.shape[1])

        gate_bias = jnp.pad(jnp.concatenate([b_igate[l], b_fgate[l]]).astype(F32),
                            (0, LANES - 2 * ML_HEADS)).reshape(1, LANES)
        wq = ML_HEADS * ML_HEAD_DIM
        cw_k = ml_conv_w[l][:, wq:].astype(F32)
        cb_k = ml_conv_b[l][wq:].astype(F32)
        y_ml = _mlstm(seq(mq), mkt, seq(mv), seq(mo), seq(gates),
                      _pad_rows(ml_conv_w[l][:, :wq], SUBLANES), _row(ml_conv_b[l][:wq]),
                      jnp.broadcast_to(cw_k[:, :, None], cw_k.shape + (ML_CHUNK,)),
                      jnp.broadcast_to(cb_k[:, None], cb_k.shape + (ML_CHUNK,)),
                      gate_bias, _row(ml_norm_g[l]))

        lam_p = jnp.stack([lam_q1[l], lam_k1[l], lam_q2[l], lam_k2[l]]).astype(F32)
        y_da = _diff_attention(seq(dq), seq(dk), dvt, lam_p,
                               da_norm_g[l].astype(F32).reshape(-1, 1), lambda_init)

        x2 = _out_mlp(y_conv.reshape(n, -1), y_ml.reshape(n, -1), y_da.reshape(n, -1), x2,
                      w_out[l].astype(BF16), _row(ln1_g[l]), _row(ln1_b[l]),
                      w_up[l].astype(BF16), w_down[l].astype(BF16),
                      _row(ln2_g[l]), _row(ln2_b[l]))
    return x2.reshape(bsz, s, d)
```

```python
import functools
import math

import jax
import jax.numpy as jnp
from jax import lax
from jax.experimental import pallas as pl
from jax.experimental.pallas import tpu as pltpu

F32 = jnp.float32
BF16 = jnp.bfloat16

LANES = 128
SUBLANES = 8
VMEM_LIMIT_BYTES = 56 * 1024 * 1024

DEPTH = 2
CONV_KSIZE = 31
ML_HEADS = 4
ML_HEAD_DIM = 64
ML_QK_CONV = 4
ML_CHUNK = 128
DA_HEADS = 4
DA_QK_DIM = 64
DA_V_DIM = 128
LN_EPS = 1e-5
DEEPNORM_ALPHA = (2 * DEPTH) ** 0.25

PROJ_ROWS = 1024
CONV_HALO = 32
CONV_SUB = 64
ATT_Q = 512
ATT_K = 512
ML_BATCH = 4
ATT_STEP_BLOCKS = 4
ATT_UNROLL = 4
ATT_SUM_ROWS = 16
MLP_ROWS = 1024
MLP_FF = 1024
MLP_SUB = 256


def _params(*semantics):
    return pltpu.CompilerParams(dimension_semantics=semantics,
                                vmem_limit_bytes=VMEM_LIMIT_BYTES)


def _sigmoid(x):
    return 1.0 / (1.0 + jnp.exp(-x))


def _layer_norm_rows(x, g, b):
    mu = jnp.mean(x, axis=-1, keepdims=True)
    xc = x - mu
    var = jnp.mean(xc * xc, axis=-1, keepdims=True)
    return xc * lax.rsqrt(var + LN_EPS) * g + b


def _in_proj_kernel(x_ref, w_ref, wvt_ref, wkt_ref, dw_ref, dwb_ref, cg_ref, cb_ref, pw_ref,
                    pwb_ref, yconv_ref, mq_ref, mv_ref, mo_ref, gate_ref, dq_ref, dk_ref,
                    dvt_ref, mkt_ref, xs, ybuf, *, tiles_per_seq):
    rows = x_ref.shape[0]
    xb = x_ref[...].astype(BF16)

    def sec(lo, width):
        return jnp.dot(xb, w_ref[:, lo:lo + width], preferred_element_type=F32)

    @pl.when(pl.program_id(0) % tiles_per_seq == 0)
    def _():
        xs[0, 0:CONV_HALO, :] = jnp.zeros((CONV_HALO, xs.shape[2]), F32)

    xs[0, CONV_HALO:CONV_HALO + rows, :] = sec(0, 256) * _sigmoid(sec(256, 256))
    span = CONV_HALO + rows - SUBLANES
    for k in range(1, SUBLANES):
        xs[k, 0:span, :] = xs[0, k:k + span, :]

    def conv_block(r, taps):
        first = CONV_HALO - (CONV_KSIZE - 1)
        acc = jnp.broadcast_to(dwb_ref[...], (CONV_SUB, xs.shape[2]))
        for j in range(CONV_KSIZE):
            k = (first + j) % SUBLANES
            lo = r + first + j - k
            acc = acc + taps[j:j + 1, :] * xs[k, lo:lo + CONV_SUB, :]
        y = _layer_norm_rows(acc, cg_ref[...], cb_ref[...])
        ybuf[r:r + CONV_SUB, :] = (y * _sigmoid(y)).astype(BF16)

    def proj_mqk():
        res = sec(512, 256)
        mq_ref[...] = res
        kt = lax.dot_general(wkt_ref[...], xb, (((1,), (1,)), ((), ())),
                             preferred_element_type=F32)
        for ck in range(mkt_ref.shape[1]):
            mkt_ref[0, ck] = kt[:, ck * ML_CHUNK:(ck + 1) * ML_CHUNK]
        return res

    def proj_mv_mo():
        mv_ref[...] = sec(1024, 256)
        res = sec(1280, 256)
        mo_ref[...] = res
        return res

    def proj_dq():
        res = sec(1536, 512)
        dq_ref[...] = (res * (DA_QK_DIM ** -0.5 * math.log2(math.e))).astype(BF16)
        return res

    def proj_dk_gate_vt():
        res = sec(2048, 512)
        dk_ref[...] = res.astype(BF16)
        gate_ref[...] = sec(2560, LANES)
        vt = lax.dot_general(wvt_ref[...], xb, (((1,), (1,)), ((), ())),
                             preferred_element_type=F32).astype(BF16)
        for kb in range(dvt_ref.shape[1]):
            dvt_ref[0, kb] = vt[:, kb * ATT_K:(kb + 1) * ATT_K]
        return res

    blocks_per_stage = rows // CONV_SUB // 4
    stages = [(proj, blocks_per_stage)
              for proj in (proj_mqk, proj_mv_mo, proj_dq, proj_dk_gate_vt)]

    def exact_zero(x):
        bits = lax.shift_right_logical(pltpu.bitcast(x, jnp.uint32), jnp.uint32(32))
        return pltpu.bitcast(bits, F32)

    pw_rows = blocks_per_stage * CONV_SUB
    taps = dw_ref[...]
    done = 0
    for idx, (piece, n_blocks) in enumerate(stages):
        res = piece()
        for _ in range(n_blocks):
            conv_block(done, taps)
            done += CONV_SUB
            if done % pw_rows == 0:
                r0 = done - pw_rows
                yconv_ref[r0:done, :] = (jnp.dot(ybuf[r0:done, :], pw_ref[...],
                                                 preferred_element_type=F32)
                                         + pwb_ref[...]).astype(BF16)
        if idx + 1 < len(stages) and stages[idx + 1][1]:
            taps = dw_ref[...] + exact_zero(res[res.shape[0] - 1:, 0:xs.shape[2]])
    xs[0, 0:CONV_HALO, :] = xs[0, rows:rows + CONV_HALO, :]


def _in_proj(x2, w_r, w_vt, w_kt, conv_params, bsz):
    n, d = x2.shape
    tm = PROJ_ROWS
    kb_per_tile = tm // ATT_K
    ck_per_tile = tm // ML_CHUNK
    tiles_per_seq = n // bsz // tm
    widths = (256, 256, 256, 256, LANES, 512, 512)
    dtypes = (BF16, F32, F32, F32, F32, BF16, BF16)
    wv = w_vt.shape[0]
    wk = w_kt.shape[0]
    const = lambda i: (0, 0)
    slabs = lambda i: (i // tiles_per_seq, i % tiles_per_seq, 0, 0)
    return pl.pallas_call(
        functools.partial(_in_proj_kernel, tiles_per_seq=tiles_per_seq),
        grid=(n // tm,),
        in_specs=[pl.BlockSpec((tm, d), lambda i: (i, 0)),
                  pl.BlockSpec(w_r.shape, const),
                  pl.BlockSpec(w_vt.shape, const),
                  pl.BlockSpec(w_kt.shape, const)]
        + [pl.BlockSpec(p.shape, const) for p in conv_params],
        out_specs=[pl.BlockSpec((tm, w), lambda i: (i, 0)) for w in widths]
        + [pl.BlockSpec((1, kb_per_tile, wv, ATT_K), slabs),
           pl.BlockSpec((1, ck_per_tile, wk, ML_CHUNK), slabs)],
        out_shape=[jax.ShapeDtypeStruct((n, w), dt) for w, dt in zip(widths, dtypes)]
        + [jax.ShapeDtypeStruct((bsz, tiles_per_seq * kb_per_tile, wv, ATT_K), BF16),
           jax.ShapeDtypeStruct((bsz, tiles_per_seq * ck_per_tile, wk, ML_CHUNK), F32)],
        scratch_shapes=[pltpu.VMEM((SUBLANES, CONV_HALO + tm, widths[0]), F32),
                        pltpu.VMEM((tm, widths[0]), BF16)],
        compiler_params=_params("arbitrary"),
        name="in_proj_conv",
    )(x2, w_r, w_vt, w_kt, *conv_params)


def _mlstm_kernel(qk_ref, halo_ref, kt_ref, ktp_ref, v_ref, o_ref, gate_ref, cw_ref, cb_ref,
                  cwk_ref, cbk_ref, gb_ref, ng_ref, sel_ref, avg_ref, out_ref, qpad, ct_ref,
                  m_ref):
    c = pl.program_id(1)
    nb = qk_ref.shape[0]
    t = ML_CHUNK
    w_pair = 2 * ML_HEAD_DIM
    n_pairs = ML_HEADS // 2
    tn = (((0,), (0,)), ((), ()))

    @pl.when(c == 0)
    def _():
        ct_ref[...] = jnp.zeros_like(ct_ref)
        m_ref[...] = jnp.zeros_like(m_ref)

    row = lax.broadcasted_iota(jnp.int32, (t, t), 0)
    col = lax.broadcasted_iota(jnp.int32, (t, t), 1)
    causal = row >= col
    tri = causal.astype(F32)
    lane = lax.broadcasted_iota(jnp.int32, (t, w_pair), 1)
    low = lane < ML_HEAD_DIM
    owns = (low, lane >= ML_HEAD_DIM)
    own_rows = (row < ML_HEAD_DIM, row >= ML_HEAD_DIM)

    first = SUBLANES - (ML_QK_CONV - 1)
    lane_k = lax.broadcasted_iota(jnp.int32, (kt_ref.shape[2], t), 1)
    qk, kt, gpre_t, log_f_t = [], [], [], []
    for bb in range(nb):
        qpad[bb, 0:SUBLANES, :] = jnp.where(c > 0, halo_ref[bb], 0.0)
        qpad[bb, SUBLANES:SUBLANES + t, :] = qk_ref[bb]
        acc = jnp.broadcast_to(cb_ref[...], (t, qk_ref.shape[2]))
        for j in range(ML_QK_CONV):
            acc = acc + cw_ref[j:j + 1, :] * qpad[bb, first + j:first + j + t, :]
        qk.append(acc * _sigmoid(acc))
        cur = kt_ref[bb, 0]
        prev = jnp.where(c > 0, ktp_ref[bb, 0], 0.0)
        acc = cbk_ref[...] + cwk_ref[ML_QK_CONV - 1] * cur
        for s in range(1, ML_QK_CONV):
            shifted = jnp.where(lane_k >= s, pltpu.roll(cur, s, 1), pltpu.roll(prev, s, 1))
            acc = acc + cwk_ref[ML_QK_CONV - 1 - s] * shifted
        kt.append(acc * _sigmoid(acc) * (ML_HEAD_DIM ** -0.5))
        g_t = (gate_ref[bb] + gb_ref[...]).T[0:SUBLANES, :]
        gpre_t.append(g_t)
        log_f_t.append(jnp.minimum(g_t, 0.0) - jnp.log(1.0 + jnp.exp(-jnp.abs(g_t))))

    tri_t = (row <= col).astype(F32)
    cums = [jnp.dot(lf, tri_t, preferred_element_type=F32, precision=lax.Precision.HIGHEST)
            for lf in log_f_t]

    items = [(bb, h) for bb in range(nb) for h in range(ML_HEADS)]
    st = {}
    for bb in range(nb):
        for pair in range(n_pairs):
            lo = pair * w_pair
            st[bb, pair, "q"] = qk[bb][:, lo:lo + w_pair]
            st[bb, pair, "kt"] = kt[bb][lo:lo + w_pair, :]
            st[bb, pair, "kbt"] = st[bb, pair, "kt"].astype(BF16)
            st[bb, pair, "v"] = v_ref[bb, :, lo:lo + w_pair]
    for (bb, h) in items:
        pair, par = divmod(h, 2)
        st[bb, h, "q_m"] = jnp.where(owns[par], st[bb, pair, "q"], 0.0).astype(BF16)
        st[bb, h, "vx"] = jnp.where(owns[par], st[bb, pair, "v"], 1.0).astype(BF16)
    for (bb, h) in items:
        st[bb, h, "s_qk"] = jnp.dot(st[bb, h, "q_m"], st[bb, h // 2, "kbt"],
                                    preferred_element_type=F32)
    for (bb, h) in items:
        st[bb, h, "ct_prev"] = ct_ref[bb, h]
        st[bb, h, "inter"] = jnp.dot(st[bb, h, "q_m"], st[bb, h, "ct_prev"].astype(BF16),
                                     preferred_element_type=F32)

    head_row = lax.broadcasted_iota(jnp.int32, (SUBLANES, t), 0) < ML_HEADS
    lane_t = lax.broadcasted_iota(jnp.int32, (SUBLANES, t), 1)
    def last_lane(x):
        return jnp.broadcast_to(x[:, t - 1:t], x.shape)

    r_t, p_end_t, a_st_t, e_st_t, g_hl = [], [], [], [], []
    for bb in range(nb):
        b_t = pltpu.roll(cums[bb], ML_HEADS, 0)
        b_t = jnp.where(head_row, b_t, 0.0)
        r = jnp.where(head_row, gpre_t[bb] - b_t, 0.0)
        cm = r
        shift = 1
        while shift < t:
            cm = jnp.where(lane_t >= shift, jnp.maximum(cm, pltpu.roll(cm, shift, 1)), cm)
            shift *= 2
        m_prev = m_ref[bb]
        u = jnp.maximum(m_prev, cm)
        a_inter = jnp.exp(m_prev - u)
        floor = jnp.exp(-(b_t + u))
        g_end = last_lane(b_t)
        cm_end = last_lane(cm)
        p_end = jnp.exp(r - cm_end)
        m_loc = g_end + cm_end
        m_new = jnp.maximum(g_end + m_prev, m_loc)
        a_st_t.append(jnp.exp(g_end + m_prev - m_new))
        e_st_t.append(jnp.exp(m_loc - m_new))
        m_ref[bb] = jnp.where(head_row, m_new, 0.0)
        r_t.append(r)
        p_end_t.append(p_end)
        parts = []
        for tile in (u, a_inter, floor):
            hi = tile.astype(BF16)
            parts += [hi, (tile - hi.astype(F32)).astype(BF16)]
        g_hl.append(jnp.concatenate(parts, axis=0))
    rep = {(bb, h): lax.dot_general(g_hl[bb], sel_ref[h], tn, preferred_element_type=F32)
           for bb in range(nb) for h in range(ML_HEADS)}

    def tile_rows(x_row):
        return jnp.tile(jnp.broadcast_to(x_row, (SUBLANES, LANES)), (t // SUBLANES, 1))

    for (bb, h) in items:
        pair, par = divmod(h, 2)
        u_col = rep[bb, h][:, 0:LANES]
        st[bb, h, "a_inter"] = rep[bb, h][:, LANES:2 * LANES]
        st[bb, h, "floor"] = rep[bb, h][:, 2 * LANES:3 * LANES]
        r_row = tile_rows(r_t[bb][h:h + 1, :])
        st[bb, h, "dexp"] = jnp.exp(jnp.where(causal, r_row - u_col, -jnp.inf))
        kp_t = (jnp.where(own_rows[par], st[bb, pair, "kt"], 0.0)
                * tile_rows(p_end_t[bb][h:h + 1, :])).astype(BF16)
        c_loc = jnp.dot(kp_t, st[bb, h, "vx"], preferred_element_type=F32)
        ct_ref[bb, h] = (tile_rows(a_st_t[bb][h:h + 1, :]) * st[bb, h, "ct_prev"]
                         + tile_rows(e_st_t[bb][h:h + 1, :]) * c_loc)

    for (bb, h) in items:
        sc = (st[bb, h, "s_qk"] * st[bb, h, "dexp"]).astype(BF16)
        st[bb, h, "intra"] = jnp.dot(sc, st[bb, h, "vx"], preferred_element_type=F32)
    def half_mean(x):
        hi = x.astype(BF16)
        lo = (x - hi.astype(F32)).astype(BF16)
        return jnp.dot(jnp.concatenate([hi, lo], axis=1), avg_ref[...],
                       preferred_element_type=F32)

    for bb in range(nb):
        for pair in range(n_pairs):
            lo = pair * w_pair
            h0, h1 = 2 * pair, 2 * pair + 1
            nd0 = st[bb, h0, "a_inter"] * st[bb, h0, "inter"] + st[bb, h0, "intra"]
            nd1 = st[bb, h1, "a_inter"] * st[bb, h1, "inter"] + st[bb, h1, "intra"]
            den = pltpu.roll(jnp.where(low, nd1, nd0), ML_HEAD_DIM, 1)
            floor = jnp.where(low, st[bb, h0, "floor"], st[bb, h1, "floor"])
            h_pair = jnp.where(low, nd0, nd1) / jnp.maximum(jnp.abs(den), floor)
            x = h_pair * _sigmoid(o_ref[bb, :, lo:lo + w_pair])
            xc = x - half_mean(x)
            var = half_mean(xc * xc)
            y = xc * lax.rsqrt(var + LN_EPS) * ng_ref[:, lo:lo + w_pair]
            out_ref[bb, :, lo:lo + w_pair] = y.astype(BF16)


def _mlstm(mq, mkt, mv, mo, gates, cw, cb, cwk, cbk, gb, ng):
    bsz, s, wqk = mq.shape
    wv = mv.shape[2]
    wk = mkt.shape[2]
    t = ML_CHUNK
    nb = math.gcd(bsz, ML_BATCH)
    halo_per_chunk = t // SUBLANES
    n_rep = 3
    src = jnp.arange(n_rep * 2 * SUBLANES)[:, None]
    dst = jnp.arange(n_rep * LANES)[None, :] // LANES
    sel = jnp.stack([((src // (2 * SUBLANES) == dst) & (src % SUBLANES == h)).astype(BF16)
                     for h in range(ML_HEADS)])
    grp = lambda idx: (idx % LANES) // ML_HEAD_DIM
    avg = jnp.where(grp(jnp.arange(2 * LANES))[:, None] == grp(jnp.arange(LANES))[None, :],
                    1.0 / ML_HEAD_DIM, 0.0).astype(BF16)
    const = lambda bi, ci: (0, 0)
    return pl.pallas_call(
        _mlstm_kernel,
        grid=(bsz // nb, s // t),
        in_specs=[
            pl.BlockSpec((nb, t, wqk), lambda bi, ci: (bi, ci, 0)),
            pl.BlockSpec((nb, SUBLANES, wqk),
                         lambda bi, ci: (bi, jnp.maximum(ci * halo_per_chunk - 1, 0), 0)),
            pl.BlockSpec((nb, 1, wk, t), lambda bi, ci: (bi, ci, 0, 0)),
            pl.BlockSpec((nb, 1, wk, t), lambda bi, ci: (bi, jnp.maximum(ci - 1, 0), 0, 0)),
            pl.BlockSpec((nb, t, wv), lambda bi, ci: (bi, ci, 0)),
            pl.BlockSpec((nb, t, wv), lambda bi, ci: (bi, ci, 0)),
            pl.BlockSpec((nb, t, LANES), lambda bi, ci: (bi, ci, 0)),
            pl.BlockSpec(cw.shape, const),
            pl.BlockSpec(cb.shape, const),
            pl.BlockSpec(cwk.shape, lambda bi, ci: (0, 0, 0)),
            pl.BlockSpec(cbk.shape, const),
            pl.BlockSpec(gb.shape, const),
            pl.BlockSpec(ng.shape, const),
            pl.BlockSpec(sel.shape, lambda bi, ci: (0, 0, 0)),
            pl.BlockSpec(avg.shape, const),
        ],
        out_specs=pl.BlockSpec((nb, t, wv), lambda bi, ci: (bi, ci, 0)),
        out_shape=jax.ShapeDtypeStruct((bsz, s, wv), BF16),
        scratch_shapes=[pltpu.VMEM((nb, SUBLANES + t, wqk), F32),
                        pltpu.VMEM((nb, ML_HEADS, 2 * ML_HEAD_DIM, LANES), F32),
                        pltpu.VMEM((nb, SUBLANES, LANES), F32)],
        compiler_params=_params("parallel", "arbitrary"),
        name="mlstm",
    )(mq, mq, mkt, mkt, mv, mo, gates, cw, cb, cwk, cbk, gb, ng, sel, avg)


def _attn_kernel(q_ref, qn_ref, k_ref, vt_ref, lam_ref, ng_ref, out_ref, qall_ref, qm_ref,
                 acc_ref, s0_ref, s1_ref, mblk_ref, m_ref, *, lambda_init):
    tq = qn_ref.shape[1]
    tk = vt_ref.shape[3]
    n_sub = q_ref.shape[1] // tq
    for t in range(n_sub):
        qall_ref[t] = q_ref[0, t * tq:(t + 1) * tq, :]
    qall_ref[n_sub] = qn_ref[0]
    _attn_init_block(qall_ref[0], qm_ref, acc_ref, m_ref)

    def query_block(t, carry):
        _attn_query_block(pl.program_id(2) * n_sub + t, qall_ref.at[t], qall_ref.at[t + 1],
                          k_ref, vt_ref, lam_ref, ng_ref,
                          out_ref.at[0, pl.ds(pl.multiple_of(t * tq, tq), tq), :],
                          qm_ref, acc_ref, s0_ref, s1_ref, mblk_ref, m_ref, lambda_init)
        return carry

    lax.fori_loop(0, n_sub, query_block, 0)


def _attn_map_operands(q):
    lane = lax.broadcasted_iota(jnp.int32, q.shape, 1)
    zero = jnp.zeros_like(q)
    return jnp.where(lane < DA_QK_DIM, q, zero), jnp.where(lane >= DA_QK_DIM, q, zero)


def _attn_init_block(q, qm_ref, acc_ref, m_ref):
    qm_ref[0], qm_ref[1] = _attn_map_operands(q)
    acc_ref[...] = jnp.zeros_like(acc_ref)
    m_ref[...] = jnp.full_like(m_ref, -jnp.inf)


def _attn_query_block(i, q_ref, qn_ref, k_ref, vt_ref, lam_ref, ng_ref, out_ref, qm_ref,
                      acc_ref, s0_ref, s1_ref, mblk_ref, m_ref, lambda_init):
    tq = q_ref.shape[0]
    tk = vt_ref.shape[3]
    map_operands = _attn_map_operands

    ones_rows = jnp.ones((ATT_SUM_ROWS, tk), BF16)

    def scores(j, s_ref, slot, qms=None):
        k_blk = k_ref[0, pl.ds(pl.multiple_of(j * tk, tk), tk), :]
        for mp in range(2):
            qm = qm_ref[mp] if qms is None else qms[mp]
            st = lax.dot_general(k_blk, qm, (((1,), (1,)), ((), ())),
                                 preferred_element_type=F32)
            s_ref[mp] = st
            mblk_ref[slot, mp] = jnp.max(st, axis=0, keepdims=True)

    def softmax_pv(j, s_ref, slot, masked):
        vt_blk = jnp.concatenate([vt_ref[0, j], ones_rows], axis=0)
        if masked:
            key = lax.broadcasted_iota(jnp.int32, (tk, tq), 0)
            qry = lax.broadcasted_iota(jnp.int32, (tk, tq), 1)
            keep = key <= qry
        for mp in range(2):
            st = s_ref[mp]
            if masked:
                st = jnp.where(keep, st, -jnp.inf)
                m_blk = jnp.max(st, axis=0, keepdims=True)
            else:
                m_blk = mblk_ref[slot, mp]
            m_old = m_ref[mp]
            m_new = jnp.maximum(m_old, m_blk)
            alpha = jnp.exp2(m_old - m_new)
            pexp = jnp.exp2(st - m_new).astype(BF16)
            acc_ref[mp] = alpha * acc_ref[mp] + jnp.dot(vt_blk, pexp,
                                                       preferred_element_type=F32)
            m_ref[mp] = m_new

    def run(bufs):
        @pl.when(i == 0)
        def _():
            scores(0, *bufs[0])

        def advance(first, count):
            for u in range(count):
                scores(first + u + 1, *bufs[(u + 1) % 2])
                softmax_pv(first + u, *bufs[u % 2], False)

        def group(jj, carry):
            advance(ATT_UNROLL * jj, ATT_UNROLL)
            return carry

        lax.fori_loop(0, i // ATT_UNROLL, group, 0)

        for rest in range(ATT_UNROLL):
            @pl.when(i % ATT_UNROLL == rest)
            def _(rest=rest):
                advance(i - rest, rest)
                scores(0, *bufs[(rest + 1) % 2], map_operands(qn_ref[...]))
                softmax_pv(i, *bufs[rest % 2], True)

    start = ((i + 1) // 2) % 2
    buf0, buf1 = (s0_ref, 0), (s1_ref, 1)

    @pl.when(start == 0)
    def _():
        run((buf0, buf1))

    @pl.when(start == 1)
    def _():
        run((buf1, buf0))

    lp = lam_ref[...]
    lam = (jnp.exp(jnp.sum(lp[0:1, :] * lp[1:2, :], axis=-1, keepdims=True))
           - jnp.exp(jnp.sum(lp[2:3, :] * lp[3:4, :], axis=-1, keepdims=True))
           + lambda_init)
    dv = DA_V_DIM
    o_t = (acc_ref[0, 0:dv, :] / acc_ref[0, dv:dv + 1, :]
           - lam * (acc_ref[1, 0:dv, :] / acc_ref[1, dv:dv + 1, :]))
    ms = jnp.mean(o_t * o_t, axis=0, keepdims=True)
    y_t = o_t * lax.rsqrt(ms + LN_EPS) * (ng_ref[...] * (1.0 - lambda_init))
    out_ref[...] = y_t.T.astype(BF16)
    _attn_init_block(qn_ref[...], qm_ref, acc_ref, m_ref)


def _diff_attention(dq, dk, dvt, lam_p, ng_col, lambda_init):
    bsz, s, _ = dq.shape
    tq = ATT_Q
    nk, tk = dvt.shape[1], dvt.shape[3]
    assert tq == tk and nk * tk == s
    nq = s // tq
    g = math.gcd(nq, ATT_STEP_BLOCKS)
    return pl.pallas_call(
        functools.partial(_attn_kernel, lambda_init=lambda_init),
        grid=(bsz, DA_HEADS, nq // g),
        in_specs=[
            pl.BlockSpec((1, g * tq, 2 * DA_QK_DIM), lambda b, h, i: (b, i, h)),
            pl.BlockSpec((1, tq, 2 * DA_QK_DIM),
                         lambda b, h, i: (b, jnp.minimum((i + 1) * g, nq - 1), h)),
            pl.BlockSpec((1, s, 2 * DA_QK_DIM), lambda b, h, i: (b, 0, h)),
            pl.BlockSpec((1, nk, DA_V_DIM, tk), lambda b, h, i: (b, 0, h, 0)),
            pl.BlockSpec(lam_p.shape, lambda b, h, i: (0, 0)),
            pl.BlockSpec(ng_col.shape, lambda b, h, i: (0, 0)),
        ],
        out_specs=pl.BlockSpec((1, g * tq, DA_V_DIM), lambda b, h, i: (b, i, h)),
        out_shape=jax.ShapeDtypeStruct((bsz, s, DA_HEADS * DA_V_DIM), BF16),
        scratch_shapes=[pltpu.VMEM((g + 1, tq, 2 * DA_QK_DIM), BF16),
                        pltpu.VMEM((2, tq, 2 * DA_QK_DIM), BF16),
                        pltpu.VMEM((2, DA_V_DIM + ATT_SUM_ROWS, tq), F32),
                        pltpu.VMEM((2, tk, tq), F32),
                        pltpu.VMEM((2, tk, tq), F32),
                        pltpu.VMEM((2, 2, 1, tq), F32),
                        pltpu.VMEM((2, 1, tq), F32)],
        compiler_params=_params("parallel", "parallel", "arbitrary"),
        name="diff_attention",
    )(dq, dq, dk, dvt, lam_p, ng_col)


def _out_mlp_kernel(yc_ref, ym_ref, yd_ref, x_ref, wo_ref, g1_ref, b1_ref, wu_ref, wd_ref,
                    g2_ref, b2_ref, out_ref, x1_ref, xb_ref, acc_ref):
    f = pl.program_id(1)
    last = pl.num_programs(1) - 1
    wc = yc_ref.shape[1]
    wm = ym_ref.shape[1]

    def mix(r):
        rows = pl.ds(r, MLP_SUB)
        h = jnp.dot(yc_ref[rows, :], wo_ref[0:wc, :], preferred_element_type=F32)
        h = h + jnp.dot(ym_ref[rows, :], wo_ref[wc:wc + wm, :], preferred_element_type=F32)
        return h + jnp.dot(yd_ref[rows, :], wo_ref[wc + wm:, :], preferred_element_type=F32)

    def step(first, final):
        n_rows = x_ref.shape[0]
        if first:
            h_next = mix(0)
        for r in range(0, n_rows, MLP_SUB):
            rows = pl.ds(r, MLP_SUB)
            if first:
                h = h_next
                if r + MLP_SUB < n_rows:
                    h_next = mix(r + MLP_SUB)
                x1 = _layer_norm_rows(DEEPNORM_ALPHA * x_ref[rows, :] + h,
                                      g1_ref[...], b1_ref[...])
                x1_ref[rows, :] = x1
                xb = x1.astype(BF16)
                xb_ref[rows, :] = xb
            else:
                xb = xb_ref[rows, :]
            up = jnp.maximum(jnp.dot(xb, wu_ref[...], preferred_element_type=F32), 0.0)
            down = jnp.dot((up * up).astype(BF16), wd_ref[...], preferred_element_type=F32)
            acc = down if first else acc_ref[rows, :] + down
            if final:
                out_ref[rows, :] = _layer_norm_rows(DEEPNORM_ALPHA * x1_ref[rows, :] + acc,
                                                    g2_ref[...], b2_ref[...])
            else:
                acc_ref[rows, :] = acc

    @pl.when(f == 0)
    def _():
        step(True, False)

    @pl.when(jnp.logical_and(f > 0, f < last))
    def _():
        step(False, False)

    @pl.when(f == last)
    def _():
        step(False, True)


def _out_mlp(yc, ym, yd, x2, wo, g1, b1, wu, wd, g2, b2):
    n, d = x2.shape
    dff = wu.shape[1]
    tm, tf = MLP_ROWS, MLP_FF
    assert dff // tf >= 2, "the first and last hidden steps are distinct code paths"
    rows = lambda i, f: (i, 0)
    const = lambda i, f: (0, 0)
    return pl.pallas_call(
        _out_mlp_kernel,
        grid=(n // tm, dff // tf),
        in_specs=[pl.BlockSpec((tm, yc.shape[1]), rows),
                  pl.BlockSpec((tm, ym.shape[1]), rows),
                  pl.BlockSpec((tm, yd.shape[1]), rows),
                  pl.BlockSpec((tm, d), rows),
                  pl.BlockSpec(wo.shape, const),
                  pl.BlockSpec(g1.shape, const),
                  pl.BlockSpec(b1.shape, const),
                  pl.BlockSpec((d, tf), lambda i, f: (0, f)),
                  pl.BlockSpec((tf, d), lambda i, f: (f, 0)),
                  pl.BlockSpec(g2.shape, const),
                  pl.BlockSpec(b2.shape, const)],
        out_specs=pl.BlockSpec((tm, d), rows),
        out_shape=jax.ShapeDtypeStruct((n, d), F32),
        scratch_shapes=[pltpu.VMEM((tm, d), F32), pltpu.VMEM((tm, d), BF16),
                        pltpu.VMEM((tm, d), F32)],
        compiler_params=_params("parallel", "arbitrary"),
        name="out_mlp_ln",
    )(yc, ym, yd, x2, wo, g1, b1, wu, wd, g2, b2)


def _row(v):
    return v.reshape(1, -1).astype(F32)


def _pad_rows(w, rows):
    return jnp.pad(w.astype(F32), ((0, rows - w.shape[0]), (0, 0)))


def _rearranged_w_in(w):
    d = w.shape[0]
    mk_lo = 256 + 256 + ML_HEADS * ML_HEAD_DIM
    mk_hi = mk_lo + ML_HEADS * ML_HEAD_DIM
    gate_lo = 256 + 256 + 512 + 256 + 256
    gate_hi = gate_lo + 2 * ML_HEADS
    v_lo = gate_hi + 2 * DA_HEADS * 2 * DA_QK_DIM
    pad = jnp.zeros((d, LANES - 2 * ML_HEADS), w.dtype)
    w_r = jnp.concatenate([w[:, :gate_lo], w[:, gate_hi:v_lo], w[:, gate_lo:gate_hi], pad], axis=1)
    return w_r.astype(BF16), w[:, v_lo:].T.astype(BF16), w[:, mk_lo:mk_hi].T.astype(BF16)


def kernel(x, w_in, b_igate, b_fgate, conv_dw_w, conv_dw_b, conv_ln_g, conv_ln_b, conv_pw_w, conv_pw_b, ml_conv_w, ml_conv_b, ml_norm_g, lam_q1, lam_k1, lam_q2, lam_k2, da_norm_g, w_out, ln1_g, ln1_b, w_up, w_down, ln2_g, ln2_b):
    bsz, s, d = x.shape
    n = bsz * s
    x2 = x.reshape(n, d)
    for l in range(DEPTH):
        lambda_init = 0.8 - 0.6 * math.exp(-0.3 * l)
        w_r, w_vt, w_kt = _rearranged_w_in(w_in[l])
        conv_params = (_pad_rows(conv_dw_w[l], 32), _row(conv_dw_b[l]), _row(conv_ln_g[l]),
                       _row(conv_ln_b[l]), conv_pw_w[l].astype(BF16), _row(conv_pw_b[l]))
        y_conv, mq, mv, mo, gates, dq, dk, dvt, mkt = _in_proj(x2, w_r, w_vt, w_kt,
                                                               conv_params, bsz)
        seq = lambda a: a.reshape(bsz, s, a.shape[1])

        gate_bias = jnp.pad(jnp.concatenate([b_igate[l], b_fgate[l]]).astype(F32),
                            (0, LANES - 2 * ML_HEADS)).reshape(1, LANES)
        wq = ML_HEADS * ML_HEAD_DIM
        cw_k = ml_conv_w[l][:, wq:].astype(F32)
        cb_k = ml_conv_b[l][wq:].astype(F32)
        y_ml = _mlstm(seq(mq), mkt, seq(mv), seq(mo), seq(gates),
                      _pad_rows(ml_conv_w[l][:, :wq], SUBLANES), _row(ml_conv_b[l][:wq]),
                      jnp.broadcast_to(cw_k[:, :, None], cw_k.shape + (ML_CHUNK,)),
                      jnp.broadcast_to(cb_k[:, None], cb_k.shape + (ML_CHUNK,)),
                      gate_bias, _row(ml_norm_g[l]))

        lam_p = jnp.stack([lam_q1[l], lam_k1[l], lam_q2[l], lam_k2[l]]).astype(F32)
        y_da = _diff_attention(seq(dq), seq(dk), dvt, lam_p,
                               da_norm_g[l].astype(F32).reshape(-1, 1), lambda_init)

        x2 = _out_mlp(y_conv.reshape(n, -1), y_ml.reshape(n, -1), y_da.reshape(n, -1), x2,
                      w_out[l].astype(BF16), _row(ln1_g[l]), _row(ln1_b[l]),
                      w_up[l].astype(BF16), w_down[l].astype(BF16),
                      _row(ln2_g[l]), _row(ln2_b[l]))
    return x2.reshape(bsz, s, d)
```

```python
import functools
import math

import jax
import jax.numpy as jnp
from jax import lax
from jax.experimental import pallas as pl
from jax.experimental.pallas import tpu as pltpu

F32 = jnp.float32
BF16 = jnp.bfloat16

LANES = 128
SUBLANES = 8
VMEM_LIMIT_BYTES = 56 * 1024 * 1024

DEPTH = 2
CONV_KSIZE = 31
ML_HEADS = 4
ML_HEAD_DIM = 64
ML_QK_CONV = 4
ML_CHUNK = 128
DA_HEADS = 4
DA_QK_DIM = 64
DA_V_DIM = 128
LN_EPS = 1e-5
DEEPNORM_ALPHA = (2 * DEPTH) ** 0.25

PROJ_ROWS = 1024
CONV_HALO = 32
CONV_SUB = 64
ATT_Q = 512
ATT_K = 512
ML_BATCH = 4
ATT_STEP_BLOCKS = 4
ATT_UNROLL = 4
ATT_SUM_ROWS = 16
MLP_ROWS = 1024
MLP_FF = 1024
MLP_SUB = 256


def _params(*semantics):
    return pltpu.CompilerParams(dimension_semantics=semantics,
                                vmem_limit_bytes=VMEM_LIMIT_BYTES)


def _sigmoid(x):
    return 1.0 / (1.0 + jnp.exp(-x))


def _layer_norm_rows(x, g, b):
    mu = jnp.mean(x, axis=-1, keepdims=True)
    xc = x - mu
    var = jnp.mean(xc * xc, axis=-1, keepdims=True)
    return xc * lax.rsqrt(var + LN_EPS) * g + b


def _in_proj_kernel(x_ref, w_ref, wvt_ref, wkt_ref, dw_ref, dwb_ref, cg_ref, cb_ref, pw_ref,
                    pwb_ref, yconv_ref, mq_ref, mv_ref, mo_ref, gate_ref, dq_ref, dk_ref,
                    dvt_ref, mkt_ref, xs, ybuf, *, tiles_per_seq):
    rows = x_ref.shape[0]
    xb = x_ref[...].astype(BF16)

    def sec(lo, width):
        return jnp.dot(xb, w_ref[:, lo:lo + width], preferred_element_type=F32)

    @pl.when(pl.program_id(0) % tiles_per_seq == 0)
    def _():
        xs[0, 0:CONV_HALO, :] = jnp.zeros((CONV_HALO, xs.shape[2]), F32)

    xs[0, CONV_HALO:CONV_HALO + rows, :] = sec(0, 256) * _sigmoid(sec(256, 256))
    span = CONV_HALO + rows - SUBLANES
    for k in range(1, SUBLANES):
        xs[k, 0:span, :] = xs[0, k:k + span, :]

    def conv_block(r, taps):
        first = CONV_HALO - (CONV_KSIZE - 1)
        acc = jnp.broadcast_to(dwb_ref[...], (CONV_SUB, xs.shape[2]))
        for j in range(CONV_KSIZE):
            k = (first + j) % SUBLANES
            lo = r + first + j - k
            acc = acc + taps[j:j + 1, :] * xs[k, lo:lo + CONV_SUB, :]
        y = _layer_norm_rows(acc, cg_ref[...], cb_ref[...])
        ybuf[r:r + CONV_SUB, :] = (y * _sigmoid(y)).astype(BF16)

    def proj_mqk():
        res = sec(512, 256)
        mq_ref[...] = res
        kt = lax.dot_general(wkt_ref[...], xb, (((1,), (1,)), ((), ())),
                             preferred_element_type=F32)
        for ck in range(mkt_ref.shape[1]):
            mkt_ref[0, ck] = kt[:, ck * ML_CHUNK:(ck + 1) * ML_CHUNK]
        return res

    def proj_mv_mo():
        mv_ref[...] = sec(1024, 256)
        res = sec(1280, 256)
        mo_ref[...] = res
        return res

    def proj_dq():
        res = sec(1536, 512)
        dq_ref[...] = (res * (DA_QK_DIM ** -0.5 * math.log2(math.e))).astype(BF16)
        return res

    def proj_dk_gate_vt():
        res = sec(2048, 512)
        dk_ref[...] = res.astype(BF16)
        gate_ref[...] = sec(2560, LANES)
        vt = lax.dot_general(wvt_ref[...], xb, (((1,), (1,)), ((), ())),
                             preferred_element_type=F32).astype(BF16)
        for kb in range(dvt_ref.shape[1]):
            dvt_ref[0, kb] = vt[:, kb * ATT_K:(kb + 1) * ATT_K]
        return res

    blocks_per_stage = rows // CONV_SUB // 4
    stages = [(proj, blocks_per_stage)
              for proj in (proj_mqk, proj_mv_mo, proj_dq, proj_dk_gate_vt)]

    def exact_zero(x):
        bits = lax.shift_right_logical(pltpu.bitcast(x, jnp.uint32), jnp.uint32(32))
        return pltpu.bitcast(bits, F32)

    pw_rows = blocks_per_stage * CONV_SUB
    taps = dw_ref[...]
    done = 0
    for idx, (piece, n_blocks) in enumerate(stages):
        res = piece()
        for _ in range(n_blocks):
            conv_block(done, taps)
            done += CONV_SUB
            if done % pw_rows == 0:
                r0 = done - pw_rows
                yconv_ref[r0:done, :] = (jnp.dot(ybuf[r0:done, :], pw_ref[...],
                                                 preferred_element_type=F32)
                                         + pwb_ref[...]).astype(BF16)
        if idx + 1 < len(stages) and stages[idx + 1][1]:
            taps = dw_ref[...] + exact_zero(res[res.shape[0] - 1:, 0:xs.shape[2]])
    xs[0, 0:CONV_HALO, :] = xs[0, rows:rows + CONV_HALO, :]


def _in_proj(x2, w_r, w_vt, w_kt, conv_params, bsz):
    n, d = x2.shape
    tm = PROJ_ROWS
    kb_per_tile = tm // ATT_K
    ck_per_tile = tm // ML_CHUNK
    tiles_per_seq = n // bsz // tm
    widths = (256, 256, 256, 256, LANES, 512, 512)
    dtypes = (BF16, F32, F32, F32, F32, BF16, BF16)
    wv = w_vt.shape[0]
    wk = w_kt.shape[0]
    const = lambda i: (0, 0)
    slabs = lambda i: (i // tiles_per_seq, i % tiles_per_seq, 0, 0)
    return pl.pallas_call(
        functools.partial(_in_proj_kernel, tiles_per_seq=tiles_per_seq),
        grid=(n // tm,),
        in_specs=[pl.BlockSpec((tm, d), lambda i: (i, 0)),
                  pl.BlockSpec(w_r.shape, const),
                  pl.BlockSpec(w_vt.shape, const),
                  pl.BlockSpec(w_kt.shape, const)]
        + [pl.BlockSpec(p.shape, const) for p in conv_params],
        out_specs=[pl.BlockSpec((tm, w), lambda i: (i, 0)) for w in widths]
        + [pl.BlockSpec((1, kb_per_tile, wv, ATT_K), slabs),
           pl.BlockSpec((1, ck_per_tile, wk, ML_CHUNK), slabs)],
        out_shape=[jax.ShapeDtypeStruct((n, w), dt) for w, dt in zip(widths, dtypes)]
        + [jax.ShapeDtypeStruct((bsz, tiles_per_seq * kb_per_tile, wv, ATT_K), BF16),
           jax.ShapeDtypeStruct((bsz, tiles_per_seq * ck_per_tile, wk, ML_CHUNK), F32)],
        scratch_shapes=[pltpu.VMEM((SUBLANES, CONV_HALO + tm, widths[0]), F32),
                        pltpu.VMEM((tm, widths[0]), BF16)],
        compiler_params=_params("arbitrary"),
        name="in_proj_conv",
    )(x2, w_r, w_vt, w_kt, *conv_params)


def _mlstm_kernel(qk_ref, halo_ref, kt_ref, ktp_ref, v_ref, o_ref, gate_ref, cw_ref, cb_ref,
                  cwk_ref, cbk_ref, gb_ref, ng_ref, sel_ref, avg_ref, out_ref, qpad, ct_ref,
                  m_ref):
    c = pl.program_id(1)
    nb = qk_ref.shape[0]
    t = ML_CHUNK
    w_pair = 2 * ML_HEAD_DIM
    n_pairs = ML_HEADS // 2
    tn = (((0,), (0,)), ((), ()))

    @pl.when(c == 0)
    def _():
        ct_ref[...] = jnp.zeros_like(ct_ref)
        m_ref[...] = jnp.zeros_like(m_ref)

    row = lax.broadcasted_iota(jnp.int32, (t, t), 0)
    col = lax.broadcasted_iota(jnp.int32, (t, t), 1)
    causal = row >= col
    tri = causal.astype(F32)
    lane = lax.broadcasted_iota(jnp.int32, (t, w_pair), 1)
    low = lane < ML_HEAD_DIM
    owns = (low, lane >= ML_HEAD_DIM)
    own_rows = (row < ML_HEAD_DIM, row >= ML_HEAD_DIM)

    first = SUBLANES - (ML_QK_CONV - 1)
    lane_k = lax.broadcasted_iota(jnp.int32, (kt_ref.shape[2], t), 1)
    qk, kt, gpre_t, log_f_t = [], [], [], []
    for bb in range(nb):
        qpad[bb, 0:SUBLANES, :] = jnp.where(c > 0, halo_ref[bb], 0.0)
        qpad[bb, SUBLANES:SUBLANES + t, :] = qk_ref[bb]
        acc = jnp.broadcast_to(cb_ref[...], (t, qk_ref.shape[2]))
        for j in range(ML_QK_CONV):
            acc = acc + cw_ref[j:j + 1, :] * qpad[bb, first + j:first + j + t, :]
        qk.append(acc * _sigmoid(acc))
        cur = kt_ref[bb, 0]
        prev = jnp.where(c > 0, ktp_ref[bb, 0], 0.0)
        acc = cbk_ref[...] + cwk_ref[ML_QK_CONV - 1] * cur
        for s in range(1, ML_QK_CONV):
            shifted = jnp.where(lane_k >= s, pltpu.roll(cur, s, 1), pltpu.roll(prev, s, 1))
            acc = acc + cwk_ref[ML_QK_CONV - 1 - s] * shifted
        kt.append(acc * _sigmoid(acc) * (ML_HEAD_DIM ** -0.5))
        g_t = (gate_ref[bb] + gb_ref[...]).T[0:SUBLANES, :]
        gpre_t.append(g_t)
        log_f_t.append(jnp.minimum(g_t, 0.0) - jnp.log(1.0 + jnp.exp(-jnp.abs(g_t))))

    tri_t = (row <= col).astype(F32)
    cums = [jnp.dot(lf, tri_t, preferred_element_type=F32, precision=lax.Precision.HIGHEST)
            for lf in log_f_t]

    items = [(bb, h) for bb in range(nb) for h in range(ML_HEADS)]
    st = {}
    for bb in range(nb):
        for pair in range(n_pairs):
            lo = pair * w_pair
            st[bb, pair, "q"] = qk[bb][:, lo:lo + w_pair]
            st[bb, pair, "kt"] = kt[bb][lo:lo + w_pair, :]
            st[bb, pair, "kbt"] = st[bb, pair, "kt"].astype(BF16)
            st[bb, pair, "v"] = v_ref[bb, :, lo:lo + w_pair]
    for (bb, h) in items:
        pair, par = divmod(h, 2)
        st[bb, h, "q_m"] = jnp.where(owns[par], st[bb, pair, "q"], 0.0).astype(BF16)
        st[bb, h, "vx"] = jnp.where(owns[par], st[bb, pair, "v"], 1.0).astype(BF16)
    for (bb, h) in items:
        st[bb, h, "s_qk"] = jnp.dot(st[bb, h, "q_m"], st[bb, h // 2, "kbt"],
                                    preferred_element_type=F32)
    for (bb, h) in items:
        st[bb, h, "ct_prev"] = ct_ref[bb, h]
        st[bb, h, "inter"] = jnp.dot(st[bb, h, "q_m"], st[bb, h, "ct_prev"].astype(BF16),
                                     preferred_element_type=F32)

    head_row = lax.broadcasted_iota(jnp.int32, (SUBLANES, t), 0) < ML_HEADS
    lane_t = lax.broadcasted_iota(jnp.int32, (SUBLANES, t), 1)
    def last_lane(x):
        return jnp.broadcast_to(x[:, t - 1:t], x.shape)

    r_t, p_end_t, a_st_t, e_st_t, g_hl = [], [], [], [], []
    for bb in range(nb):
        b_t = pltpu.roll(cums[bb], ML_HEADS, 0)
        b_t = jnp.where(head_row, b_t, 0.0)
        r = jnp.where(head_row, gpre_t[bb] - b_t, 0.0)
        cm = r
        shift = 1
        while shift < t:
            cm = jnp.where(lane_t >= shift, jnp.maximum(cm, pltpu.roll(cm, shift, 1)), cm)
            shift *= 2
        m_prev = m_ref[bb]
        u = jnp.maximum(m_prev, cm)
        a_inter = jnp.exp(m_prev - u)
        floor = jnp.exp(-(b_t + u))
        g_end = last_lane(b_t)
        cm_end = last_lane(cm)
        p_end = jnp.exp(r - cm_end)
        m_loc = g_end + cm_end
        m_new = jnp.maximum(g_end + m_prev, m_loc)
        a_st_t.append(jnp.exp(g_end + m_prev - m_new))
        e_st_t.append(jnp.exp(m_loc - m_new))
        m_ref[bb] = jnp.where(head_row, m_new, 0.0)
        r_t.append(r)
        p_end_t.append(p_end)
        parts = []
        for tile in (u, a_inter, floor):
            hi = tile.astype(BF16)
            parts += [hi, (tile - hi.astype(F32)).astype(BF16)]
        g_hl.append(jnp.concatenate(parts, axis=0))
    rep = {(bb, h): lax.dot_general(g_hl[bb], sel_ref[h], tn, preferred_element_type=F32)
           for bb in range(nb) for h in range(ML_HEADS)}

    def tile_rows(x_row):
        return jnp.tile(jnp.broadcast_to(x_row, (SUBLANES, LANES)), (t // SUBLANES, 1))

    for (bb, h) in items:
        pair, par = divmod(h, 2)
        u_col = rep[bb, h][:, 0:LANES]
        st[bb, h, "a_inter"] = rep[bb, h][:, LANES:2 * LANES]
        st[bb, h, "floor"] = rep[bb, h][:, 2 * LANES:3 * LANES]
        r_row = tile_rows(r_t[bb][h:h + 1, :])
        st[bb, h, "dexp"] = jnp.exp(jnp.where(causal, r_row - u_col, -jnp.inf))
        kp_t = (jnp.where(own_rows[par], st[bb, pair, "kt"], 0.0)
                * tile_rows(p_end_t[bb][h:h + 1, :])).astype(BF16)
        c_loc = jnp.dot(kp_t, st[bb, h, "vx"], preferred_element_type=F32)
        ct_ref[bb, h] = (tile_rows(a_st_t[bb][h:h + 1, :]) * st[bb, h, "ct_prev"]
                         + tile_rows(e_st_t[bb][h:h + 1, :]) * c_loc)

    for (bb, h) in items:
        sc = (st[bb, h, "s_qk"] * st[bb, h, "dexp"]).astype(BF16)
        st[bb, h, "intra"] = jnp.dot(sc, st[bb, h, "vx"], preferred_element_type=F32)
    def half_mean(x):
        hi = x.astype(BF16)
        lo = (x - hi.astype(F32)).astype(BF16)
        return jnp.dot(jnp.concatenate([hi, lo], axis=1), avg_ref[...],
                       preferred_element_type=F32)

    for bb in range(nb):
        for pair in range(n_pairs):
            lo = pair * w_pair
            h0, h1 = 2 * pair, 2 * pair + 1
            nd0 = st[bb, h0, "a_inter"] * st[bb, h0, "inter"] + st[bb, h0, "intra"]
            nd1 = st[bb, h1, "a_inter"] * st[bb, h1, "inter"] + st[bb, h1, "intra"]
            den = pltpu.roll(jnp.where(low, nd1, nd0), ML_HEAD_DIM, 1)
            floor = jnp.where(low, st[bb, h0, "floor"], st[bb, h1, "floor"])
            h_pair = jnp.where(low, nd0, nd1) / jnp.maximum(jnp.abs(den), floor)
            x = h_pair * _sigmoid(o_ref[bb, :, lo:lo + w_pair])
            xc = x - half_mean(x)
            var = half_mean(xc * xc)
            y = xc * lax.rsqrt(var + LN_EPS) * ng_ref[:, lo:lo + w_pair]
            out_ref[bb, :, lo:lo + w_pair] = y.astype(BF16)


def _mlstm(mq, mkt, mv, mo, gates, cw, cb, cwk, cbk, gb, ng):
    bsz, s, wqk = mq.shape
    wv = mv.shape[2]
    wk = mkt.shape[2]
    t = ML_CHUNK
    nb = math.gcd(bsz, ML_BATCH)
    halo_per_chunk = t // SUBLANES
    n_rep = 3
    src = jnp.arange(n_rep * 2 * SUBLANES)[:, None]
    dst = jnp.arange(n_rep * LANES)[None, :] // LANES
    sel = jnp.stack([((src // (2 * SUBLANES) == dst) & (src % SUBLANES == h)).astype(BF16)
                     for h in range(ML_HEADS)])
    grp = lambda idx: (idx % LANES) // ML_HEAD_DIM
    avg = jnp.where(grp(jnp.arange(2 * LANES))[:, None] == grp(jnp.arange(LANES))[None, :],
                    1.0 / ML_HEAD_DIM, 0.0).astype(BF16)
    const = lambda bi, ci: (0, 0)
    return pl.pallas_call(
        _mlstm_kernel,
        grid=(bsz // nb, s // t),
        in_specs=[
            pl.BlockSpec((nb, t, wqk), lambda bi, ci: (bi, ci, 0)),
            pl.BlockSpec((nb, SUBLANES, wqk),
                         lambda bi, ci: (bi, jnp.maximum(ci * halo_per_chunk - 1, 0), 0)),
            pl.BlockSpec((nb, 1, wk, t), lambda bi, ci: (bi, ci, 0, 0)),
            pl.BlockSpec((nb, 1, wk, t), lambda bi, ci: (bi, jnp.maximum(ci - 1, 0), 0, 0)),
            pl.BlockSpec((nb, t, wv), lambda bi, ci: (bi, ci, 0)),
            pl.BlockSpec((nb, t, wv), lambda bi, ci: (bi, ci, 0)),
            pl.BlockSpec((nb, t, LANES), lambda bi, ci: (bi, ci, 0)),
            pl.BlockSpec(cw.shape, const),
            pl.BlockSpec(cb.shape, const),
            pl.BlockSpec(cwk.shape, lambda bi, ci: (0, 0, 0)),
            pl.BlockSpec(cbk.shape, const),
            pl.BlockSpec(gb.shape, const),
            pl.BlockSpec(ng.shape, const),
            pl.BlockSpec(sel.shape, lambda bi, ci: (0, 0, 0)),
            pl.BlockSpec(avg.shape, const),
        ],
        out_specs=pl.BlockSpec((nb, t, wv), lambda bi, ci: (bi, ci, 0)),
        out_shape=jax.ShapeDtypeStruct((bsz, s, wv), BF16),
        scratch_shapes=[pltpu.VMEM((nb, SUBLANES + t, wqk), F32),
                        pltpu.VMEM((nb, ML_HEADS, 2 * ML_HEAD_DIM, LANES), F32),
                        pltpu.VMEM((nb, SUBLANES, LANES), F32)],
        compiler_params=_params("parallel", "arbitrary"),
        name="mlstm",
    )(mq, mq, mkt, mkt, mv, mo, gates, cw, cb, cwk, cbk, gb, ng, sel, avg)


def _attn_kernel(q_ref, qn_ref, k_ref, vt_ref, lam_ref, ng_ref, out_ref, qall_ref, qm_ref,
                 acc_ref, s0_ref, s1_ref, mblk_ref, m_ref, *, lambda_init):
    tq = qn_ref.shape[1]
    tk = vt_ref.shape[3]
    n_sub = q_ref.shape[1] // tq
    for t in range(n_sub):
        qall_ref[t] = q_ref[0, t * tq:(t + 1) * tq, :]
    qall_ref[n_sub] = qn_ref[0]
    _attn_init_block(qall_ref[0], qm_ref, acc_ref, m_ref)

    def query_block(t, carry):
        _attn_query_block(pl.program_id(2) * n_sub + t, qall_ref.at[t], qall_ref.at[t + 1],
                          k_ref, vt_ref, lam_ref, ng_ref,
                          out_ref.at[0, pl.ds(pl.multiple_of(t * tq, tq), tq), :],
                          qm_ref, acc_ref, s0_ref, s1_ref, mblk_ref, m_ref, lambda_init)
        return carry

    lax.fori_loop(0, n_sub, query_block, 0)


def _attn_map_operands(q):
    lane = lax.broadcasted_iota(jnp.int32, q.shape, 1)
    zero = jnp.zeros_like(q)
    return jnp.where(lane < DA_QK_DIM, q, zero), jnp.where(lane >= DA_QK_DIM, q, zero)


def _attn_init_block(q, qm_ref, acc_ref, m_ref):
    qm_ref[0], qm_ref[1] = _attn_map_operands(q)
    acc_ref[...] = jnp.zeros_like(acc_ref)
    m_ref[...] = jnp.full_like(m_ref, -jnp.inf)


def _attn_query_block(i, q_ref, qn_ref, k_ref, vt_ref, lam_ref, ng_ref, out_ref, qm_ref,
                      acc_ref, s0_ref, s1_ref, mblk_ref, m_ref, lambda_init):
    tq = q_ref.shape[0]
    tk = vt_ref.shape[3]
    map_operands = _attn_map_operands

    ones_rows = jnp.ones((ATT_SUM_ROWS, tk), BF16)

    def scores(j, s_ref, slot, qms=None):
        k_blk = k_ref[0, pl.ds(pl.multiple_of(j * tk, tk), tk), :]
        for mp in range(2):
            qm = qm_ref[mp] if qms is None else qms[mp]
            st = lax.dot_general(k_blk, qm, (((1,), (1,)), ((), ())),
                                 preferred_element_type=F32)
            s_ref[mp] = st
            mblk_ref[slot, mp] = jnp.max(st, axis=0, keepdims=True)

    def softmax_pv(j, s_ref, slot, masked):
        vt_blk = jnp.concatenate([vt_ref[0, j], ones_rows], axis=0)
        if masked:
            hk, hq = tk // 2, tq // 2
            tri = (lax.broadcasted_iota(jnp.int32, (hk, hq), 0)
                   <= lax.broadcasted_iota(jnp.int32, (hk, hq), 1))
            for mp in range(2):
                st_a = jnp.where(tri, s_ref[mp, 0:hk, 0:hq], -jnp.inf)
                st_b = s_ref[mp, 0:hk, hq:tq]
                st_c = jnp.where(tri, s_ref[mp, hk:tk, hq:tq], -jnp.inf)
                m_blk = jnp.concatenate(
                    [jnp.max(st_a, axis=0, keepdims=True),
                     jnp.maximum(jnp.max(st_b, axis=0, keepdims=True),
                                 jnp.max(st_c, axis=0, keepdims=True))], axis=1)
                m_old = m_ref[mp]
                m_new = jnp.maximum(m_old, m_blk)
                alpha = jnp.exp2(m_old - m_new)
                p_a = jnp.exp2(st_a - m_new[:, 0:hq]).astype(BF16)
                p_bc = jnp.concatenate([jnp.exp2(st_b - m_new[:, hq:tq]),
                                        jnp.exp2(st_c - m_new[:, hq:tq])], axis=0).astype(BF16)
                acc_ref[mp, :, 0:hq] = (alpha[:, 0:hq] * acc_ref[mp, :, 0:hq]
                                        + jnp.dot(vt_blk[:, 0:hk], p_a,
                                                  preferred_element_type=F32))
                acc_ref[mp, :, hq:tq] = (alpha[:, hq:tq] * acc_ref[mp, :, hq:tq]
                                         + jnp.dot(vt_blk, p_bc, preferred_element_type=F32))
                m_ref[mp] = m_new
            return
        for mp in range(2):
            st = s_ref[mp]
            m_blk = mblk_ref[slot, mp]
            m_old = m_ref[mp]
            m_new = jnp.maximum(m_old, m_blk)
            alpha = jnp.exp2(m_old - m_new)
            pexp = jnp.exp2(st - m_new).astype(BF16)
            acc_ref[mp] = alpha * acc_ref[mp] + jnp.dot(vt_blk, pexp,
                                                       preferred_element_type=F32)
            m_ref[mp] = m_new

    def run(bufs):
        @pl.when(i == 0)
        def _():
            scores(0, *bufs[0])

        def advance(first, count):
            for u in range(count):
                scores(first + u + 1, *bufs[(u + 1) % 2])
                softmax_pv(first + u, *bufs[u % 2], False)

        def group(jj, carry):
            advance(ATT_UNROLL * jj, ATT_UNROLL)
            return carry

        lax.fori_loop(0, i // ATT_UNROLL, group, 0)

        for rest in range(ATT_UNROLL):
            @pl.when(i % ATT_UNROLL == rest)
            def _(rest=rest):
                advance(i - rest, rest)
                scores(0, *bufs[(rest + 1) % 2], map_operands(qn_ref[...]))
                softmax_pv(i, *bufs[rest % 2], True)

    start = ((i + 1) // 2) % 2
    buf0, buf1 = (s0_ref, 0), (s1_ref, 1)

    @pl.when(start == 0)
    def _():
        run((buf0, buf1))

    @pl.when(start == 1)
    def _():
        run((buf1, buf0))

    lp = lam_ref[...]
    lam = (jnp.exp(jnp.sum(lp[0:1, :] * lp[1:2, :], axis=-1, keepdims=True))
           - jnp.exp(jnp.sum(lp[2:3, :] * lp[3:4, :], axis=-1, keepdims=True))
           + lambda_init)
    dv = DA_V_DIM
    o_t = (acc_ref[0, 0:dv, :] / acc_ref[0, dv:dv + 1, :]
           - lam * (acc_ref[1, 0:dv, :] / acc_ref[1, dv:dv + 1, :]))
    ms = jnp.mean(o_t * o_t, axis=0, keepdims=True)
    y_t = o_t * lax.rsqrt(ms + LN_EPS) * (ng_ref[...] * (1.0 - lambda_init))
    out_ref[...] = y_t.T.astype(BF16)
    _attn_init_block(qn_ref[...], qm_ref, acc_ref, m_ref)


def _diff_attention(dq, dk, dvt, lam_p, ng_col, lambda_init):
    bsz, s, _ = dq.shape
    tq = ATT_Q
    nk, tk = dvt.shape[1], dvt.shape[3]
    assert tq == tk and nk * tk == s
    nq = s // tq
    g = math.gcd(nq, ATT_STEP_BLOCKS)
    return pl.pallas_call(
        functools.partial(_attn_kernel, lambda_init=lambda_init),
        grid=(bsz, DA_HEADS, nq // g),
        in_specs=[
            pl.BlockSpec((1, g * tq, 2 * DA_QK_DIM), lambda b, h, i: (b, i, h)),
            pl.BlockSpec((1, tq, 2 * DA_QK_DIM),
                         lambda b, h, i: (b, jnp.minimum((i + 1) * g, nq - 1), h)),
            pl.BlockSpec((1, s, 2 * DA_QK_DIM), lambda b, h, i: (b, 0, h)),
            pl.BlockSpec((1, nk, DA_V_DIM, tk), lambda b, h, i: (b, 0, h, 0)),
            pl.BlockSpec(lam_p.shape, lambda b, h, i: (0, 0)),
            pl.BlockSpec(ng_col.shape, lambda b, h, i: (0, 0)),
        ],
        out_specs=pl.BlockSpec((1, g * tq, DA_V_DIM), lambda b, h, i: (b, i, h)),
        out_shape=jax.ShapeDtypeStruct((bsz, s, DA_HEADS * DA_V_DIM), BF16),
        scratch_shapes=[pltpu.VMEM((g + 1, tq, 2 * DA_QK_DIM), BF16),
                        pltpu.VMEM((2, tq, 2 * DA_QK_DIM), BF16),
                        pltpu.VMEM((2, DA_V_DIM + ATT_SUM_ROWS, tq), F32),
                        pltpu.VMEM((2, tk, tq), F32),
                        pltpu.VMEM((2, tk, tq), F32),
                        pltpu.VMEM((2, 2, 1, tq), F32),
                        pltpu.VMEM((2, 1, tq), F32)],
        compiler_params=_params("parallel", "parallel", "arbitrary"),
        name="diff_attention",
    )(dq, dq, dk, dvt, lam_p, ng_col)


def _out_mlp_kernel(yc_ref, ym_ref, yd_ref, x_ref, wo_ref, g1_ref, b1_ref, wu_ref, wd_ref,
                    g2_ref, b2_ref, out_ref, x1_ref, xb_ref, acc_ref):
    f = pl.program_id(1)
    last = pl.num_programs(1) - 1
    wc = yc_ref.shape[1]
    wm = ym_ref.shape[1]

    def mix(r):
        rows = pl.ds(r, MLP_SUB)
        h = jnp.dot(yc_ref[rows, :], wo_ref[0:wc, :], preferred_element_type=F32)
        h = h + jnp.dot(ym_ref[rows, :], wo_ref[wc:wc + wm, :], preferred_element_type=F32)
        return h + jnp.dot(yd_ref[rows, :], wo_ref[wc + wm:, :], preferred_element_type=F32)

    def step(first, final):
        n_rows = x_ref.shape[0]
        if first:
            h_next = mix(0)
        for r in range(0, n_rows, MLP_SUB):
            rows = pl.ds(r, MLP_SUB)
            if first:
                h = h_next
                if r + MLP_SUB < n_rows:
                    h_next = mix(r + MLP_SUB)
                x1 = _layer_norm_rows(DEEPNORM_ALPHA * x_ref[rows, :] + h,
                                      g1_ref[...], b1_ref[...])
                x1_ref[rows, :] = x1
                xb = x1.astype(BF16)
                xb_ref[rows, :] = xb
            else:
                xb = xb_ref[rows, :]
            up = jnp.maximum(jnp.dot(xb, wu_ref[...], preferred_element_type=F32), 0.0)
            down = jnp.dot((up * up).astype(BF16), wd_ref[...], preferred_element_type=F32)
            acc = down if first else acc_ref[rows, :] + down
            if final:
                out_ref[rows, :] = _layer_norm_rows(DEEPNORM_ALPHA * x1_ref[rows, :] + acc,
                                                    g2_ref[...], b2_ref[...])
            else:
                acc_ref[rows, :] = acc

    @pl.when(f == 0)
    def _():
        step(True, False)

    @pl.when(jnp.logical_and(f > 0, f < last))
    def _():
        step(False, False)

    @pl.when(f == last)
    def _():
        step(False, True)


def _out_mlp(yc, ym, yd, x2, wo, g1, b1, wu, wd, g2, b2):
    n, d = x2.shape
    dff = wu.shape[1]
    tm, tf = MLP_ROWS, MLP_FF
    assert dff // tf >= 2, "the first and last hidden steps are distinct code paths"
    rows = lambda i, f: (i, 0)
    const = lambda i, f: (0, 0)
    return pl.pallas_call(
        _out_mlp_kernel,
        grid=(n // tm, dff // tf),
        in_specs=[pl.BlockSpec((tm, yc.shape[1]), rows),
                  pl.BlockSpec((tm, ym.shape[1]), rows),
                  pl.BlockSpec((tm, yd.shape[1]), rows),
                  pl.BlockSpec((tm, d), rows),
                  pl.BlockSpec(wo.shape, const),
                  pl.BlockSpec(g1.shape, const),
                  pl.BlockSpec(b1.shape, const),
                  pl.BlockSpec((d, tf), lambda i, f: (0, f)),
                  pl.BlockSpec((tf, d), lambda i, f: (f, 0)),
                  pl.BlockSpec(g2.shape, const),
                  pl.BlockSpec(b2.shape, const)],
        out_specs=pl.BlockSpec((tm, d), rows),
        out_shape=jax.ShapeDtypeStruct((n, d), F32),
        scratch_shapes=[pltpu.VMEM((tm, d), F32), pltpu.VMEM((tm, d), BF16),
                        pltpu.VMEM((tm, d), F32)],
        compiler_params=_params("parallel", "arbitrary"),
        name="out_mlp_ln",
    )(yc, ym, yd, x2, wo, g1, b1, wu, wd, g2, b2)


def _row(v):
    return v.reshape(1, -1).astype(F32)


def _pad_rows(w, rows):
    return jnp.pad(w.astype(F32), ((0, rows - w.shape[0]), (0, 0)))


def _rearranged_w_in(w):
    d = w.shape[0]
    mk_lo = 256 + 256 + ML_HEADS * ML_HEAD_DIM
    mk_hi = mk_lo + ML_HEADS * ML_HEAD_DIM
    gate_lo = 256 + 256 + 512 + 256 + 256
    gate_hi = gate_lo + 2 * ML_HEADS
    v_lo = gate_hi + 2 * DA_HEADS * 2 * DA_QK_DIM
    pad = jnp.zeros((d, LANES - 2 * ML_HEADS), w.dtype)
    w_r = jnp.concatenate([w[:, :gate_lo], w[:, gate_hi:v_lo], w[:, gate_lo:gate_hi], pad], axis=1)
    return w_r.astype(BF16), w[:, v_lo:].T.astype(BF16), w[:, mk_lo:mk_hi].T.astype(BF16)


def kernel(x, w_in, b_igate, b_fgate, conv_dw_w, conv_dw_b, conv_ln_g, conv_ln_b, conv_pw_w, conv_pw_b, ml_conv_w, ml_conv_b, ml_norm_g, lam_q1, lam_k1, lam_q2, lam_k2, da_norm_g, w_out, ln1_g, ln1_b, w_up, w_down, ln2_g, ln2_b):
    bsz, s, d = x.shape
    n = bsz * s
    x2 = x.reshape(n, d)
    for l in range(DEPTH):
        lambda_init = 0.8 - 0.6 * math.exp(-0.3 * l)
        w_r, w_vt, w_kt = _rearranged_w_in(w_in[l])
        conv_params = (_pad_rows(conv_dw_w[l], 32), _row(conv_dw_b[l]), _row(conv_ln_g[l]),
                       _row(conv_ln_b[l]), conv_pw_w[l].astype(BF16), _row(conv_pw_b[l]))
        y_conv, mq, mv, mo, gates, dq, dk, dvt, mkt = _in_proj(x2, w_r, w_vt, w_kt,
                                                               conv_params, bsz)
        seq = lambda a: a.reshape(bsz, s, a.shape[1])

        gate_bias = jnp.pad(jnp.concatenate([b_igate[l], b_fgate[l]]).astype(F32),
                            (0, LANES - 2 * ML_HEADS)).reshape(1, LANES)
        wq = ML_HEADS * ML_HEAD_DIM
        cw_k = ml_conv_w[l][:, wq:].astype(F32)
        cb_k = ml_conv_b[l][wq:].astype(F32)
        y_ml = _mlstm(seq(mq), mkt, seq(mv), seq(mo), seq(gates),
                      _pad_rows(ml_conv_w[l][:, :wq], SUBLANES), _row(ml_conv_b[l][:wq]),
                      jnp.broadcast_to(cw_k[:, :, None], cw_k.shape + (ML_CHUNK,)),
                      jnp.broadcast_to(cb_k[:, None], cb_k.shape + (ML_CHUNK,)),
                      gate_bias, _row(ml_norm_g[l]))

        lam_p = jnp.stack([lam_q1[l], lam_k1[l], lam_q2[l], lam_k2[l]]).astype(F32)
        y_da = _diff_attention(seq(dq), seq(dk), dvt, lam_p,
                               da_norm_g[l].astype(F32).reshape(-1, 1), lambda_init)

        x2 = _out_mlp(y_conv.reshape(n, -1), y_ml.reshape(n, -1), y_da.reshape(n, -1), x2,
                      w_out[l].astype(BF16), _row(ln1_g[l]), _row(ln1_b[l]),
                      w_up[l].astype(BF16), w_down[l].astype(BF16),
                      _row(ln2_g[l]), _row(ln2_b[l]))
    return x2.reshape(bsz, s, d)
```

```python
import functools
import math

import jax
import jax.numpy as jnp
from jax import lax
from jax.experimental import pallas as pl
from jax.experimental.pallas import tpu as pltpu

F32 = jnp.float32
BF16 = jnp.bfloat16

LANES = 128
SUBLANES = 8
VMEM_LIMIT_BYTES = 56 * 1024 * 1024

DEPTH = 2
CONV_KSIZE = 31
ML_HEADS = 4
ML_HEAD_DIM = 64
ML_QK_CONV = 4
ML_CHUNK = 128
DA_HEADS = 4
DA_QK_DIM = 64
DA_V_DIM = 128
LN_EPS = 1e-5
DEEPNORM_ALPHA = (2 * DEPTH) ** 0.25

PROJ_ROWS = 1024
CONV_HALO = 32
CONV_SUB = 64
ATT_Q = 512
ATT_K = 512
ML_BATCH = 4
ATT_STEP_BLOCKS = 4
ATT_UNROLL = 4
ATT_SUM_ROWS = 16
MLP_ROWS = 1024
MLP_FF = 1024
MLP_SUB = 256


def _params(*semantics):
    return pltpu.CompilerParams(dimension_semantics=semantics,
                                vmem_limit_bytes=VMEM_LIMIT_BYTES)


def _sigmoid(x):
    return 1.0 / (1.0 + jnp.exp(-x))


def _layer_norm_rows(x, g, b):
    mu = jnp.mean(x, axis=-1, keepdims=True)
    xc = x - mu
    var = jnp.mean(xc * xc, axis=-1, keepdims=True)
    return xc * lax.rsqrt(var + LN_EPS) * g + b


def _in_proj_kernel(x_ref, w_ref, wvt_ref, wkt_ref, dw_ref, dwb_ref, cg_ref, cb_ref, pw_ref,
                    pwb_ref, yconv_ref, mq_ref, mv_ref, mo_ref, gate_ref, dq_ref, dk_ref,
                    dvt_ref, mkt_ref, xs, ybuf, *, tiles_per_seq):
    rows = x_ref.shape[0]
    xb = x_ref[...].astype(BF16)

    def sec(lo, width):
        return jnp.dot(xb, w_ref[:, lo:lo + width], preferred_element_type=F32)

    @pl.when(pl.program_id(0) % tiles_per_seq == 0)
    def _():
        xs[0, 0:CONV_HALO, :] = jnp.zeros((CONV_HALO, xs.shape[2]), F32)

    xs[0, CONV_HALO:CONV_HALO + rows, :] = sec(0, 256) * _sigmoid(sec(256, 256))
    span = CONV_HALO + rows - SUBLANES
    for k in range(1, SUBLANES):
        xs[k, 0:span, :] = xs[0, k:k + span, :]

    def conv_block(r, taps):
        first = CONV_HALO - (CONV_KSIZE - 1)
        acc = jnp.broadcast_to(dwb_ref[...], (CONV_SUB, xs.shape[2]))
        for j in range(CONV_KSIZE):
            k = (first + j) % SUBLANES
            lo = r + first + j - k
            acc = acc + taps[j:j + 1, :] * xs[k, lo:lo + CONV_SUB, :]
        y = _layer_norm_rows(acc, cg_ref[...], cb_ref[...])
        ybuf[r:r + CONV_SUB, :] = (y * _sigmoid(y)).astype(BF16)

    def proj_mqk():
        res = sec(512, 256)
        mq_ref[...] = res
        kt = lax.dot_general(wkt_ref[...], xb, (((1,), (1,)), ((), ())),
                             preferred_element_type=F32)
        for ck in range(mkt_ref.shape[1]):
            mkt_ref[0, ck] = kt[:, ck * ML_CHUNK:(ck + 1) * ML_CHUNK]
        return res

    def proj_mv_mo():
        mv_ref[...] = sec(1024, 256)
        res = sec(1280, 256)
        mo_ref[...] = res
        return res

    def proj_dq():
        res = sec(1536, 512)
        dq_ref[...] = (res * (DA_QK_DIM ** -0.5 * math.log2(math.e))).astype(BF16)
        return res

    def proj_dk_gate_vt():
        res = sec(2048, 512)
        dk_ref[...] = res.astype(BF16)
        gate_ref[...] = sec(2560, LANES)
        vt = lax.dot_general(wvt_ref[...], xb, (((1,), (1,)), ((), ())),
                             preferred_element_type=F32).astype(BF16)
        for kb in range(dvt_ref.shape[1]):
            dvt_ref[0, kb] = vt[:, kb * ATT_K:(kb + 1) * ATT_K]
        return res

    blocks_per_stage = rows // CONV_SUB // 4
    stages = [(proj, blocks_per_stage)
              for proj in (proj_mqk, proj_mv_mo, proj_dq, proj_dk_gate_vt)]

    def exact_zero(x):
        bits = lax.shift_right_logical(pltpu.bitcast(x, jnp.uint32), jnp.uint32(32))
        return pltpu.bitcast(bits, F32)

    pw_rows = blocks_per_stage * CONV_SUB
    taps = dw_ref[...]
    done = 0
    for idx, (piece, n_blocks) in enumerate(stages):
        res = piece()
        for _ in range(n_blocks):
            conv_block(done, taps)
            done += CONV_SUB
            if done % pw_rows == 0:
                r0 = done - pw_rows
                yconv_ref[r0:done, :] = (jnp.dot(ybuf[r0:done, :], pw_ref[...],
                                                 preferred_element_type=F32)
                                         + pwb_ref[...]).astype(BF16)
        if idx + 1 < len(stages) and stages[idx + 1][1]:
            taps = dw_ref[...] + exact_zero(res[res.shape[0] - 1:, 0:xs.shape[2]])
    xs[0, 0:CONV_HALO, :] = xs[0, rows:rows + CONV_HALO, :]


def _in_proj(x2, w_r, w_vt, w_kt, conv_params, bsz):
    n, d = x2.shape
    tm = PROJ_ROWS
    kb_per_tile = tm // ATT_K
    ck_per_tile = tm // ML_CHUNK
    tiles_per_seq = n // bsz // tm
    widths = (256, 256, 256, 256, LANES, 512, 512)
    dtypes = (BF16, F32, F32, F32, F32, BF16, BF16)
    wv = w_vt.shape[0]
    wk = w_kt.shape[0]
    const = lambda i: (0, 0)
    slabs = lambda i: (i // tiles_per_seq, i % tiles_per_seq, 0, 0)
    return pl.pallas_call(
        functools.partial(_in_proj_kernel, tiles_per_seq=tiles_per_seq),
        grid=(n // tm,),
        in_specs=[pl.BlockSpec((tm, d), lambda i: (i, 0)),
                  pl.BlockSpec(w_r.shape, const),
                  pl.BlockSpec(w_vt.shape, const),
                  pl.BlockSpec(w_kt.shape, const)]
        + [pl.BlockSpec(p.shape, const) for p in conv_params],
        out_specs=[pl.BlockSpec((tm, w), lambda i: (i, 0)) for w in widths]
        + [pl.BlockSpec((1, kb_per_tile, wv, ATT_K), slabs),
           pl.BlockSpec((1, ck_per_tile, wk, ML_CHUNK), slabs)],
        out_shape=[jax.ShapeDtypeStruct((n, w), dt) for w, dt in zip(widths, dtypes)]
        + [jax.ShapeDtypeStruct((bsz, tiles_per_seq * kb_per_tile, wv, ATT_K), BF16),
           jax.ShapeDtypeStruct((bsz, tiles_per_seq * ck_per_tile, wk, ML_CHUNK), F32)],
        scratch_shapes=[pltpu.VMEM((SUBLANES, CONV_HALO + tm, widths[0]), F32),
                        pltpu.VMEM((tm, widths[0]), BF16)],
        compiler_params=_params("arbitrary"),
        name="in_proj_conv",
    )(x2, w_r, w_vt, w_kt, *conv_params)


def _mlstm_kernel(qk_ref, halo_ref, kt_ref, ktp_ref, v_ref, o_ref, gate_ref, cw_ref, cb_ref,
                  cwk_ref, cbk_ref, gb_ref, ng_ref, sel_ref, avg_ref, out_ref, qpad, ct_ref,
                  m_ref):
    c = pl.program_id(1)
    nb = qk_ref.shape[0]
    t = ML_CHUNK
    w_pair = 2 * ML_HEAD_DIM
    n_pairs = ML_HEADS // 2
    tn = (((0,), (0,)), ((), ()))

    @pl.when(c == 0)
    def _():
        ct_ref[...] = jnp.zeros_like(ct_ref)
        m_ref[...] = jnp.zeros_like(m_ref)

    row = lax.broadcasted_iota(jnp.int32, (t, t), 0)
    col = lax.broadcasted_iota(jnp.int32, (t, t), 1)
    causal = row >= col
    tri = causal.astype(F32)
    lane = lax.broadcasted_iota(jnp.int32, (t, w_pair), 1)
    low = lane < ML_HEAD_DIM
    owns = (low, lane >= ML_HEAD_DIM)
    own_rows = (row < ML_HEAD_DIM, row >= ML_HEAD_DIM)

    first = SUBLANES - (ML_QK_CONV - 1)
    lane_k = lax.broadcasted_iota(jnp.int32, (kt_ref.shape[2], t), 1)
    qk, kt, gpre_t, log_f_t = [], [], [], []
    for bb in range(nb):
        qpad[bb, 0:SUBLANES, :] = jnp.where(c > 0, halo_ref[bb], 0.0)
        qpad[bb, SUBLANES:SUBLANES + t, :] = qk_ref[bb]
        acc = jnp.broadcast_to(cb_ref[...], (t, qk_ref.shape[2]))
        for j in range(ML_QK_CONV):
            acc = acc + cw_ref[j:j + 1, :] * qpad[bb, first + j:first + j + t, :]
        qk.append(acc * _sigmoid(acc))
        cur = kt_ref[bb, 0]
        prev = jnp.where(c > 0, ktp_ref[bb, 0], 0.0)
        acc = cbk_ref[...] + cwk_ref[ML_QK_CONV - 1] * cur
        for s in range(1, ML_QK_CONV):
            shifted = jnp.where(lane_k >= s, pltpu.roll(cur, s, 1), pltpu.roll(prev, s, 1))
            acc = acc + cwk_ref[ML_QK_CONV - 1 - s] * shifted
        kt.append(acc * _sigmoid(acc) * (ML_HEAD_DIM ** -0.5))
        g_t = (gate_ref[bb] + gb_ref[...]).T[0:SUBLANES, :]
        gpre_t.append(g_t)
        log_f_t.append(jnp.minimum(g_t, 0.0) - jnp.log(1.0 + jnp.exp(-jnp.abs(g_t))))

    tri_t = (row <= col).astype(F32)
    cums = [jnp.dot(lf, tri_t, preferred_element_type=F32, precision=lax.Precision.HIGHEST)
            for lf in log_f_t]

    items = [(bb, h) for bb in range(nb) for h in range(ML_HEADS)]
    st = {}
    for bb in range(nb):
        for pair in range(n_pairs):
            lo = pair * w_pair
            st[bb, pair, "q"] = qk[bb][:, lo:lo + w_pair]
            st[bb, pair, "kt"] = kt[bb][lo:lo + w_pair, :]
            st[bb, pair, "kbt"] = st[bb, pair, "kt"].astype(BF16)
            st[bb, pair, "v"] = v_ref[bb, :, lo:lo + w_pair]
    for (bb, h) in items:
        pair, par = divmod(h, 2)
        st[bb, h, "q_m"] = jnp.where(owns[par], st[bb, pair, "q"], 0.0).astype(BF16)
        st[bb, h, "vx"] = jnp.where(owns[par], st[bb, pair, "v"], 1.0).astype(BF16)
    for (bb, h) in items:
        st[bb, h, "s_qk"] = jnp.dot(st[bb, h, "q_m"], st[bb, h // 2, "kbt"],
                                    preferred_element_type=F32)
    for (bb, h) in items:
        st[bb, h, "ct_prev"] = ct_ref[bb, h]
        st[bb, h, "inter"] = jnp.dot(st[bb, h, "q_m"], st[bb, h, "ct_prev"].astype(BF16),
                                     preferred_element_type=F32)

    head_row = lax.broadcasted_iota(jnp.int32, (SUBLANES, t), 0) < ML_HEADS
    lane_t = lax.broadcasted_iota(jnp.int32, (SUBLANES, t), 1)
    def last_lane(x):
        return jnp.broadcast_to(x[:, t - 1:t], x.shape)

    r_t, p_end_t, a_st_t, e_st_t, g_hl = [], [], [], [], []
    for bb in range(nb):
        b_t = pltpu.roll(cums[bb], ML_HEADS, 0)
        b_t = jnp.where(head_row, b_t, 0.0)
        r = jnp.where(head_row, gpre_t[bb] - b_t, 0.0)
        cm = r
        shift = 1
        while shift < t:
            cm = jnp.where(lane_t >= shift, jnp.maximum(cm, pltpu.roll(cm, shift, 1)), cm)
            shift *= 2
        m_prev = m_ref[bb]
        u = jnp.maximum(m_prev, cm)
        a_inter = jnp.exp(m_prev - u)
        floor = jnp.exp(-(b_t + u))
        g_end = last_lane(b_t)
        cm_end = last_lane(cm)
        p_end = jnp.exp(r - cm_end)
        m_loc = g_end + cm_end
        m_new = jnp.maximum(g_end + m_prev, m_loc)
        a_st_t.append(jnp.exp(g_end + m_prev - m_new))
        e_st_t.append(jnp.exp(m_loc - m_new))
        m_ref[bb] = jnp.where(head_row, m_new, 0.0)
        r_t.append(r)
        p_end_t.append(p_end)
        parts = []
        for tile in (u, a_inter, floor):
            hi = tile.astype(BF16)
            parts += [hi, (tile - hi.astype(F32)).astype(BF16)]
        g_hl.append(jnp.concatenate(parts, axis=0))
    rep = {(bb, h): lax.dot_general(g_hl[bb], sel_ref[h], tn, preferred_element_type=F32)
           for bb in range(nb) for h in range(ML_HEADS)}

    def tile_rows(x_row):
        return jnp.tile(jnp.broadcast_to(x_row, (SUBLANES, LANES)), (t // SUBLANES, 1))

    for (bb, h) in items:
        pair, par = divmod(h, 2)
        u_col = rep[bb, h][:, 0:LANES]
        st[bb, h, "a_inter"] = rep[bb, h][:, LANES:2 * LANES]
        st[bb, h, "floor"] = rep[bb, h][:, 2 * LANES:3 * LANES]
        r_row = tile_rows(r_t[bb][h:h + 1, :])
        st[bb, h, "dexp"] = jnp.exp(jnp.where(causal, r_row - u_col, -jnp.inf))
        kp_t = (jnp.where(own_rows[par], st[bb, pair, "kt"], 0.0)
                * tile_rows(p_end_t[bb][h:h + 1, :])).astype(BF16)
        c_loc = jnp.dot(kp_t, st[bb, h, "vx"], preferred_element_type=F32)
        ct_ref[bb, h] = (tile_rows(a_st_t[bb][h:h + 1, :]) * st[bb, h, "ct_prev"]
                         + tile_rows(e_st_t[bb][h:h + 1, :]) * c_loc)

    for (bb, h) in items:
        sc = (st[bb, h, "s_qk"] * st[bb, h, "dexp"]).astype(BF16)
        st[bb, h, "intra"] = jnp.dot(sc, st[bb, h, "vx"], preferred_element_type=F32)
    def half_mean(x):
        hi = x.astype(BF16)
        lo = (x - hi.astype(F32)).astype(BF16)
        return jnp.dot(jnp.concatenate([hi, lo], axis=1), avg_ref[...],
                       preferred_element_type=F32)

    for bb in range(nb):
        for pair in range(n_pairs):
            lo = pair * w_pair
            h0, h1 = 2 * pair, 2 * pair + 1
            nd0 = st[bb, h0, "a_inter"] * st[bb, h0, "inter"] + st[bb, h0, "intra"]
            nd1 = st[bb, h1, "a_inter"] * st[bb, h1, "inter"] + st[bb, h1, "intra"]
            den = pltpu.roll(jnp.where(low, nd1, nd0), ML_HEAD_DIM, 1)
            floor = jnp.where(low, st[bb, h0, "floor"], st[bb, h1, "floor"])
            h_pair = jnp.where(low, nd0, nd1) / jnp.maximum(jnp.abs(den), floor)
            x = h_pair * _sigmoid(o_ref[bb, :, lo:lo + w_pair])
            xc = x - half_mean(x)
            var = half_mean(xc * xc)
            y = xc * lax.rsqrt(var + LN_EPS) * ng_ref[:, lo:lo + w_pair]
            out_ref[bb, :, lo:lo + w_pair] = y.astype(BF16)


def _mlstm(mq, mkt, mv, mo, gates, cw, cb, cwk, cbk, gb, ng):
    bsz, s, wqk = mq.shape
    wv = mv.shape[2]
    wk = mkt.shape[2]
    t = ML_CHUNK
    nb = math.gcd(bsz, ML_BATCH)
    halo_per_chunk = t // SUBLANES
    n_rep = 3
    src = jnp.arange(n_rep * 2 * SUBLANES)[:, None]
    dst = jnp.arange(n_rep * LANES)[None, :] // LANES
    sel = jnp.stack([((src // (2 * SUBLANES) == dst) & (src % SUBLANES == h)).astype(BF16)
                     for h in range(ML_HEADS)])
    grp = lambda idx: (idx % LANES) // ML_HEAD_DIM
    avg = jnp.where(grp(jnp.arange(2 * LANES))[:, None] == grp(jnp.arange(LANES))[None, :],
                    1.0 / ML_HEAD_DIM, 0.0).astype(BF16)
    const = lambda bi, ci: (0, 0)
    return pl.pallas_call(
        _mlstm_kernel,
        grid=(bsz // nb, s // t),
        in_specs=[
            pl.BlockSpec((nb, t, wqk), lambda bi, ci: (bi, ci, 0)),
            pl.BlockSpec((nb, SUBLANES, wqk),
                         lambda bi, ci: (bi, jnp.maximum(ci * halo_per_chunk - 1, 0), 0)),
            pl.BlockSpec((nb, 1, wk, t), lambda bi, ci: (bi, ci, 0, 0)),
            pl.BlockSpec((nb, 1, wk, t), lambda bi, ci: (bi, jnp.maximum(ci - 1, 0), 0, 0)),
            pl.BlockSpec((nb, t, wv), lambda bi, ci: (bi, ci, 0)),
            pl.BlockSpec((nb, t, wv), lambda bi, ci: (bi, ci, 0)),
            pl.BlockSpec((nb, t, LANES), lambda bi, ci: (bi, ci, 0)),
            pl.BlockSpec(cw.shape, const),
            pl.BlockSpec(cb.shape, const),
            pl.BlockSpec(cwk.shape, lambda bi, ci: (0, 0, 0)),
            pl.BlockSpec(cbk.shape, const),
            pl.BlockSpec(gb.shape, const),
            pl.BlockSpec(ng.shape, const),
            pl.BlockSpec(sel.shape, lambda bi, ci: (0, 0, 0)),
            pl.BlockSpec(avg.shape, const),
        ],
        out_specs=pl.BlockSpec((nb, t, wv), lambda bi, ci: (bi, ci, 0)),
        out_shape=jax.ShapeDtypeStruct((bsz, s, wv), BF16),
        scratch_shapes=[pltpu.VMEM((nb, SUBLANES + t, wqk), F32),
                        pltpu.VMEM((nb, ML_HEADS, 2 * ML_HEAD_DIM, LANES), F32),
                        pltpu.VMEM((nb, SUBLANES, LANES), F32)],
        compiler_params=_params("parallel", "arbitrary"),
        name="mlstm",
    )(mq, mq, mkt, mkt, mv, mo, gates, cw, cb, cwk, cbk, gb, ng, sel, avg)


def _attn_kernel(q_ref, qn_ref, k_ref, vt_ref, lam_ref, ng_ref, out_ref, qall_ref, qm_ref,
                 acc_ref, s0_ref, s1_ref, mblk_ref, m_ref, *, lambda_init):
    tq = qn_ref.shape[1]
    tk = vt_ref.shape[3]
    n_sub = q_ref.shape[1] // tq
    for t in range(n_sub):
        qall_ref[t] = q_ref[0, t * tq:(t + 1) * tq, :]
    qall_ref[n_sub] = qn_ref[0]
    _attn_init_block(qall_ref[0], qm_ref, acc_ref, m_ref)

    def query_block(t, carry):
        _attn_query_block(pl.program_id(2) * n_sub + t, qall_ref.at[t], qall_ref.at[t + 1],
                          k_ref, vt_ref, lam_ref, ng_ref,
                          out_ref.at[0, pl.ds(pl.multiple_of(t * tq, tq), tq), :],
                          qm_ref, acc_ref, s0_ref, s1_ref, mblk_ref, m_ref, lambda_init)
        return carry

    lax.fori_loop(0, n_sub, query_block, 0)


def _attn_map_operands(q):
    lane = lax.broadcasted_iota(jnp.int32, q.shape, 1)
    zero = jnp.zeros_like(q)
    return jnp.where(lane < DA_QK_DIM, q, zero), jnp.where(lane >= DA_QK_DIM, q, zero)


def _attn_init_block(q, qm_ref, acc_ref, m_ref):
    qm_ref[0], qm_ref[1] = _attn_map_operands(q)
    acc_ref[...] = jnp.zeros_like(acc_ref)
    m_ref[...] = jnp.full_like(m_ref, -jnp.inf)


def _attn_query_block(i, q_ref, qn_ref, k_ref, vt_ref, lam_ref, ng_ref, out_ref, qm_ref,
                      acc_ref, s0_ref, s1_ref, mblk_ref, m_ref, lambda_init):
    tq = q_ref.shape[0]
    tk = vt_ref.shape[3]
    map_operands = _attn_map_operands

    ones_rows = jnp.ones((ATT_SUM_ROWS, tk), BF16)

    def scores(j, s_ref, slot, qms=None, diagonal=False):
        nt = (((1,), (1,)), ((), ()))
        if diagonal:
            hk, hq = tk // 2, tq // 2
            k_lo = k_ref[0, pl.ds(pl.multiple_of(j * tk, tk), hk), :]
            k_hi = k_ref[0, pl.ds(pl.multiple_of(j * tk + hk, hk), hk), :]
            for mp in range(2):
                s_ref[mp, 0:hk, :] = lax.dot_general(k_lo, qm_ref[mp], nt,
                                                     preferred_element_type=F32)
                s_ref[mp, hk:tk, hq:tq] = lax.dot_general(k_hi, qm_ref[mp, hq:tq, :], nt,
                                                          preferred_element_type=F32)
            return
        k_blk = k_ref[0, pl.ds(pl.multiple_of(j * tk, tk), tk), :]
        for mp in range(2):
            qm = qm_ref[mp] if qms is None else qms[mp]
            st = lax.dot_general(k_blk, qm, (((1,), (1,)), ((), ())),
                                 preferred_element_type=F32)
            s_ref[mp] = st
            mblk_ref[slot, mp] = jnp.max(st, axis=0, keepdims=True)

    def softmax_pv(j, s_ref, slot, masked):
        vt_blk = jnp.concatenate([vt_ref[0, j], ones_rows], axis=0)
        if masked:
            hk, hq = tk // 2, tq // 2
            tri = (lax.broadcasted_iota(jnp.int32, (hk, hq), 0)
                   <= lax.broadcasted_iota(jnp.int32, (hk, hq), 1))
            for mp in range(2):
                st_a = jnp.where(tri, s_ref[mp, 0:hk, 0:hq], -jnp.inf)
                st_b = s_ref[mp, 0:hk, hq:tq]
                st_c = jnp.where(tri, s_ref[mp, hk:tk, hq:tq], -jnp.inf)
                m_blk = jnp.concatenate(
                    [jnp.max(st_a, axis=0, keepdims=True),
                     jnp.maximum(jnp.max(st_b, axis=0, keepdims=True),
                                 jnp.max(st_c, axis=0, keepdims=True))], axis=1)
                m_old = m_ref[mp]
                m_new = jnp.maximum(m_old, m_blk)
                alpha = jnp.exp2(m_old - m_new)
                p_a = jnp.exp2(st_a - m_new[:, 0:hq]).astype(BF16)
                p_bc = jnp.concatenate([jnp.exp2(st_b - m_new[:, hq:tq]),
                                        jnp.exp2(st_c - m_new[:, hq:tq])], axis=0).astype(BF16)
                acc_ref[mp, :, 0:hq] = (alpha[:, 0:hq] * acc_ref[mp, :, 0:hq]
                                        + jnp.dot(vt_blk[:, 0:hk], p_a,
                                                  preferred_element_type=F32))
                acc_ref[mp, :, hq:tq] = (alpha[:, hq:tq] * acc_ref[mp, :, hq:tq]
                                         + jnp.dot(vt_blk, p_bc, preferred_element_type=F32))
                m_ref[mp] = m_new
            return
        for mp in range(2):
            st = s_ref[mp]
            m_blk = mblk_ref[slot, mp]
            m_old = m_ref[mp]
            m_new = jnp.maximum(m_old, m_blk)
            alpha = jnp.exp2(m_old - m_new)
            pexp = jnp.exp2(st - m_new).astype(BF16)
            acc_ref[mp] = alpha * acc_ref[mp] + jnp.dot(vt_blk, pexp,
                                                       preferred_element_type=F32)
            m_ref[mp] = m_new

    def run(bufs):
        @pl.when(i == 0)
        def _():
            scores(0, *bufs[0])

        def advance(first, count, ends_on_diagonal=False):
            for u in range(count):
                scores(first + u + 1, *bufs[(u + 1) % 2],
                       diagonal=ends_on_diagonal and u == count - 1)
                softmax_pv(first + u, *bufs[u % 2], False)

        def group(jj, carry):
            advance(ATT_UNROLL * jj, ATT_UNROLL)
            return carry

        lax.fori_loop(0, i // ATT_UNROLL, group, 0)

        for rest in range(ATT_UNROLL):
            @pl.when(i % ATT_UNROLL == rest)
            def _(rest=rest):
                advance(i - rest, rest, ends_on_diagonal=True)
                scores(0, *bufs[(rest + 1) % 2], map_operands(qn_ref[...]))
                softmax_pv(i, *bufs[rest % 2], True)

    start = ((i + 1) // 2) % 2
    buf0, buf1 = (s0_ref, 0), (s1_ref, 1)

    @pl.when(start == 0)
    def _():
        run((buf0, buf1))

    @pl.when(start == 1)
    def _():
        run((buf1, buf0))

    lp = lam_ref[...]
    lam = (jnp.exp(jnp.sum(lp[0:1, :] * lp[1:2, :], axis=-1, keepdims=True))
           - jnp.exp(jnp.sum(lp[2:3, :] * lp[3:4, :], axis=-1, keepdims=True))
           + lambda_init)
    dv = DA_V_DIM
    o_t = (acc_ref[0, 0:dv, :] / acc_ref[0, dv:dv + 1, :]
           - lam * (acc_ref[1, 0:dv, :] / acc_ref[1, dv:dv + 1, :]))
    ms = jnp.mean(o_t * o_t, axis=0, keepdims=True)
    y_t = o_t * lax.rsqrt(ms + LN_EPS) * (ng_ref[...] * (1.0 - lambda_init))
    out_ref[...] = y_t.T.astype(BF16)
    _attn_init_block(qn_ref[...], qm_ref, acc_ref, m_ref)


def _diff_attention(dq, dk, dvt, lam_p, ng_col, lambda_init):
    bsz, s, _ = dq.shape
    tq = ATT_Q
    nk, tk = dvt.shape[1], dvt.shape[3]
    assert tq == tk and nk * tk == s
    nq = s // tq
    g = math.gcd(nq, ATT_STEP_BLOCKS)
    return pl.pallas_call(
        functools.partial(_attn_kernel, lambda_init=lambda_init),
        grid=(bsz, DA_HEADS, nq // g),
        in_specs=[
            pl.BlockSpec((1, g * tq, 2 * DA_QK_DIM), lambda b, h, i: (b, i, h)),
            pl.BlockSpec((1, tq, 2 * DA_QK_DIM),
                         lambda b, h, i: (b, jnp.minimum((i + 1) * g, nq - 1), h)),
            pl.BlockSpec((1, s, 2 * DA_QK_DIM), lambda b, h, i: (b, 0, h)),
            pl.BlockSpec((1, nk, DA_V_DIM, tk), lambda b, h, i: (b, 0, h, 0)),
            pl.BlockSpec(lam_p.shape, lambda b, h, i: (0, 0)),
            pl.BlockSpec(ng_col.shape, lambda b, h, i: (0, 0)),
        ],
        out_specs=pl.BlockSpec((1, g * tq, DA_V_DIM), lambda b, h, i: (b, i, h)),
        out_shape=jax.ShapeDtypeStruct((bsz, s, DA_HEADS * DA_V_DIM), BF16),
        scratch_shapes=[pltpu.VMEM((g + 1, tq, 2 * DA_QK_DIM), BF16),
                        pltpu.VMEM((2, tq, 2 * DA_QK_DIM), BF16),
                        pltpu.VMEM((2, DA_V_DIM + ATT_SUM_ROWS, tq), F32),
                        pltpu.VMEM((2, tk, tq), F32),
                        pltpu.VMEM((2, tk, tq), F32),
                        pltpu.VMEM((2, 2, 1, tq), F32),
                        pltpu.VMEM((2, 1, tq), F32)],
        compiler_params=_params("parallel", "parallel", "arbitrary"),
        name="diff_attention",
    )(dq, dq, dk, dvt, lam_p, ng_col)


def _out_mlp_kernel(yc_ref, ym_ref, yd_ref, x_ref, wo_ref, g1_ref, b1_ref, wu_ref, wd_ref,
                    g2_ref, b2_ref, out_ref, x1_ref, xb_ref, acc_ref):
    f = pl.program_id(1)
    last = pl.num_programs(1) - 1
    wc = yc_ref.shape[1]
    wm = ym_ref.shape[1]

    def mix(r):
        rows = pl.ds(r, MLP_SUB)
        h = jnp.dot(yc_ref[rows, :], wo_ref[0:wc, :], preferred_element_type=F32)
        h = h + jnp.dot(ym_ref[rows, :], wo_ref[wc:wc + wm, :], preferred_element_type=F32)
        return h + jnp.dot(yd_ref[rows, :], wo_ref[wc + wm:, :], preferred_element_type=F32)

    def step(first, final):
        n_rows = x_ref.shape[0]
        if first:
            h_next = mix(0)
        for r in range(0, n_rows, MLP_SUB):
            rows = pl.ds(r, MLP_SUB)
            if first:
                h = h_next
                if r + MLP_SUB < n_rows:
                    h_next = mix(r + MLP_SUB)
                x1 = _layer_norm_rows(DEEPNORM_ALPHA * x_ref[rows, :] + h,
                                      g1_ref[...], b1_ref[...])
                x1_ref[rows, :] = x1
                xb = x1.astype(BF16)
                xb_ref[rows, :] = xb
            else:
                xb = xb_ref[rows, :]
            up = jnp.maximum(jnp.dot(xb, wu_ref[...], preferred_element_type=F32), 0.0)
            down = jnp.dot((up * up).astype(BF16), wd_ref[...], preferred_element_type=F32)
            acc = down if first else acc_ref[rows, :] + down
            if final:
                out_ref[rows, :] = _layer_norm_rows(DEEPNORM_ALPHA * x1_ref[rows, :] + acc,
                                                    g2_ref[...], b2_ref[...])
            else:
                acc_ref[rows, :] = acc

    @pl.when(f == 0)
    def _():
        step(True, False)

    @pl.when(jnp.logical_and(f > 0, f < last))
    def _():
        step(False, False)

    @pl.when(f == last)
    def _():
        step(False, True)


def _out_mlp(yc, ym, yd, x2, wo, g1, b1, wu, wd, g2, b2):
    n, d = x2.shape
    dff = wu.shape[1]
    tm, tf = MLP_ROWS, MLP_FF
    assert dff // tf >= 2, "the first and last hidden steps are distinct code paths"
    rows = lambda i, f: (i, 0)
    const = lambda i, f: (0, 0)
    return pl.pallas_call(
        _out_mlp_kernel,
        grid=(n // tm, dff // tf),
        in_specs=[pl.BlockSpec((tm, yc.shape[1]), rows),
                  pl.BlockSpec((tm, ym.shape[1]), rows),
                  pl.BlockSpec((tm, yd.shape[1]), rows),
                  pl.BlockSpec((tm, d), rows),
                  pl.BlockSpec(wo.shape, const),
                  pl.BlockSpec(g1.shape, const),
                  pl.BlockSpec(b1.shape, const),
                  pl.BlockSpec((d, tf), lambda i, f: (0, f)),
                  pl.BlockSpec((tf, d), lambda i, f: (f, 0)),
                  pl.BlockSpec(g2.shape, const),
                  pl.BlockSpec(b2.shape, const)],
        out_specs=pl.BlockSpec((tm, d), rows),
        out_shape=jax.ShapeDtypeStruct((n, d), F32),
        scratch_shapes=[pltpu.VMEM((tm, d), F32), pltpu.VMEM((tm, d), BF16),
                        pltpu.VMEM((tm, d), F32)],
        compiler_params=_params("parallel", "arbitrary"),
        name="out_mlp_ln",
    )(yc, ym, yd, x2, wo, g1, b1, wu, wd, g2, b2)


def _row(v):
    return v.reshape(1, -1).astype(F32)


def _pad_rows(w, rows):
    return jnp.pad(w.astype(F32), ((0, rows - w.shape[0]), (0, 0)))


def _rearranged_w_in(w):
    d = w.shape[0]
    mk_lo = 256 + 256 + ML_HEADS * ML_HEAD_DIM
    mk_hi = mk_lo + ML_HEADS * ML_HEAD_DIM
    gate_lo = 256 + 256 + 512 + 256 + 256
    gate_hi = gate_lo + 2 * ML_HEADS
    v_lo = gate_hi + 2 * DA_HEADS * 2 * DA_QK_DIM
    pad = jnp.zeros((d, LANES - 2 * ML_HEADS), w.dtype)
    w_r = jnp.concatenate([w[:, :gate_lo], w[:, gate_hi:v_lo], w[:, gate_lo:gate_hi], pad], axis=1)
    return w_r.astype(BF16), w[:, v_lo:].T.astype(BF16), w[:, mk_lo:mk_hi].T.astype(BF16)


def kernel(x, w_in, b_igate, b_fgate, conv_dw_w, conv_dw_b, conv_ln_g, conv_ln_b, conv_pw_w, conv_pw_b, ml_conv_w, ml_conv_b, ml_norm_g, lam_q1, lam_k1, lam_q2, lam_k2, da_norm_g, w_out, ln1_g, ln1_b, w_up, w_down, ln2_g, ln2_b):
    bsz, s, d = x.shape
    n = bsz * s
    x2 = x.reshape(n, d)
    for l in range(DEPTH):
        lambda_init = 0.8 - 0.6 * math.exp(-0.3 * l)
        w_r, w_vt, w_kt = _rearranged_w_in(w_in[l])
        conv_params = (_pad_rows(conv_dw_w[l], 32), _row(conv_dw_b[l]), _row(conv_ln_g[l]),
                       _row(conv_ln_b[l]), conv_pw_w[l].astype(BF16), _row(conv_pw_b[l]))
        y_conv, mq, mv, mo, gates, dq, dk, dvt, mkt = _in_proj(x2, w_r, w_vt, w_kt,
                                                               conv_params, bsz)
        seq = lambda a: a.reshape(bsz, s, a.shape[1])

        gate_bias = jnp.pad(jnp.concatenate([b_igate[l], b_fgate[l]]).astype(F32),
                            (0, LANES - 2 * ML_HEADS)).reshape(1, LANES)
        wq = ML_HEADS * ML_HEAD_DIM
        cw_k = ml_conv_w[l][:, wq:].astype(F32)
        cb_k = ml_conv_b[l][wq:].astype(F32)
        y_ml = _mlstm(seq(mq), mkt, seq(mv), seq(mo), seq(gates),
                      _pad_rows(ml_conv_w[l][:, :wq], SUBLANES), _row(ml_conv_b[l][:wq]),
                      jnp.broadcast_to(cw_k[:, :, None], cw_k.shape + (ML_CHUNK,)),
                      jnp.broadcast_to(cb_k[:, None], cb_k.shape + (ML_CHUNK,)),
                      gate_bias, _row(ml_norm_g[l]))

        lam_p = jnp.stack([lam_q1[l], lam_k1[l], lam_q2[l], lam_k2[l]]).astype(F32)
        y_da = _diff_attention(seq(dq), seq(dk), dvt, lam_p,
                               da_norm_g[l].astype(F32).reshape(-1, 1), lambda_init)

        x2 = _out_mlp(y_conv.reshape(n, -1), y_ml.reshape(n, -1), y_da.reshape(n, -1), x2,
                      w_out[l].astype(BF16), _row(ln1_g[l]), _row(ln1_b[l]),
                      w_up[l].astype(BF16), w_down[l].astype(BF16),
                      _row(ln2_g[l]), _row(ln2_b[l]))
    return x2.reshape(bsz, s, d)
```

```python
import functools
import math

import jax
import jax.numpy as jnp
from jax import lax
from jax.experimental import pallas as pl
from jax.experimental.pallas import tpu as pltpu

F32 = jnp.float32
BF16 = jnp.bfloat16

LANES = 128
SUBLANES = 8
VMEM_LIMIT_BYTES = 56 * 1024 * 1024

DEPTH = 2
CONV_KSIZE = 31
ML_HEADS = 4
ML_HEAD_DIM = 64
ML_QK_CONV = 4
ML_CHUNK = 128
DA_HEADS = 4
DA_QK_DIM = 64
DA_V_DIM = 128
LN_EPS = 1e-5
DEEPNORM_ALPHA = (2 * DEPTH) ** 0.25

PROJ_ROWS = 1024
CONV_HALO = 32
CONV_SUB = 64
ATT_Q = 512
ATT_K = 512
ML_BATCH = 4
ATT_STEP_BLOCKS = 4
ATT_UNROLL = 4
ATT_SUM_ROWS = 16
MLP_ROWS = 1024
MLP_FF = 1024
MLP_SUB = 256


def _params(*semantics):
    return pltpu.CompilerParams(dimension_semantics=semantics,
                                vmem_limit_bytes=VMEM_LIMIT_BYTES)


def _sigmoid(x):
    return 1.0 / (1.0 + jnp.exp(-x))


def _layer_norm_rows(x, g, b):
    mu = jnp.mean(x, axis=-1, keepdims=True)
    xc = x - mu
    var = jnp.mean(xc * xc, axis=-1, keepdims=True)
    return xc * lax.rsqrt(var + LN_EPS) * g + b


def _in_proj_kernel(x_ref, w_ref, wvt_ref, wkt_ref, dw_ref, dwb_ref, cg_ref, cb_ref, pw_ref,
                    pwb_ref, yconv_ref, mq_ref, mv_ref, mo_ref, gate_ref, dq_ref, dk_ref,
                    dvt_ref, mkt_ref, xs, ybuf, *, tiles_per_seq):
    rows = x_ref.shape[0]
    xb = x_ref[...].astype(BF16)

    def sec(lo, width):
        return jnp.dot(xb, w_ref[:, lo:lo + width], preferred_element_type=F32)

    @pl.when(pl.program_id(0) % tiles_per_seq == 0)
    def _():
        xs[0, 0:CONV_HALO, :] = jnp.zeros((CONV_HALO, xs.shape[2]), F32)

    xs[0, CONV_HALO:CONV_HALO + rows, :] = sec(0, 256) * _sigmoid(sec(256, 256))
    span = CONV_HALO + rows - SUBLANES
    for k in range(1, SUBLANES):
        xs[k, 0:span, :] = xs[0, k:k + span, :]

    def conv_block(r, taps):
        first = CONV_HALO - (CONV_KSIZE - 1)
        acc = jnp.broadcast_to(dwb_ref[...], (CONV_SUB, xs.shape[2]))
        for j in range(CONV_KSIZE):
            k = (first + j) % SUBLANES
            lo = r + first + j - k
            acc = acc + taps[j:j + 1, :] * xs[k, lo:lo + CONV_SUB, :]
        y = _layer_norm_rows(acc, cg_ref[...], cb_ref[...])
        ybuf[r:r + CONV_SUB, :] = (y * _sigmoid(y)).astype(BF16)

    def proj_mqk():
        res = sec(512, 256)
        mq_ref[...] = res
        kt = lax.dot_general(wkt_ref[...], xb, (((1,), (1,)), ((), ())),
                             preferred_element_type=F32)
        for ck in range(mkt_ref.shape[1]):
            mkt_ref[0, ck] = kt[:, ck * ML_CHUNK:(ck + 1) * ML_CHUNK]
        return res

    def proj_mv_mo():
        mv_ref[...] = sec(1024, 256)
        res = sec(1280, 256)
        mo_ref[...] = res
        return res

    def proj_dq():
        res = sec(1536, 512)
        dq_ref[...] = (res * (DA_QK_DIM ** -0.5 * math.log2(math.e))).astype(BF16)
        return res

    def proj_dk_gate_vt():
        res = sec(2048, 512)
        dk_ref[...] = res.astype(BF16)
        gate_ref[...] = sec(2560, LANES)
        vt = lax.dot_general(wvt_ref[...], xb, (((1,), (1,)), ((), ())),
                             preferred_element_type=F32).astype(BF16)
        for kb in range(dvt_ref.shape[1]):
            dvt_ref[0, kb] = vt[:, kb * ATT_K:(kb + 1) * ATT_K]
        return res

    blocks_per_stage = rows // CONV_SUB // 4
    stages = [(proj, blocks_per_stage)
              for proj in (proj_mqk, proj_mv_mo, proj_dq, proj_dk_gate_vt)]

    def exact_zero(x):
        bits = lax.shift_right_logical(pltpu.bitcast(x, jnp.uint32), jnp.uint32(32))
        return pltpu.bitcast(bits, F32)

    pw_rows = blocks_per_stage * CONV_SUB
    taps = dw_ref[...]
    done = 0
    for idx, (piece, n_blocks) in enumerate(stages):
        res = piece()
        for _ in range(n_blocks):
            conv_block(done, taps)
            done += CONV_SUB
            if done % pw_rows == 0:
                r0 = done - pw_rows
                yconv_ref[r0:done, :] = (jnp.dot(ybuf[r0:done, :], pw_ref[...],
                                                 preferred_element_type=F32)
                                         + pwb_ref[...]).astype(BF16)
        if idx + 1 < len(stages) and stages[idx + 1][1]:
            taps = dw_ref[...] + exact_zero(res[res.shape[0] - 1:, 0:xs.shape[2]])
    xs[0, 0:CONV_HALO, :] = xs[0, rows:rows + CONV_HALO, :]


def _in_proj(x2, w_r, w_vt, w_kt, conv_params, bsz):
    n, d = x2.shape
    tm = PROJ_ROWS
    kb_per_tile = tm // ATT_K
    ck_per_tile = tm // ML_CHUNK
    tiles_per_seq = n // bsz // tm
    widths = (256, 256, 256, 256, LANES, 512, 512)
    dtypes = (BF16, F32, F32, F32, F32, BF16, BF16)
    wv = w_vt.shape[0]
    wk = w_kt.shape[0]
    const = lambda i: (0, 0)
    slabs = lambda i: (i // tiles_per_seq, i % tiles_per_seq, 0, 0)
    return pl.pallas_call(
        functools.partial(_in_proj_kernel, tiles_per_seq=tiles_per_seq),
        grid=(n // tm,),
        in_specs=[pl.BlockSpec((tm, d), lambda i: (i, 0)),
                  pl.BlockSpec(w_r.shape, const),
                  pl.BlockSpec(w_vt.shape, const),
                  pl.BlockSpec(w_kt.shape, const)]
        + [pl.BlockSpec(p.shape, const) for p in conv_params],
        out_specs=[pl.BlockSpec((tm, w), lambda i: (i, 0)) for w in widths]
        + [pl.BlockSpec((1, kb_per_tile, wv, ATT_K), slabs),
           pl.BlockSpec((1, ck_per_tile, wk, ML_CHUNK), slabs)],
        out_shape=[jax.ShapeDtypeStruct((n, w), dt) for w, dt in zip(widths, dtypes)]
        + [jax.ShapeDtypeStruct((bsz, tiles_per_seq * kb_per_tile, wv, ATT_K), BF16),
           jax.ShapeDtypeStruct((bsz, tiles_per_seq * ck_per_tile, wk, ML_CHUNK), F32)],
        scratch_shapes=[pltpu.VMEM((SUBLANES, CONV_HALO + tm, widths[0]), F32),
                        pltpu.VMEM((tm, widths[0]), BF16)],
        compiler_params=_params("arbitrary"),
        name="in_proj_conv",
    )(x2, w_r, w_vt, w_kt, *conv_params)


def _mlstm_kernel(qk_ref, halo_ref, kt_ref, ktp_ref, v_ref, o_ref, gate_ref, cw_ref, cb_ref,
                  cwk_ref, cbk_ref, gb_ref, ng_ref, sel_ref, avg_ref, out_ref, qpad, ct_ref,
                  m_ref):
    c = pl.program_id(1)
    nb = qk_ref.shape[0]
    t = ML_CHUNK
    w_pair = 2 * ML_HEAD_DIM
    n_pairs = ML_HEADS // 2
    tn = (((0,), (0,)), ((), ()))

    @pl.when(c == 0)
    def _():
        ct_ref[...] = jnp.zeros_like(ct_ref)
        m_ref[...] = jnp.zeros_like(m_ref)

    row = lax.broadcasted_iota(jnp.int32, (t, t), 0)
    col = lax.broadcasted_iota(jnp.int32, (t, t), 1)
    causal = row >= col
    tri = causal.astype(F32)
    lane = lax.broadcasted_iota(jnp.int32, (t, w_pair), 1)
    low = lane < ML_HEAD_DIM
    owns = (low, lane >= ML_HEAD_DIM)
    own_rows = (row < ML_HEAD_DIM, row >= ML_HEAD_DIM)

    first = SUBLANES - (ML_QK_CONV - 1)
    lane_k = lax.broadcasted_iota(jnp.int32, (kt_ref.shape[2], t), 1)
    qk, kt, gpre_t, log_f_t = [], [], [], []
    for bb in range(nb):
        qpad[bb, 0:SUBLANES, :] = jnp.where(c > 0, halo_ref[bb], 0.0)
        qpad[bb, SUBLANES:SUBLANES + t, :] = qk_ref[bb]
        acc = jnp.broadcast_to(cb_ref[...], (t, qk_ref.shape[2]))
        for j in range(ML_QK_CONV):
            acc = acc + cw_ref[j:j + 1, :] * qpad[bb, first + j:first + j + t, :]
        qk.append(acc * _sigmoid(acc))
        cur = kt_ref[bb, 0]
        prev = jnp.where(c > 0, ktp_ref[bb, 0], 0.0)
        acc = cbk_ref[...] + cwk_ref[ML_QK_CONV - 1] * cur
        for s in range(1, ML_QK_CONV):
            shifted = jnp.where(lane_k >= s, pltpu.roll(cur, s, 1), pltpu.roll(prev, s, 1))
            acc = acc + cwk_ref[ML_QK_CONV - 1 - s] * shifted
        kt.append(acc * _sigmoid(acc) * (ML_HEAD_DIM ** -0.5))
        g_t = (gate_ref[bb] + gb_ref[...]).T[0:SUBLANES, :]
        gpre_t.append(g_t)
        log_f_t.append(jnp.minimum(g_t, 0.0) - jnp.log(1.0 + jnp.exp(-jnp.abs(g_t))))

    tri_t = (row <= col).astype(F32)
    cums = [jnp.dot(lf, tri_t, preferred_element_type=F32, precision=lax.Precision.HIGHEST)
            for lf in log_f_t]

    items = [(bb, h) for bb in range(nb) for h in range(ML_HEADS)]
    st = {}
    for bb in range(nb):
        for pair in range(n_pairs):
            lo = pair * w_pair
            st[bb, pair, "q"] = qk[bb][:, lo:lo + w_pair]
            st[bb, pair, "kt"] = kt[bb][lo:lo + w_pair, :]
            st[bb, pair, "kbt"] = st[bb, pair, "kt"].astype(BF16)
            st[bb, pair, "v"] = v_ref[bb, :, lo:lo + w_pair]
    for (bb, h) in items:
        pair, par = divmod(h, 2)
        st[bb, h, "q_m"] = jnp.where(owns[par], st[bb, pair, "q"], 0.0).astype(BF16)
        st[bb, h, "vx"] = jnp.where(owns[par], st[bb, pair, "v"], 1.0).astype(BF16)
    for (bb, h) in items:
        st[bb, h, "s_qk"] = jnp.dot(st[bb, h, "q_m"], st[bb, h // 2, "kbt"],
                                    preferred_element_type=F32)
    for (bb, h) in items:
        st[bb, h, "ct_prev"] = ct_ref[bb, h]
        st[bb, h, "inter"] = jnp.dot(st[bb, h, "q_m"], st[bb, h, "ct_prev"].astype(BF16),
                                     preferred_element_type=F32)

    head_row = lax.broadcasted_iota(jnp.int32, (SUBLANES, t), 0) < ML_HEADS
    lane_t = lax.broadcasted_iota(jnp.int32, (SUBLANES, t), 1)
    def last_lane(x):
        return jnp.broadcast_to(x[:, t - 1:t], x.shape)

    r_t, p_end_t, a_st_t, e_st_t, g_hl = [], [], [], [], []
    for bb in range(nb):
        b_t = pltpu.roll(cums[bb], ML_HEADS, 0)
        b_t = jnp.where(head_row, b_t, 0.0)
        r = jnp.where(head_row, gpre_t[bb] - b_t, 0.0)
        cm = r
        shift = 1
        while shift < t:
            cm = jnp.where(lane_t >= shift, jnp.maximum(cm, pltpu.roll(cm, shift, 1)), cm)
            shift *= 2
        m_prev = m_ref[bb]
        u = jnp.maximum(m_prev, cm)
        a_inter = jnp.exp(m_prev - u)
        floor = jnp.exp(-(b_t + u))
        g_end = last_lane(b_t)
        cm_end = last_lane(cm)
        p_end = jnp.exp(r - cm_end)
        m_loc = g_end + cm_end
        m_new = jnp.maximum(g_end + m_prev, m_loc)
        a_st_t.append(jnp.exp(g_end + m_prev - m_new))
        e_st_t.append(jnp.exp(m_loc - m_new))
        m_ref[bb] = jnp.where(head_row, m_new, 0.0)
        r_t.append(r)
        p_end_t.append(p_end)
        parts = []
        for tile in (u, a_inter, floor):
            hi = tile.astype(BF16)
            parts += [hi, (tile - hi.astype(F32)).astype(BF16)]
        g_hl.append(jnp.concatenate(parts, axis=0))
    rep = {(bb, h): lax.dot_general(g_hl[bb], sel_ref[h], tn, preferred_element_type=F32)
           for bb in range(nb) for h in range(ML_HEADS)}

    def tile_rows(x_row):
        return jnp.tile(jnp.broadcast_to(x_row, (SUBLANES, LANES)), (t // SUBLANES, 1))

    for (bb, h) in items:
        pair, par = divmod(h, 2)
        u_col = rep[bb, h][:, 0:LANES]
        st[bb, h, "a_inter"] = rep[bb, h][:, LANES:2 * LANES]
        st[bb, h, "floor"] = rep[bb, h][:, 2 * LANES:3 * LANES]
        r_row = tile_rows(r_t[bb][h:h + 1, :])
        st[bb, h, "dexp"] = jnp.exp(jnp.where(causal, r_row - u_col, -jnp.inf))
        kp_t = (jnp.where(own_rows[par], st[bb, pair, "kt"], 0.0)
                * tile_rows(p_end_t[bb][h:h + 1, :])).astype(BF16)
        c_loc = jnp.dot(kp_t, st[bb, h, "vx"], preferred_element_type=F32)
        ct_ref[bb, h] = (tile_rows(a_st_t[bb][h:h + 1, :]) * st[bb, h, "ct_prev"]
                         + tile_rows(e_st_t[bb][h:h + 1, :]) * c_loc)

    for (bb, h) in items:
        sc = (st[bb, h, "s_qk"] * st[bb, h, "dexp"]).astype(BF16)
        st[bb, h, "intra"] = jnp.dot(sc, st[bb, h, "vx"], preferred_element_type=F32)
    def half_mean(x):
        hi = x.astype(BF16)
        lo = (x - hi.astype(F32)).astype(BF16)
        return jnp.dot(jnp.concatenate([hi, lo], axis=1), avg_ref[...],
                       preferred_element_type=F32)

    for bb in range(nb):
        for pair in range(n_pairs):
            lo = pair * w_pair
            h0, h1 = 2 * pair, 2 * pair + 1
            nd0 = st[bb, h0, "a_inter"] * st[bb, h0, "inter"] + st[bb, h0, "intra"]
            nd1 = st[bb, h1, "a_inter"] * st[bb, h1, "inter"] + st[bb, h1, "intra"]
            den = pltpu.roll(jnp.where(low, nd1, nd0), ML_HEAD_DIM, 1)
            floor = jnp.where(low, st[bb, h0, "floor"], st[bb, h1, "floor"])
            h_pair = jnp.where(low, nd0, nd1) / jnp.maximum(jnp.abs(den), floor)
            x = h_pair * _sigmoid(o_ref[bb, :, lo:lo + w_pair])
            xc = x - half_mean(x)
            var = half_mean(xc * xc)
            y = xc * lax.rsqrt(var + LN_EPS) * ng_ref[:, lo:lo + w_pair]
            out_ref[bb, :, lo:lo + w_pair] = y.astype(BF16)


def _mlstm(mq, mkt, mv, mo, gates, cw, cb, cwk, cbk, gb, ng):
    bsz, s, wqk = mq.shape
    wv = mv.shape[2]
    wk = mkt.shape[2]
    t = ML_CHUNK
    nb = math.gcd(bsz, ML_BATCH)
    halo_per_chunk = t // SUBLANES
    n_rep = 3
    src = jnp.arange(n_rep * 2 * SUBLANES)[:, None]
    dst = jnp.arange(n_rep * LANES)[None, :] // LANES
    sel = jnp.stack([((src // (2 * SUBLANES) == dst) & (src % SUBLANES == h)).astype(BF16)
                     for h in range(ML_HEADS)])
    grp = lambda idx: (idx % LANES) // ML_HEAD_DIM
    avg = jnp.where(grp(jnp.arange(2 * LANES))[:, None] == grp(jnp.arange(LANES))[None, :],
                    1.0 / ML_HEAD_DIM, 0.0).astype(BF16)
    const = lambda bi, ci: (0, 0)
    return pl.pallas_call(
        _mlstm_kernel,
        grid=(bsz // nb, s // t),
        in_specs=[
            pl.BlockSpec((nb, t, wqk), lambda bi, ci: (bi, ci, 0)),
            pl.BlockSpec((nb, SUBLANES, wqk),
                         lambda bi, ci: (bi, jnp.maximum(ci * halo_per_chunk - 1, 0), 0)),
            pl.BlockSpec((nb, 1, wk, t), lambda bi, ci: (bi, ci, 0, 0)),
            pl.BlockSpec((nb, 1, wk, t), lambda bi, ci: (bi, jnp.maximum(ci - 1, 0), 0, 0)),
            pl.BlockSpec((nb, t, wv), lambda bi, ci: (bi, ci, 0)),
            pl.BlockSpec((nb, t, wv), lambda bi, ci: (bi, ci, 0)),
            pl.BlockSpec((nb, t, LANES), lambda bi, ci: (bi, ci, 0)),
            pl.BlockSpec(cw.shape, const),
            pl.BlockSpec(cb.shape, const),
            pl.BlockSpec(cwk.shape, lambda bi, ci: (0, 0, 0)),
            pl.BlockSpec(cbk.shape, const),
            pl.BlockSpec(gb.shape, const),
            pl.BlockSpec(ng.shape, const),
            pl.BlockSpec(sel.shape, lambda bi, ci: (0, 0, 0)),
            pl.BlockSpec(avg.shape, const),
        ],
        out_specs=pl.BlockSpec((nb, t, wv), lambda bi, ci: (bi, ci, 0)),
        out_shape=jax.ShapeDtypeStruct((bsz, s, wv), BF16),
        scratch_shapes=[pltpu.VMEM((nb, SUBLANES + t, wqk), F32),
                        pltpu.VMEM((nb, ML_HEADS, 2 * ML_HEAD_DIM, LANES), F32),
                        pltpu.VMEM((nb, SUBLANES, LANES), F32)],
        compiler_params=_params("parallel", "arbitrary"),
        name="mlstm",
    )(mq, mq, mkt, mkt, mv, mo, gates, cw, cb, cwk, cbk, gb, ng, sel, avg)


def _attn_kernel(q_ref, qn_ref, k_ref, vt_ref, lam_ref, ng_ref, out_ref, qall_ref, qm_ref,
                 acc_ref, s0_ref, s1_ref, mblk_ref, m_ref, *, lambda_init):
    tq = qn_ref.shape[1]
    tk = vt_ref.shape[3]
    n_sub = q_ref.shape[1] // tq
    for t in range(n_sub):
        qall_ref[t] = q_ref[0, t * tq:(t + 1) * tq, :]
    qall_ref[n_sub] = qn_ref[0]
    _attn_init_block(qall_ref[0], qm_ref, acc_ref, m_ref)

    def query_block(t, carry):
        _attn_query_block(pl.program_id(2) * n_sub + t, qall_ref.at[t], qall_ref.at[t + 1],
                          k_ref, vt_ref, lam_ref, ng_ref,
                          out_ref.at[0, pl.ds(pl.multiple_of(t * tq, tq), tq), :],
                          qm_ref, acc_ref, s0_ref, s1_ref, mblk_ref, m_ref, lambda_init)
        return carry

    lax.fori_loop(0, n_sub, query_block, 0)


def _attn_map_operands(q):
    lane = lax.broadcasted_iota(jnp.int32, q.shape, 1)
    zero = jnp.zeros_like(q)
    return jnp.where(lane < DA_QK_DIM, q, zero), jnp.where(lane >= DA_QK_DIM, q, zero)


def _attn_init_block(q, qm_ref, acc_ref, m_ref):
    qm_ref[0], qm_ref[1] = _attn_map_operands(q)
    acc_ref[...] = jnp.zeros_like(acc_ref)
    m_ref[...] = jnp.full_like(m_ref, -jnp.inf)


def _attn_query_block(i, q_ref, qn_ref, k_ref, vt_ref, lam_ref, ng_ref, out_ref, qm_ref,
                      acc_ref, s0_ref, s1_ref, mblk_ref, m_ref, lambda_init):
    tq = q_ref.shape[0]
    tk = vt_ref.shape[3]
    map_operands = _attn_map_operands

    ones_rows = jnp.ones((ATT_SUM_ROWS, tk), BF16)

    def scores(j, s_ref, slot, qms=None, diagonal=False):
        nt = (((1,), (1,)), ((), ()))
        if diagonal:
            hk, hq = tk // 2, tq // 2
            k_lo = k_ref[0, pl.ds(pl.multiple_of(j * tk, tk), hk), :]
            k_hi = k_ref[0, pl.ds(pl.multiple_of(j * tk + hk, hk), hk), :]
            for mp in range(2):
                s_ref[mp, 0:hk, :] = lax.dot_general(k_lo, qm_ref[mp], nt,
                                                     preferred_element_type=F32)
                s_ref[mp, hk:tk, hq:tq] = lax.dot_general(k_hi, qm_ref[mp, hq:tq, :], nt,
                                                          preferred_element_type=F32)
            return
        k_blk = k_ref[0, pl.ds(pl.multiple_of(j * tk, tk), tk), :]
        for mp in range(2):
            qm = qm_ref[mp] if qms is None else qms[mp]
            st = lax.dot_general(k_blk, qm, (((1,), (1,)), ((), ())),
                                 preferred_element_type=F32)
            s_ref[mp] = st
            mblk_ref[slot, mp] = jnp.max(st, axis=0, keepdims=True)

    def softmax_pv(j, s_ref, slot, masked):
        vt_blk = jnp.concatenate([vt_ref[0, j], ones_rows], axis=0)
        if masked:
            hk, hq = tk // 2, tq // 2
            tri = (lax.broadcasted_iota(jnp.int32, (hk, hq), 0)
                   <= lax.broadcasted_iota(jnp.int32, (hk, hq), 1))
            for mp in range(2):
                st_a = jnp.where(tri, s_ref[mp, 0:hk, 0:hq], -jnp.inf)
                st_b = s_ref[mp, 0:hk, hq:tq]
                st_c = jnp.where(tri, s_ref[mp, hk:tk, hq:tq], -jnp.inf)
                m_blk = jnp.concatenate(
                    [jnp.max(st_a, axis=0, keepdims=True),
                     jnp.maximum(jnp.max(st_b, axis=0, keepdims=True),
                                 jnp.max(st_c, axis=0, keepdims=True))], axis=1)
                m_old = m_ref[mp]
                m_new = jnp.maximum(m_old, m_blk)
                alpha = jnp.exp2(m_old - m_new)
                p_a = jnp.exp2(st_a - m_new[:, 0:hq]).astype(BF16)
                p_bc = jnp.concatenate([jnp.exp2(st_b - m_new[:, hq:tq]),
                                        jnp.exp2(st_c - m_new[:, hq:tq])], axis=0).astype(BF16)
                acc_ref[mp, :, 0:hq] = (alpha[:, 0:hq] * acc_ref[mp, :, 0:hq]
                                        + jnp.dot(vt_blk[:, 0:hk], p_a,
                                                  preferred_element_type=F32))
                acc_ref[mp, :, hq:tq] = (alpha[:, hq:tq] * acc_ref[mp, :, hq:tq]
                                         + jnp.dot(vt_blk, p_bc, preferred_element_type=F32))
                m_ref[mp] = m_new
            return
        for mp in range(2):
            st = s_ref[mp]
            m_blk = mblk_ref[slot, mp]
            m_old = m_ref[mp]
            m_new = jnp.maximum(m_old, m_blk)
            alpha = jnp.exp2(m_old - m_new)
            pexp = jnp.exp2(st - m_new).astype(BF16)
            acc_ref[mp] = alpha * acc_ref[mp] + jnp.dot(vt_blk, pexp,
                                                       preferred_element_type=F32)
            m_ref[mp] = m_new

    def run(bufs):
        @pl.when(i == 0)
        def _():
            scores(0, *bufs[0])

        def advance(first, count, ends_on_diagonal=False):
            for u in range(count):
                scores(first + u + 1, *bufs[(u + 1) % 2],
                       diagonal=ends_on_diagonal and u == count - 1)
                softmax_pv(first + u, *bufs[u % 2], False)

        def group(jj, carry):
            advance(ATT_UNROLL * jj, ATT_UNROLL)
            return carry

        lax.fori_loop(0, i // ATT_UNROLL, group, 0)

        for rest in range(ATT_UNROLL):
            @pl.when(i % ATT_UNROLL == rest)
            def _(rest=rest):
                advance(i - rest, rest, ends_on_diagonal=True)
                scores(0, *bufs[(rest + 1) % 2], map_operands(qn_ref[...]))
                softmax_pv(i, *bufs[rest % 2], True)

    start = ((i + 1) // 2) % 2
    buf0, buf1 = (s0_ref, 0), (s1_ref, 1)

    @pl.when(start == 0)
    def _():
        run((buf0, buf1))

    @pl.when(start == 1)
    def _():
        run((buf1, buf0))

    lp = lam_ref[...]
    lam = (jnp.exp(jnp.sum(lp[0:1, :] * lp[1:2, :], axis=-1, keepdims=True))
           - jnp.exp(jnp.sum(lp[2:3, :] * lp[3:4, :], axis=-1, keepdims=True))
           + lambda_init)
    dv = DA_V_DIM
    o_t = (acc_ref[0, 0:dv, :] / acc_ref[0, dv:dv + 1, :]
           - lam * (acc_ref[1, 0:dv, :] / acc_ref[1, dv:dv + 1, :]))
    ms = jnp.mean(o_t * o_t, axis=0, keepdims=True)
    y_t = o_t * lax.rsqrt(ms + LN_EPS) * (ng_ref[...] * (1.0 - lambda_init))
    out_ref[...] = y_t.T.astype(BF16)
    _attn_init_block(qn_ref[...], qm_ref, acc_ref, m_ref)


def _diff_attention(dq, dk, dvt, lam_p, ng_col, lambda_init):
    bsz, s, _ = dq.shape
    tq = ATT_Q
    nk, tk = dvt.shape[1], dvt.shape[3]
    assert tq == tk and nk * tk == s
    nq = s // tq
    g = math.gcd(nq, ATT_STEP_BLOCKS)
    return pl.pallas_call(
        functools.partial(_attn_kernel, lambda_init=lambda_init),
        grid=(bsz, DA_HEADS, nq // g),
        in_specs=[
            pl.BlockSpec((1, g * tq, 2 * DA_QK_DIM), lambda b, h, i: (b, i, h)),
            pl.BlockSpec((1, tq, 2 * DA_QK_DIM),
                         lambda b, h, i: (b, jnp.minimum((i + 1) * g, nq - 1), h)),
            pl.BlockSpec((1, s, 2 * DA_QK_DIM), lambda b, h, i: (b, 0, h)),
            pl.BlockSpec((1, nk, DA_V_DIM, tk), lambda b, h, i: (b, 0, h, 0)),
            pl.BlockSpec(lam_p.shape, lambda b, h, i: (0, 0)),
            pl.BlockSpec(ng_col.shape, lambda b, h, i: (0, 0)),
        ],
        out_specs=pl.BlockSpec((1, g * tq, DA_V_DIM), lambda b, h, i: (b, i, h)),
        out_shape=jax.ShapeDtypeStruct((bsz, s, DA_HEADS * DA_V_DIM), BF16),
        scratch_shapes=[pltpu.VMEM((g + 1, tq, 2 * DA_QK_DIM), BF16),
                        pltpu.VMEM((2, tq, 2 * DA_QK_DIM), BF16),
                        pltpu.VMEM((2, DA_V_DIM + ATT_SUM_ROWS, tq), F32),
                        pltpu.VMEM((2, tk, tq), F32),
                        pltpu.VMEM((2, tk, tq), F32),
                        pltpu.VMEM((2, 2, 1, tq), F32),
                        pltpu.VMEM((2, 1, tq), F32)],
        compiler_params=_params("parallel", "parallel", "arbitrary"),
        name="diff_attention",
    )(dq, dq, dk, dvt, lam_p, ng_col)


def _out_mlp_kernel(yc_ref, ym_ref, yd_ref, x_ref, wo_ref, g1_ref, b1_ref, wu_ref, wd_ref,
                    g2_ref, b2_ref, out_ref, x1_ref, xb_ref, acc_ref):
    f = pl.program_id(1)
    last = pl.num_programs(1) - 1
    wc = yc_ref.shape[1]
    wm = ym_ref.shape[1]

    def mix(r):
        rows = pl.ds(r, MLP_SUB)
        h = jnp.dot(yc_ref[rows, :], wo_ref[0:wc, :], preferred_element_type=F32)
        h = h + jnp.dot(ym_ref[rows, :], wo_ref[wc:wc + wm, :], preferred_element_type=F32)
        return h + jnp.dot(yd_ref[rows, :], wo_ref[wc + wm:, :], preferred_element_type=F32)

    def step(first, final):
        n_rows = x_ref.shape[0]
        if first:
            h_next = mix(0)
        for r in range(0, n_rows, MLP_SUB):
            rows = pl.ds(r, MLP_SUB)
            if first:
                h = h_next
                if r + MLP_SUB < n_rows:
                    h_next = mix(r + MLP_SUB)
                x1 = _layer_norm_rows(DEEPNORM_ALPHA * x_ref[rows, :] + h,
                                      g1_ref[...], b1_ref[...])
                x1_ref[rows, :] = x1
                xb = x1.astype(BF16)
                xb_ref[rows, :] = xb
            else:
                xb = xb_ref[rows, :]
            up = jnp.maximum(jnp.dot(xb, wu_ref[...], preferred_element_type=F32), 0.0)
            down = jnp.dot((up * up).astype(BF16), wd_ref[...], preferred_element_type=F32)
            acc = down if first else acc_ref[rows, :] + down
            if final:
                out_ref[rows, :] = _layer_norm_rows(DEEPNORM_ALPHA * x1_ref[rows, :] + acc,
                                                    g2_ref[...], b2_ref[...])
            else:
                acc_ref[rows, :] = acc

    @pl.when(f == 0)
    def _():
        step(True, False)

    @pl.when(jnp.logical_and(f > 0, f < last))
    def _():
        step(False, False)

    @pl.when(f == last)
    def _():
        step(False, True)


def _out_mlp(yc, ym, yd, x2, wo, g1, b1, wu, wd, g2, b2):
    n, d = x2.shape
    dff = wu.shape[1]
    tm, tf = MLP_ROWS, MLP_FF
    assert dff // tf >= 2, "the first and last hidden steps are distinct code paths"
    rows = lambda i, f: (i, 0)
    const = lambda i, f: (0, 0)
    return pl.pallas_call(
        _out_mlp_kernel,
        grid=(n // tm, dff // tf),
        in_specs=[pl.BlockSpec((tm, yc.shape[1]), rows),
                  pl.BlockSpec((tm, ym.shape[1]), rows),
                  pl.BlockSpec((tm, yd.shape[1]), rows),
                  pl.BlockSpec((tm, d), rows),
                  pl.BlockSpec(wo.shape, const),
                  pl.BlockSpec(g1.shape, const),
                  pl.BlockSpec(b1.shape, const),
                  pl.BlockSpec((d, tf), lambda i, f: (0, f)),
                  pl.BlockSpec((tf, d), lambda i, f: (f, 0)),
                  pl.BlockSpec(g2.shape, const),
                  pl.BlockSpec(b2.shape, const)],
        out_specs=pl.BlockSpec((tm, d), rows),
        out_shape=jax.ShapeDtypeStruct((n, d), F32),
        scratch_shapes=[pltpu.VMEM((tm, d), F32), pltpu.VMEM((tm, d), BF16),
                        pltpu.VMEM((tm, d), F32)],
        compiler_params=_params("parallel", "arbitrary"),
        name="out_mlp_ln",
    )(yc, ym, yd, x2, wo, g1, b1, wu, wd, g2, b2)


def _out_mlp_resident_kernel(yc_ref, ym_ref, yd_ref, x_ref, wo_ref, g1_ref, b1_ref, wu_ref,
                             wd_ref, g2_ref, b2_ref, out_ref):
    wc = yc_ref.shape[1]
    wm = ym_ref.shape[1]
    n_rows = x_ref.shape[0]
    dff = wu_ref.shape[1]

    def mix(r):
        rows = pl.ds(r, MLP_SUB)
        h = jnp.dot(yc_ref[rows, :], wo_ref[0:wc, :], preferred_element_type=F32)
        h = h + jnp.dot(ym_ref[rows, :], wo_ref[wc:wc + wm, :], preferred_element_type=F32)
        return h + jnp.dot(yd_ref[rows, :], wo_ref[wc + wm:, :], preferred_element_type=F32)

    h_next = mix(0)
    for r in range(0, n_rows, MLP_SUB):
        rows = pl.ds(r, MLP_SUB)
        h = h_next
        if r + MLP_SUB < n_rows:
            h_next = mix(r + MLP_SUB)
        x1 = _layer_norm_rows(DEEPNORM_ALPHA * x_ref[rows, :] + h, g1_ref[...], b1_ref[...])
        xb = x1.astype(BF16)
        acc = None
        for f in range(0, dff, MLP_FF):
            up = jnp.maximum(jnp.dot(xb, wu_ref[:, f:f + MLP_FF],
                                     preferred_element_type=F32), 0.0)
            down = jnp.dot((up * up).astype(BF16), wd_ref[f:f + MLP_FF, :],
                           preferred_element_type=F32)
            acc = down if acc is None else acc + down
        out_ref[rows, :] = _layer_norm_rows(DEEPNORM_ALPHA * x1 + acc, g2_ref[...], b2_ref[...])


def _out_mlp_resident(yc, ym, yd, x2, wo, g1, b1, wu, wd, g2, b2):
    n, d = x2.shape
    tm = MLP_ROWS
    rows = lambda i: (i, 0)
    once = lambda a: pl.BlockSpec(a.shape, lambda i: (0, 0), pipeline_mode=pl.Buffered(1))
    return pl.pallas_call(
        _out_mlp_resident_kernel,
        grid=(n // tm,),
        in_specs=[pl.BlockSpec((tm, yc.shape[1]), rows),
                  pl.BlockSpec((tm, ym.shape[1]), rows),
                  pl.BlockSpec((tm, yd.shape[1]), rows),
                  pl.BlockSpec((tm, d), rows),
                  once(wo), once(g1), once(b1), once(wu), once(wd), once(g2), once(b2)],
        out_specs=pl.BlockSpec((tm, d), rows),
        out_shape=jax.ShapeDtypeStruct((n, d), F32),
        compiler_params=_params("parallel"),
        name="out_mlp_ln",
    )(yc, ym, yd, x2, wo, g1, b1, wu, wd, g2, b2)


def _row(v):
    return v.reshape(1, -1).astype(F32)


def _pad_rows(w, rows):
    return jnp.pad(w.astype(F32), ((0, rows - w.shape[0]), (0, 0)))


def _rearranged_w_in(w):
    d = w.shape[0]
    mk_lo = 256 + 256 + ML_HEADS * ML_HEAD_DIM
    mk_hi = mk_lo + ML_HEADS * ML_HEAD_DIM
    gate_lo = 256 + 256 + 512 + 256 + 256
    gate_hi = gate_lo + 2 * ML_HEADS
    v_lo = gate_hi + 2 * DA_HEADS * 2 * DA_QK_DIM
    pad = jnp.zeros((d, LANES - 2 * ML_HEADS), w.dtype)
    w_r = jnp.concatenate([w[:, :gate_lo], w[:, gate_hi:v_lo], w[:, gate_lo:gate_hi], pad], axis=1)
    return w_r.astype(BF16), w[:, v_lo:].T.astype(BF16), w[:, mk_lo:mk_hi].T.astype(BF16)


def kernel(x, w_in, b_igate, b_fgate, conv_dw_w, conv_dw_b, conv_ln_g, conv_ln_b, conv_pw_w, conv_pw_b, ml_conv_w, ml_conv_b, ml_norm_g, lam_q1, lam_k1, lam_q2, lam_k2, da_norm_g, w_out, ln1_g, ln1_b, w_up, w_down, ln2_g, ln2_b):
    bsz, s, d = x.shape
    n = bsz * s
    x2 = x.reshape(n, d)
    for l in range(DEPTH):
        lambda_init = 0.8 - 0.6 * math.exp(-0.3 * l)
        w_r, w_vt, w_kt = _rearranged_w_in(w_in[l])
        conv_params = (_pad_rows(conv_dw_w[l], 32), _row(conv_dw_b[l]), _row(conv_ln_g[l]),
                       _row(conv_ln_b[l]), conv_pw_w[l].astype(BF16), _row(conv_pw_b[l]))
        y_conv, mq, mv, mo, gates, dq, dk, dvt, mkt = _in_proj(x2, w_r, w_vt, w_kt,
                                                               conv_params, bsz)
        seq = lambda a: a.reshape(bsz, s, a.shape[1])

        gate_bias = jnp.pad(jnp.concatenate([b_igate[l], b_fgate[l]]).astype(F32),
                            (0, LANES - 2 * ML_HEADS)).reshape(1, LANES)
        wq = ML_HEADS * ML_HEAD_DIM
        cw_k = ml_conv_w[l][:, wq:].astype(F32)
        cb_k = ml_conv_b[l][wq:].astype(F32)
        y_ml = _mlstm(seq(mq), mkt, seq(mv), seq(mo), seq(gates),
                      _pad_rows(ml_conv_w[l][:, :wq], SUBLANES), _row(ml_conv_b[l][:wq]),
                      jnp.broadcast_to(cw_k[:, :, None], cw_k.shape + (ML_CHUNK,)),
                      jnp.broadcast_to(cb_k[:, None], cb_k.shape + (ML_CHUNK,)),
                      gate_bias, _row(ml_norm_g[l]))

        lam_p = jnp.stack([lam_q1[l], lam_k1[l], lam_q2[l], lam_k2[l]]).astype(F32)
        y_da = _diff_attention(seq(dq), seq(dk), dvt, lam_p,
                               da_norm_g[l].astype(F32).reshape(-1, 1), lambda_init)

        x2 = _out_mlp_resident(y_conv.reshape(n, -1), y_ml.reshape(n, -1), y_da.reshape(n, -1), x2,
                      w_out[l].astype(BF16), _row(ln1_g[l]), _row(ln1_b[l]),
                      w_up[l].astype(BF16), w_down[l].astype(BF16),
                      _row(ln2_g[l]), _row(ln2_b[l]))
    return x2.reshape(bsz, s, d)
```
